```python
import math
import jax, jax.numpy as jnp
from jax import lax
import numpy as np

D_MODEL = 1024
BATCH = 1
SEQ = 16384
DEPTH = 1

HEAD_DIM = 64
HEADS_PER_GROUP = D_MODEL // 128
ATT_GROUPS = ((128, 1), (512, 4), (2048, 16))
N_GROUPS = 3
ATT_QKV_WIDTH = N_GROUPS * HEADS_PER_GROUP * HEAD_DIM
ATT_OUT_WIDTH = HEADS_PER_GROUP * HEAD_DIM
CONV_WIDTH = D_MODEL
CONV_K = 3
DN_ALPHA = (2.0 * DEPTH) ** 0.25
DN_BETA = (8.0 * DEPTH) ** -0.25
LN_EPS = 1e-5
SECTION_WIDTHS = (ATT_QKV_WIDTH, ATT_QKV_WIDTH, ATT_QKV_WIDTH, ATT_OUT_WIDTH,
                  CONV_WIDTH, CONV_WIDTH, CONV_WIDTH, CONV_WIDTH, 2 * D_MODEL)
IN_WIDTH = 4 * ATT_QKV_WIDTH // 4 * 3 + ATT_OUT_WIDTH + 4 * CONV_WIDTH + 2 * D_MODEL
V_START = 2 * ATT_QKV_WIDTH
V_END = 3 * ATT_QKV_WIDTH

kernel_name = "hybrid_conv_dilated_swa_deepnorm"


def layer_norm(x, g, b):
    xf = x.astype(jnp.float32)
    mu = jnp.mean(xf, axis=-1, keepdims=True)
    var = jnp.mean(jnp.square(xf - mu), axis=-1, keepdims=True)
    y = (xf - mu) * lax.rsqrt(var + LN_EPS) * g.astype(jnp.float32) + b.astype(jnp.float32)
    return y.astype(x.dtype)


def causal_short_conv(u, w):
    s = u.shape[1]
    up = jnp.pad(u, ((0, 0), (CONV_K - 1, 0), (0, 0)))
    y = w[0] * up[:, 0:s]
    for j in range(1, CONV_K):
        y = y + w[j] * up[:, j:j + s]
    return y


def dilated_window_attention(q, k, v, window, dilation):
    b, s, h, dh = q.shape
    n_win = window // dilation
    blk = n_win
    length = s // dilation
    n_blk = -(-length // blk)
    pad = n_blk * blk - length

    def to_blocks(t):
        t = t.reshape(b, length, dilation, h, dh).transpose(0, 2, 1, 3, 4)
        t = jnp.pad(t, ((0, 0), (0, 0), (0, pad), (0, 0), (0, 0)))
        return t.reshape(b, dilation, n_blk, blk, h, dh)

    def with_prev(t):
        prev = jnp.pad(t, ((0, 0), (0, 0), (1, 0), (0, 0), (0, 0), (0, 0)))[:, :, :-1]
        return jnp.concatenate([prev, t], axis=3)

    qb = to_blocks(q)
    kc = with_prev(to_blocks(k))
    vc = with_prev(to_blocks(v))

    scores = jnp.einsum('brnqhd,brnkhd->brnhqk', qb, kc,
                        preferred_element_type=jnp.float32) * (dh ** -0.5)
    q_idx = jnp.arange(blk)[:, None]
    k_idx = jnp.arange(2 * blk)[None, :] - blk
    rel = q_idx - k_idx
    blk_start = (jnp.arange(n_blk) * blk)[:, None, None]
    valid = (rel >= 0) & (rel <= n_win) & (blk_start + k_idx >= 0)
    scores = jnp.where(valid[:, None], scores, -jnp.inf)
    lse = jax.nn.logsumexp(scores, axis=-1)
    probs = jnp.exp(scores - lse[..., None])
    out = jnp.einsum('brnhqk,brnkhd->brnqhd', probs, vc.astype(jnp.float32))

    def from_blocks(t):
        rest = t.shape[4:]
        t = t.reshape((b, dilation, n_blk * blk) + rest)[:, :, :length]
        return jnp.swapaxes(t, 1, 2).reshape((b, s) + rest)

    return from_blocks(out), from_blocks(lse.transpose(0, 1, 2, 4, 3))


def hybrid_layer(x, w_in, conv_w, w_conv_out, w_att_out, b_gate, w_o, ln_g, ln_b):
    b, s, _ = x.shape
    proj = jnp.einsum('bsd,de->bse', x, w_in)
    idx = np.cumsum(SECTION_WIDTHS)[:-1]
    q, k, v, g_att, h_c, b_c, c_c, g_conv, gate_logits = jnp.split(proj, idx, axis=-1)

    q = q.reshape(b, s, N_GROUPS, HEADS_PER_GROUP, HEAD_DIM)
    k = k.reshape(b, s, N_GROUPS, HEADS_PER_GROUP, HEAD_DIM)
    v = v.reshape(b, s, N_GROUPS, HEADS_PER_GROUP, HEAD_DIM)
    outs, lses = [], []
    for gi, (window, dilation) in enumerate(ATT_GROUPS):
        o, l = dilated_window_attention(q[:, :, gi], k[:, :, gi], v[:, :, gi], window, dilation)
        outs.append(o)
        lses.append(l)
    outs = jnp.stack(outs)
    mix = jax.nn.softmax(jnp.stack(lses), axis=0)
    att = jnp.sum(mix[..., None] * outs, axis=0).reshape(b, s, ATT_OUT_WIDTH).astype(x.dtype)
    y_att = jnp.einsum('bse,ed->bsd', att * jax.nn.silu(g_att), w_att_out)

    conv = causal_short_conv(c_c * h_c, conv_w)
    y_conv = jnp.einsum('bse,ed->bsd', (b_c * conv) * jax.nn.silu(g_conv), w_conv_out)

    gates = jax.nn.sigmoid(gate_logits + b_gate)
    g_c, g_a = jnp.split(gates, 2, axis=-1)
    merged = g_c * y_conv + g_a * y_att
    out = jnp.einsum('bsd,de->bse', merged, w_o)

    return layer_norm(DN_ALPHA * x + out, ln_g, ln_b)


def setup_inputs(seed: int = 0) -> dict:
    key = jax.random.key(seed)
    ks = jax.random.split(key, 10)
    x = jax.random.normal(ks[0], (BATCH, SEQ, D_MODEL), jnp.float32)
    w_in = jax.random.normal(ks[1], (DEPTH, D_MODEL, IN_WIDTH), jnp.float32) * D_MODEL ** -0.5
    w_in = w_in.at[:, :, V_START:V_END].multiply(DN_BETA)
    conv_w = jax.random.normal(ks[2], (DEPTH, CONV_K, CONV_WIDTH), jnp.float32) * CONV_K ** -0.5
    w_conv_out = jax.random.normal(ks[3], (DEPTH, CONV_WIDTH, D_MODEL), jnp.float32) * (CONV_WIDTH ** -0.5 * DN_BETA)
    w_att_out = jax.random.normal(ks[4], (DEPTH, ATT_OUT_WIDTH, D_MODEL), jnp.float32) * (ATT_OUT_WIDTH ** -0.5 * DN_BETA)
    b_gate = jax.random.normal(ks[5], (DEPTH, 2 * D_MODEL), jnp.float32) * 0.1
    w_o = jax.random.normal(ks[6], (DEPTH, D_MODEL, D_MODEL), jnp.float32) * (D_MODEL ** -0.5 * DN_BETA)
    ln_g = 1.0 + 0.02 * jax.random.normal(ks[7], (DEPTH, D_MODEL), jnp.float32)
    ln_b = 0.02 * jax.random.normal(ks[8], (DEPTH, D_MODEL), jnp.float32)
    return {"x": x, "w_in": w_in, "conv_w": conv_w, "w_conv_out": w_conv_out,
            "w_att_out": w_att_out, "b_gate": b_gate, "w_o": w_o, "ln_g": ln_g, "ln_b": ln_b}


def reference(x, w_in, conv_w, w_conv_out, w_att_out, b_gate, w_o, ln_g, ln_b):
    h = x
    for layer in range(DEPTH):
        h = hybrid_layer(h, w_in[layer], conv_w[layer], w_conv_out[layer], w_att_out[layer],
                         b_gate[layer], w_o[layer], ln_g[layer], ln_b[layer])
    return h
```

```python
import functools

import jax
import jax.numpy as jnp
from jax import lax
from jax.experimental import pallas as pl
from jax.experimental.pallas import tpu as pltpu

D_MODEL = 1024
HEAD_DIM = 64
HEADS_PER_GROUP = 8
ATT_GROUPS = ((128, 1), (512, 4), (2048, 16))
N_GROUPS = len(ATT_GROUPS)
GROUP_WIDTH = HEADS_PER_GROUP * HEAD_DIM
ATT_QKV_WIDTH = N_GROUPS * GROUP_WIDTH
CONV_WIDTH = D_MODEL
CONV_K = 3
LN_EPS = 1e-5
LANES = 128

ATT_BLOCK = 128
assert all(w // d == ATT_BLOCK for w, d in ATT_GROUPS)
SUPER_BLOCK = ATT_BLOCK * max(d for _, d in ATT_GROUPS)
TILES_PER_SUPER = SUPER_BLOCK // ATT_BLOCK

QKV_ROW_TILE = 1024
TAIL_ROW_TILE = 512
CONV_HALO = 8
MERGE_ROWS = 16
HEAD_PAIRS = HEADS_PER_GROUP // 2
assert 2 * HEAD_DIM == LANES

VMEM_LIMIT_BYTES = 56 * 1024 * 1024

F32 = jnp.float32
BF16 = jnp.bfloat16


def _qkv_kernel(x_ref, w_ref, o1_ref, o2_ref, o3_ref, xp_ref, xl_ref):
  tm = x_ref.shape[0]

  @pl.when(pl.program_id(1) == 0)
  def _():
    for c in range(D_MODEL // LANES):
      xl_ref[c] = x_ref[:, c * LANES:(c + 1) * LANES]
    for g, (_, d) in enumerate(ATT_GROUPS):
      n = tm // d
      if d == 1:
        xp_ref[g] = x_ref[...].astype(BF16)
      else:
        for r in range(d):
          for c in range(D_MODEL // LANES):
            xp_ref[g, r * n:(r + 1) * n, c * LANES:(c + 1) * LANES] = (
                xl_ref[c, pl.ds(r, n, stride=d), :].astype(BF16))

  for g, o_ref in enumerate((o1_ref, o2_ref, o3_ref)):
    d = ATT_GROUPS[g][1]
    n = tm // d
    res = jnp.dot(xp_ref[g], w_ref[0, g], preferred_element_type=F32).astype(BF16)
    for r in range(d):
      o_ref[r] = res[r * n:(r + 1) * n]


def _qkv_projection(x2d, w_qkv):
  s = x2d.shape[0]
  tm = QKV_ROW_TILE
  out_shape = [jax.ShapeDtypeStruct((d, s // d, ATT_QKV_WIDTH), BF16) for _, d in ATT_GROUPS]
  out_specs = [
      pl.BlockSpec((d, tm // d, GROUP_WIDTH), lambda i, j: (0, i, j)) for _, d in ATT_GROUPS
  ]
  return pl.pallas_call(
      _qkv_kernel,
      out_shape=out_shape,
      grid=(s // tm, 3),
      in_specs=[
          pl.BlockSpec((tm, D_MODEL), lambda i, j: (i, 0)),
          pl.BlockSpec((1, N_GROUPS, D_MODEL, GROUP_WIDTH), lambda i, j: (j, 0, 0, 0)),
      ],
      out_specs=out_specs,
      scratch_shapes=[
          pltpu.VMEM((N_GROUPS, tm, D_MODEL), BF16),
          pltpu.VMEM((D_MODEL // LANES, tm, LANES), F32),
      ],
      compiler_params=pltpu.CompilerParams(
          dimension_semantics=("parallel", "arbitrary"),
          vmem_limit_bytes=VMEM_LIMIT_BYTES,
      ),
      name="qkv_projection",
  )(x2d, w_qkv)


def _attention_tile(q_ref, kp_ref, kc_ref, vp_ref, vc_ref, has_prev):
  nq = ATT_BLOCK
  row = lax.broadcasted_iota(jnp.int32, (2 * nq, 2 * nq), 0) % nq
  col = lax.broadcasted_iota(jnp.int32, (2 * nq, 2 * nq), 1)
  first_col = jnp.maximum(row, jnp.where(has_prev, 0, nq))
  valid = jnp.logical_and(col >= first_col, col - nq <= row)
  lane = lax.broadcasted_iota(jnp.int32, (nq, 2 * HEAD_DIM), 1)
  low_half = lane < HEAD_DIM

  outs, lses = [], []
  for p in range(HEAD_PAIRS):
    sl = slice(p * 2 * HEAD_DIM, (p + 1) * 2 * HEAD_DIM)
    q2 = q_ref[0, :, sl]
    k2 = jnp.concatenate([kp_ref[0, :, sl], kc_ref[0, :, sl]], axis=0)
    v2 = jnp.concatenate([vp_ref[0, :, sl], vc_ref[0, :, sl]], axis=0)
    zero = jnp.zeros_like(q2)
    qs = jnp.concatenate([jnp.where(low_half, q2, zero), jnp.where(low_half, zero, q2)], axis=0)
    sc = lax.dot_general(qs, k2, (((1,), (1,)), ((), ())), preferred_element_type=F32)
    sc = jnp.where(valid, sc, -jnp.inf)
    m = jnp.max(sc, axis=1, keepdims=True)
    e = jnp.exp(sc - m)
    l = jnp.sum(e, axis=1, keepdims=True)
    o = jnp.dot(e.astype(BF16), v2, preferred_element_type=F32)
    o = o / l
    lse = jnp.broadcast_to(m + jnp.log(l), o.shape)
    outs.append(jnp.where(low_half, o[:nq], o[nq:]))
    lses.append(jnp.where(low_half, lse[:nq], lse[nq:]))
  return outs, lses


def _attention_kernel(*refs):
  in_refs = refs[:5 * N_GROUPS]
  o_ref = refs[5 * N_GROUPS]
  acc_ref, lse_ref = refs[5 * N_GROUPS + 1:]
  sb = pl.program_id(0)
  j = pl.program_id(1)
  n_sub = TILES_PER_SUPER

  for g, (_, d) in enumerate(ATT_GROUPS):
    blocks_per_res = n_sub // d
    blk = sb * blocks_per_res + j % blocks_per_res
    o, lse = _attention_tile(*in_refs[5 * g:5 * g + 5], blk > 0)
    if d == 1:
      rows = pl.ds(pl.multiple_of(j * ATT_BLOCK, ATT_BLOCK), ATT_BLOCK)
    else:
      r = j // blocks_per_res
      b = j % blocks_per_res
      rows = pl.ds(b * ATT_BLOCK * d + r, ATT_BLOCK, stride=d)
    for p in range(HEAD_PAIRS):
      acc_ref[g, p, rows, :] = o[p]
      lse_ref[g, p, rows, :] = lse[p]

  @pl.when(j == n_sub - 1)
  def _():
    def body(c, carry):
      rows = pl.ds(pl.multiple_of(c * MERGE_ROWS, MERGE_ROWS), MERGE_ROWS)
      for p in range(HEAD_PAIRS):
        ls = [lse_ref[g, p, rows, :] for g in range(N_GROUPS)]
        m = functools.reduce(jnp.maximum, ls)
        ws = [jnp.exp(x - m) for x in ls]
        num = functools.reduce(jnp.add, [w * acc_ref[g, p, rows, :] for g, w in enumerate(ws)])
        den = functools.reduce(jnp.add, ws)
        o_ref[rows, p * LANES:(p + 1) * LANES] = num / den
      return carry
    lax.fori_loop(0, SUPER_BLOCK // MERGE_ROWS, body, 0)


def _attention(qkv_groups):
  s = qkv_groups[0].shape[1]
  n_sub = TILES_PER_SUPER
  in_specs, args = [], []
  for (_, d), arr in zip(ATT_GROUPS, qkv_groups):
    bpr = n_sub // d

    def res(j, bpr=bpr):
      return j // bpr

    def blk(sb, j, bpr=bpr):
      return sb * bpr + j % bpr

    shape = (1, ATT_BLOCK, GROUP_WIDTH)
    in_specs += [
        pl.BlockSpec(shape, lambda sb, j, res=res, blk=blk: (res(j), blk(sb, j), 0)),
        pl.BlockSpec(shape, lambda sb, j, res=res, blk=blk: (res(j), jnp.maximum(blk(sb, j) - 1, 0), 1)),
        pl.BlockSpec(shape, lambda sb, j, res=res, blk=blk: (res(j), blk(sb, j), 1)),
        pl.BlockSpec(shape, lambda sb, j, res=res, blk=blk: (res(j), jnp.maximum(blk(sb, j) - 1, 0), 2)),
        pl.BlockSpec(shape, lambda sb, j, res=res, blk=blk: (res(j), blk(sb, j), 2)),
    ]
    args += [arr] * 5
  return pl.pallas_call(
      _attention_kernel,
      out_shape=jax.ShapeDtypeStruct((s, GROUP_WIDTH), F32),
      grid=(s // SUPER_BLOCK, n_sub),
      in_specs=in_specs,
      out_specs=pl.BlockSpec((SUPER_BLOCK, GROUP_WIDTH), lambda sb, j: (sb, 0)),
      scratch_shapes=[
          pltpu.VMEM((N_GROUPS, HEAD_PAIRS, SUPER_BLOCK, LANES), F32),
          pltpu.VMEM((N_GROUPS, HEAD_PAIRS, SUPER_BLOCK, LANES), F32),
      ],
      compiler_params=pltpu.CompilerParams(
          dimension_semantics=("parallel", "arbitrary"),
          vmem_limit_bytes=VMEM_LIMIT_BYTES,
      ),
      name="dilated_attention",
  )(*args)


def _silu(x):
  return x * jax.nn.sigmoid(x)


def _tail_kernel(alpha, x_ref, att_ref, w_ref, cw_ref, wco_ref, wao_ref, bg_ref, wo_ref,
                 lg_ref, lb_ref, o_ref, u_ref):
  tm = x_ref.shape[0]
  c0 = GROUP_WIDTH
  cw = CONV_WIDTH
  x = x_ref[...]
  xb = x.astype(BF16)

  def proj(lo, hi):
    return jnp.dot(xb, w_ref[:, lo:hi], preferred_element_type=F32)

  @pl.when(pl.program_id(0) == 0)
  def _():
    u_ref[0:CONV_HALO, :] = jnp.zeros((CONV_HALO, cw), F32)

  u_ref[CONV_HALO:, :] = proj(c0 + 2 * cw, c0 + 3 * cw) * proj(c0, c0 + cw)
  conv = cw_ref[CONV_K - 1:CONV_K, :] * u_ref[CONV_HALO:, :]
  for k in range(CONV_K - 1):
    off = CONV_HALO - (CONV_K - 1 - k)
    conv = conv + cw_ref[k:k + 1, :] * u_ref[off:off + tm, :]
  u_ref[0:CONV_HALO, :] = u_ref[tm:tm + CONV_HALO, :]
  a_conv = (proj(c0 + cw, c0 + 2 * cw) * conv) * _silu(proj(c0 + 3 * cw, c0 + 4 * cw))
  y_conv = jnp.dot(a_conv.astype(BF16), wco_ref[...], preferred_element_type=F32)

  a_att = att_ref[...] * _silu(proj(0, c0))
  y_att = jnp.dot(a_att.astype(BF16), wao_ref[...], preferred_element_type=F32)

  g0 = c0 + 4 * cw
  g_c = jax.nn.sigmoid(proj(g0, g0 + D_MODEL) + bg_ref[:, :D_MODEL])
  g_a = jax.nn.sigmoid(proj(g0 + D_MODEL, g0 + 2 * D_MODEL) + bg_ref[:, D_MODEL:])
  merged = g_c * y_conv + g_a * y_att
  out = jnp.dot(merged.astype(BF16), wo_ref[...], preferred_element_type=F32)
  y = alpha * x + out
  mu = jnp.mean(y, axis=-1, keepdims=True)
  yc = y - mu
  var = jnp.mean(yc * yc, axis=-1, keepdims=True)
  o_ref[...] = yc * lax.rsqrt(var + LN_EPS) * lg_ref[...] + lb_ref[...]


def _tail(x2d, att, w_rest, conv_w, w_conv_out, w_att_out, b_gate, w_o, ln_g, ln_b, alpha):
  s = x2d.shape[0]
  tm = TAIL_ROW_TILE

  def whole(arr):
    return pl.BlockSpec(arr.shape, lambda i: (0,) * arr.ndim, pipeline_mode=pl.Buffered(1))

  weights = (w_rest, conv_w, w_conv_out, w_att_out, b_gate, w_o, ln_g, ln_b)
  return pl.pallas_call(
      functools.partial(_tail_kernel, alpha),
      out_shape=jax.ShapeDtypeStruct((s, D_MODEL), F32),
      grid=(s // tm,),
      in_specs=[
          pl.BlockSpec((tm, D_MODEL), lambda i: (i, 0)),
          pl.BlockSpec((tm, GROUP_WIDTH), lambda i: (i, 0)),
      ] + [whole(w) for w in weights],
      out_specs=pl.BlockSpec((tm, D_MODEL), lambda i: (i, 0)),
      scratch_shapes=[pltpu.VMEM((CONV_HALO + tm, CONV_WIDTH), F32)],
      compiler_params=pltpu.CompilerParams(
          dimension_semantics=("arbitrary",),
          vmem_limit_bytes=VMEM_LIMIT_BYTES,
      ),
      name="conv_merge_norm",
  )(x2d, att, *weights)


def _layer(x2d, w_in, conv_w, w_conv_out, w_att_out, b_gate, w_o, ln_g, ln_b, alpha):
  scale = HEAD_DIM ** -0.5
  sections = []
  for sec in range(3):
    w_sec = w_in[:, sec * ATT_QKV_WIDTH:(sec + 1) * ATT_QKV_WIDTH]
    if sec == 0:
      w_sec = w_sec * scale
    sections.append(w_sec.reshape(D_MODEL, N_GROUPS, GROUP_WIDTH).transpose(1, 0, 2))
  w_qkv = jnp.stack(sections).astype(BF16)
  w_rest = w_in[:, 3 * ATT_QKV_WIDTH:].astype(BF16)

  qkv_groups = _qkv_projection(x2d, w_qkv)
  att = _attention(qkv_groups)
  return _tail(x2d, att, w_rest, conv_w, w_conv_out.astype(BF16), w_att_out.astype(BF16),
               b_gate.reshape(1, -1), w_o.astype(BF16), ln_g.reshape(1, -1), ln_b.reshape(1, -1),
               alpha)


def kernel(x, w_in, conv_w, w_conv_out, w_att_out, b_gate, w_o, ln_g, ln_b):
  batch, seq, d_model = x.shape
  depth = w_in.shape[0]
  assert d_model == D_MODEL and seq % SUPER_BLOCK == 0
  assert HEAD_DIM ** -0.5 == 0.125
  alpha = (2.0 * depth) ** 0.25
  outs = []
  for b in range(batch):
    h = x[b]
    for layer in range(depth):
      h = _layer(h, w_in[layer], conv_w[layer], w_conv_out[layer], w_att_out[layer],
                 b_gate[layer], w_o[layer], ln_g[layer], ln_b[layer], alpha)
    outs.append(h)
  return jnp.stack(outs)
```

```python
import functools

import jax
import jax.numpy as jnp
from jax import lax
from jax.experimental import pallas as pl
from jax.experimental.pallas import tpu as pltpu

D_MODEL = 1024
HEAD_DIM = 64
HEADS_PER_GROUP = 8
ATT_GROUPS = ((128, 1), (512, 4), (2048, 16))
N_GROUPS = len(ATT_GROUPS)
GROUP_WIDTH = HEADS_PER_GROUP * HEAD_DIM
ATT_QKV_WIDTH = N_GROUPS * GROUP_WIDTH
CONV_WIDTH = D_MODEL
CONV_K = 3
LN_EPS = 1e-5
LANES = 128

ATT_BLOCK = 128
assert all(w // d == ATT_BLOCK for w, d in ATT_GROUPS)
SUPER_BLOCK = ATT_BLOCK * max(d for _, d in ATT_GROUPS)
TILES_PER_SUPER = SUPER_BLOCK // ATT_BLOCK
ATT_TILES_PER_STEP = 2

QKV_ROW_TILE = 1024
TAIL_ROW_TILE = 512
CONV_HALO = 8
MERGE_ROWS = 16
HEAD_PAIRS = HEADS_PER_GROUP // 2
assert 2 * HEAD_DIM == LANES

VMEM_LIMIT_BYTES = 56 * 1024 * 1024

F32 = jnp.float32
BF16 = jnp.bfloat16

Q_OFF, K_OFF, V_OFF = 0, ATT_QKV_WIDTH, 2 * ATT_QKV_WIDTH
G_ATT_OFF = 3 * ATT_QKV_WIDTH
H_OFF = G_ATT_OFF + GROUP_WIDTH
B_OFF = H_OFF + CONV_WIDTH
C_OFF = B_OFF + CONV_WIDTH
G_CONV_OFF = C_OFF + CONV_WIDTH
GATE_OFF = G_CONV_OFF + CONV_WIDTH


def _qkv_kernel(x_ref, wq_ref, wk_ref, wv_ref, o1_ref, o2_ref, o3_ref, xp_ref, xl_ref):
  tm = x_ref.shape[0]
  out_refs = (o1_ref, o2_ref, o3_ref)
  scale = HEAD_DIM ** -0.5

  def permute_rows(g):
    d = ATT_GROUPS[g][1]
    n = tm // d
    for r in range(d):
      for c in range(D_MODEL // LANES):
        xp_ref[g, r * n:(r + 1) * n, c * LANES:(c + 1) * LANES] = (
            xl_ref[c, pl.ds(r, n, stride=d), :].astype(BF16))

  def project(g):
    d = ATT_GROUPS[g][1]
    n = tm // d
    for sec, w_ref in enumerate((wq_ref, wk_ref, wv_ref)):
      res = jnp.dot(xp_ref[g], w_ref[...], preferred_element_type=F32)
      if sec == 0:
        res = res * scale
      res = res.astype(BF16)
      for r in range(d):
        out_refs[g][r, :, sec * GROUP_WIDTH:(sec + 1) * GROUP_WIDTH] = res[r * n:(r + 1) * n]

  for g in range(N_GROUPS):
    @pl.when(pl.program_id(1) == g)
    def _(g=g):
      if g == 0:
        assert ATT_GROUPS[0][1] == 1
        xp_ref[0] = x_ref[...].astype(BF16)
        for c in range(D_MODEL // LANES):
          xl_ref[c] = x_ref[:, c * LANES:(c + 1) * LANES]
      project(g)
      if g + 1 < N_GROUPS:
        permute_rows(g + 1)


def _qkv_projection(x2d, w_bf16):
  s = x2d.shape[0]
  tm = QKV_ROW_TILE

  def w_spec(off):
    return pl.BlockSpec((D_MODEL, GROUP_WIDTH), lambda i, g: (0, off // GROUP_WIDTH + g))

  return pl.pallas_call(
      _qkv_kernel,
      out_shape=[jax.ShapeDtypeStruct((d, s // d, ATT_QKV_WIDTH), BF16) for _, d in ATT_GROUPS],
      grid=(s // tm, N_GROUPS),
      in_specs=[pl.BlockSpec((tm, D_MODEL), lambda i, g: (i, 0)),
                w_spec(Q_OFF), w_spec(K_OFF), w_spec(V_OFF)],
      out_specs=[pl.BlockSpec((d, tm // d, ATT_QKV_WIDTH), lambda i, g: (0, i, 0))
                 for _, d in ATT_GROUPS],
      scratch_shapes=[
          pltpu.VMEM((N_GROUPS, tm, D_MODEL), BF16),
          pltpu.VMEM((D_MODEL // LANES, tm, LANES), F32),
      ],
      compiler_params=pltpu.CompilerParams(
          dimension_semantics=("arbitrary", "arbitrary"),
          vmem_limit_bytes=VMEM_LIMIT_BYTES,
      ),
      name="qkv_projection",
  )(x2d, w_bf16, w_bf16, w_bf16)


def _attention_tile(cur, prev, bias):
  nq = ATT_BLOCK
  lane = lax.broadcasted_iota(jnp.int32, (nq, LANES), 1)
  low_half = lane < HEAD_DIM
  ones = jnp.ones((2 * nq, LANES), BF16)

  def pair_tile(x):
    return jnp.where(low_half, x[:nq], x[nq:])

  outs, maxes, sums = [], [], []
  for p in range(HEAD_PAIRS):
    q2 = cur(0, p)
    k2 = jnp.concatenate([prev(1, p), cur(1, p)], axis=0)
    v2 = jnp.concatenate([prev(2, p), cur(2, p)], axis=0)
    zero = jnp.zeros_like(q2)
    qs = jnp.concatenate([jnp.where(low_half, q2, zero), jnp.where(low_half, zero, q2)], axis=0)
    sc = lax.dot_general(qs, k2, (((1,), (1,)), ((), ())), preferred_element_type=F32)
    sc = sc + bias
    m = jnp.max(sc, axis=1, keepdims=True)
    e = jnp.exp(sc - m).astype(BF16)
    ol = jnp.dot(e, jnp.concatenate([v2, ones], axis=1), preferred_element_type=F32)
    outs.append(pair_tile(ol[:, :LANES]))
    sums.append(pair_tile(ol[:, LANES:]))
    maxes.append(pair_tile(jnp.broadcast_to(m, (2 * nq, LANES))))
  return outs, maxes, sums


def _step_tiles(d, sb, j):
  nt = ATT_TILES_PER_STEP
  bpr = TILES_PER_SUPER // d
  if bpr >= nt:
    assert bpr % nt == 0
    spr = bpr // nt
    b0 = (j % spr) * nt
    return sb * bpr + b0, (lambda t: j // spr), (lambda t: b0 + t)
  assert bpr == 1
  return sb, (lambda t: j * nt + t), (lambda t: 0)


def _attention_kernel(*refs):
  in_refs = refs[:2 * N_GROUPS]
  o_ref = refs[2 * N_GROUPS]
  acc_ref, max_ref, sum_ref, bias_ref = refs[2 * N_GROUPS + 1:]
  sb = pl.program_id(0)
  j = pl.program_id(1)
  nq = ATT_BLOCK
  nt = ATT_TILES_PER_STEP

  @pl.when(jnp.logical_and(sb == 0, j == 0))
  def _():
    row = lax.broadcasted_iota(jnp.int32, (2 * nq, 2 * nq), 0) % nq
    col = lax.broadcasted_iota(jnp.int32, (2 * nq, 2 * nq), 1)
    for hp in range(2):
      first_col = row if hp else jnp.maximum(row, nq)
      valid = jnp.logical_and(col >= first_col, col - nq <= row)
      bias_ref[hp] = jnp.where(valid, 0.0, -jnp.inf).astype(F32)

  def cols(sec, p):
    return pl.ds(sec * GROUP_WIDTH + p * LANES, LANES)

  for g, (_, d) in enumerate(ATT_GROUPS):
    cur_ref, prev_ref = in_refs[2 * g:2 * g + 2]
    first_blk, res, loc = _step_tiles(d, sb, j)
    for t in range(nt):
      if cur_ref.shape[0] == 1:
        cur = lambda sec, p, t=t: cur_ref[0, t * nq:(t + 1) * nq, cols(sec, p)]
        if t == 0:
          prev = lambda sec, p: prev_ref[0, :, cols(sec, p)]
          has_prev = (first_blk > 0).astype(jnp.int32)
        else:
          prev = lambda sec, p, t=t: cur_ref[0, (t - 1) * nq:t * nq, cols(sec, p)]
          has_prev = 1
      else:
        cur = lambda sec, p, t=t: cur_ref[t, :, cols(sec, p)]
        prev = lambda sec, p, t=t: prev_ref[t, :, cols(sec, p)]
        has_prev = (first_blk > 0).astype(jnp.int32)
      o, m, l = _attention_tile(cur, prev, bias_ref[has_prev])
      r, b = res(t), loc(t)
      if d == 1:
        rows = pl.ds(pl.multiple_of(b * nq, nq), nq)
      else:
        rows = pl.ds(b * nq * d + r, nq, stride=d)
      for p in range(HEAD_PAIRS):
        acc_ref[g, p, rows, :] = o[p]
        max_ref[g, p, rows, :] = m[p]
        sum_ref[g, p, rows, :] = l[p]

  @pl.when(j == TILES_PER_SUPER // nt - 1)
  def _():
    def body(c, carry):
      rows = pl.ds(pl.multiple_of(c * MERGE_ROWS, MERGE_ROWS), MERGE_ROWS)
      for p in range(HEAD_PAIRS):
        ms = [max_ref[g, p, rows, :] for g in range(N_GROUPS)]
        top = functools.reduce(jnp.maximum, ms)
        ws = [jnp.exp(m - top) for m in ms]
        num = functools.reduce(jnp.add, [w * acc_ref[g, p, rows, :] for g, w in enumerate(ws)])
        den = functools.reduce(jnp.add, [w * sum_ref[g, p, rows, :] for g, w in enumerate(ws)])
        o_ref[rows, p * LANES:(p + 1) * LANES] = num / den
      return carry
    lax.fori_loop(0, SUPER_BLOCK // MERGE_ROWS, body, 0)


def _attention(qkv_groups):
  s = qkv_groups[0].shape[1]
  nt = ATT_TILES_PER_STEP
  in_specs, args = [], []
  for (_, d), arr in zip(ATT_GROUPS, qkv_groups):
    if TILES_PER_SUPER // d >= nt:
      def cur_map(sb, j, d=d):
        first_blk, res, _ = _step_tiles(d, sb, j)
        return res(0), first_blk // nt, 0

      def prev_map(sb, j, d=d):
        first_blk, res, _ = _step_tiles(d, sb, j)
        return res(0), jnp.maximum(first_blk - 1, 0), 0

      in_specs += [pl.BlockSpec((1, nt * ATT_BLOCK, ATT_QKV_WIDTH), cur_map),
                   pl.BlockSpec((1, ATT_BLOCK, ATT_QKV_WIDTH), prev_map)]
    else:
      def cur_map(sb, j, d=d):
        first_blk, res, _ = _step_tiles(d, sb, j)
        return res(0) // nt, first_blk, 0

      def prev_map(sb, j, d=d):
        first_blk, res, _ = _step_tiles(d, sb, j)
        return res(0) // nt, jnp.maximum(first_blk - 1, 0), 0

      in_specs += [pl.BlockSpec((nt, ATT_BLOCK, ATT_QKV_WIDTH), cur_map),
                   pl.BlockSpec((nt, ATT_BLOCK, ATT_QKV_WIDTH), prev_map)]
    args += [arr] * 2
  scratch = pltpu.VMEM((N_GROUPS, HEAD_PAIRS, SUPER_BLOCK, LANES), F32)
  return pl.pallas_call(
      _attention_kernel,
      out_shape=jax.ShapeDtypeStruct((s, GROUP_WIDTH), F32),
      grid=(s // SUPER_BLOCK, TILES_PER_SUPER // nt),
      in_specs=in_specs,
      out_specs=pl.BlockSpec((SUPER_BLOCK, GROUP_WIDTH), lambda sb, j: (sb, 0)),
      scratch_shapes=[scratch] * 3 + [pltpu.VMEM((2, 2 * ATT_BLOCK, 2 * ATT_BLOCK), F32)],
      compiler_params=pltpu.CompilerParams(
          dimension_semantics=("arbitrary", "arbitrary"),
          vmem_limit_bytes=VMEM_LIMIT_BYTES,
      ),
      name="dilated_attention",
  )(*args)


def _silu(x):
  return x * jax.nn.sigmoid(x)


def _tail_kernel(alpha, x_ref, att_ref, wga_ref, wh_ref, wb_ref, wc_ref, wgc_ref, wg1_ref, wg2_ref,
                 cw_ref, wco_ref, wao_ref, bg_ref, wo_ref, lg_ref, lb_ref, o_ref, u_ref):
  tm = x_ref.shape[0]
  x = x_ref[...]
  xb = x.astype(BF16)

  def proj(w_ref):
    return jnp.dot(xb, w_ref[...], preferred_element_type=F32)

  @pl.when(pl.program_id(0) == 0)
  def _():
    u_ref[0:CONV_HALO, :] = jnp.zeros((CONV_HALO, CONV_WIDTH), F32)

  u_ref[CONV_HALO:, :] = proj(wc_ref) * proj(wh_ref)
  conv = cw_ref[CONV_K - 1:CONV_K, :] * u_ref[CONV_HALO:, :]
  for k in range(CONV_K - 1):
    off = CONV_HALO - (CONV_K - 1 - k)
    conv = conv + cw_ref[k:k + 1, :] * u_ref[off:off + tm, :]
  u_ref[0:CONV_HALO, :] = u_ref[tm:tm + CONV_HALO, :]
  a_conv = (proj(wb_ref) * conv) * _silu(proj(wgc_ref))
  y_conv = jnp.dot(a_conv.astype(BF16), wco_ref[...], preferred_element_type=F32)

  a_att = att_ref[...] * _silu(proj(wga_ref))
  y_att = jnp.dot(a_att.astype(BF16), wao_ref[...], preferred_element_type=F32)

  g_c = jax.nn.sigmoid(proj(wg1_ref) + bg_ref[:, :D_MODEL])
  g_a = jax.nn.sigmoid(proj(wg2_ref) + bg_ref[:, D_MODEL:])
  merged = g_c * y_conv + g_a * y_att
  out = jnp.dot(merged.astype(BF16), wo_ref[...], preferred_element_type=F32)
  y = alpha * x + out
  mu = jnp.mean(y, axis=-1, keepdims=True)
  yc = y - mu
  var = jnp.mean(yc * yc, axis=-1, keepdims=True)
  o_ref[...] = yc * lax.rsqrt(var + LN_EPS) * lg_ref[...] + lb_ref[...]


def _tail(x2d, att, w_bf16, conv_w, w_conv_out, w_att_out, b_gate, w_o, ln_g, ln_b, alpha):
  s = x2d.shape[0]
  tm = TAIL_ROW_TILE

  def whole(arr):
    return pl.BlockSpec(arr.shape, lambda i: (0,) * arr.ndim, pipeline_mode=pl.Buffered(1))

  def w_cols(off, width):
    assert off % width == 0
    return pl.BlockSpec((D_MODEL, width), lambda i: (0, off // width), pipeline_mode=pl.Buffered(1))

  w_specs = [w_cols(G_ATT_OFF, GROUP_WIDTH)] + [
      w_cols(off, CONV_WIDTH)
      for off in (H_OFF, B_OFF, C_OFF, G_CONV_OFF, GATE_OFF, GATE_OFF + D_MODEL)]
  others = (conv_w, w_conv_out, w_att_out, b_gate, w_o, ln_g, ln_b)
  return pl.pallas_call(
      functools.partial(_tail_kernel, alpha),
      out_shape=jax.ShapeDtypeStruct((s, D_MODEL), F32),
      grid=(s // tm,),
      in_specs=[
          pl.BlockSpec((tm, D_MODEL), lambda i: (i, 0)),
          pl.BlockSpec((tm, GROUP_WIDTH), lambda i: (i, 0)),
      ] + w_specs + [whole(w) for w in others],
      out_specs=pl.BlockSpec((tm, D_MODEL), lambda i: (i, 0)),
      scratch_shapes=[pltpu.VMEM((CONV_HALO + tm, CONV_WIDTH), F32)],
      compiler_params=pltpu.CompilerParams(
          dimension_semantics=("arbitrary",),
          vmem_limit_bytes=VMEM_LIMIT_BYTES,
      ),
      name="conv_merge_norm",
  )(x2d, att, *([w_bf16] * len(w_specs)), *others)


def _layer(x2d, w_in, conv_w, w_conv_out, w_att_out, b_gate, w_o, ln_g, ln_b, alpha):
  assert w_in.shape == (D_MODEL, GATE_OFF + 2 * D_MODEL)
  w_bf16 = w_in.astype(BF16)
  qkv_groups = _qkv_projection(x2d, w_bf16)
  att = _attention(qkv_groups)
  return _tail(x2d, att, w_bf16, conv_w, w_conv_out.astype(BF16), w_att_out.astype(BF16),
               b_gate.reshape(1, -1), w_o.astype(BF16), ln_g.reshape(1, -1), ln_b.reshape(1, -1),
               alpha)


def kernel(x, w_in, conv_w, w_conv_out, w_att_out, b_gate, w_o, ln_g, ln_b):
  batch, seq, d_model = x.shape
  depth = w_in.shape[0]
  assert d_model == D_MODEL and seq % SUPER_BLOCK == 0
  assert HEAD_DIM ** -0.5 == 0.125
  alpha = (2.0 * depth) ** 0.25
  outs = []
  for b in range(batch):
    h = x[b]
    for layer in range(depth):
      h = _layer(h, w_in[layer], conv_w[layer], w_conv_out[layer], w_att_out[layer],
                 b_gate[layer], w_o[layer], ln_g[layer], ln_b[layer], alpha)
    outs.append(h)
  return jnp.stack(outs)
```

```python
import functools

import jax
import jax.numpy as jnp
from jax import lax
from jax.experimental import pallas as pl
from jax.experimental.pallas import tpu as pltpu

D_MODEL = 1024
HEAD_DIM = 64
HEADS_PER_GROUP = 8
ATT_GROUPS = ((128, 1), (512, 4), (2048, 16))
N_GROUPS = len(ATT_GROUPS)
GROUP_WIDTH = HEADS_PER_GROUP * HEAD_DIM
ATT_QKV_WIDTH = N_GROUPS * GROUP_WIDTH
CONV_WIDTH = D_MODEL
CONV_K = 3
LN_EPS = 1e-5
LANES = 128

ATT_BLOCK = 128
assert all(w // d == ATT_BLOCK for w, d in ATT_GROUPS)
SUPER_BLOCK = ATT_BLOCK * max(d for _, d in ATT_GROUPS)
TILES_PER_SUPER = SUPER_BLOCK // ATT_BLOCK
ATT_TILES_PER_STEP = 2

QKV_ROW_TILE = 1024
TAIL_ROW_TILE = 512
CONV_HALO = 8
MERGE_ROWS = 16
HEAD_PAIRS = HEADS_PER_GROUP // 2
assert 2 * HEAD_DIM == LANES

VMEM_LIMIT_BYTES = 56 * 1024 * 1024

F32 = jnp.float32
BF16 = jnp.bfloat16

Q_OFF, K_OFF, V_OFF = 0, ATT_QKV_WIDTH, 2 * ATT_QKV_WIDTH
G_ATT_OFF = 3 * ATT_QKV_WIDTH
H_OFF = G_ATT_OFF + GROUP_WIDTH
B_OFF = H_OFF + CONV_WIDTH
C_OFF = B_OFF + CONV_WIDTH
G_CONV_OFF = C_OFF + CONV_WIDTH
GATE_OFF = G_CONV_OFF + CONV_WIDTH


def _qkv_kernel(x_ref, wq_ref, wk_ref, wv_ref, o1_ref, o2_ref, o3_ref, xp_ref, xl_ref):
  tm = x_ref.shape[0]
  out_refs = (o1_ref, o2_ref, o3_ref)
  dil = [d for _, d in ATT_GROUPS]
  assert dil[0] == 1
  scale = HEAD_DIM ** -0.5
  lane_tiles = [pl.ds(c * LANES, LANES) for c in range(D_MODEL // LANES)]

  def load_rows():
    xp_ref[0] = x_ref[...].astype(BF16)
    for c, lanes in enumerate(lane_tiles):
      xl_ref[0, c] = x_ref[:, lanes]

  def permute_rows(g):
    q = dil[g + 1] // dil[g]
    assert dil[g + 1] == q * dil[g]
    n, n_next = tm // dil[g], tm // dil[g + 1]
    for r in range(dil[g]):
      for a in range(q):
        dst = pl.ds((a * dil[g] + r) * n_next, n_next)
        for c, lanes in enumerate(lane_tiles):
          rows = xl_ref[g, c, pl.ds(r * n + a, n_next, stride=q), :]
          xp_ref[g + 1, dst, lanes] = rows.astype(BF16)
          if g + 2 < N_GROUPS:
            xl_ref[g + 1, c, dst, :] = rows

  def project(g):
    n = tm // dil[g]
    for sec, w_ref in enumerate((wq_ref, wk_ref, wv_ref)):
      res = jnp.dot(xp_ref[g], w_ref[...], preferred_element_type=F32)
      if sec == 0:
        res = res * scale
      res = res.astype(BF16)
      for r in range(dil[g]):
        out_refs[g][r, :, sec * GROUP_WIDTH:(sec + 1) * GROUP_WIDTH] = res[r * n:(r + 1) * n]

  @pl.when(jnp.logical_and(pl.program_id(0) == 0, pl.program_id(1) == 0))
  def _():
    load_rows()

  for g in range(N_GROUPS):
    @pl.when(pl.program_id(1) == g)
    def _(g=g):
      project(g)
      if g + 1 < N_GROUPS:
        permute_rows(g)
      else:
        load_rows()


def _qkv_projection(x2d, w_bf16):
  s = x2d.shape[0]
  tm = QKV_ROW_TILE

  def w_spec(off):
    return pl.BlockSpec((D_MODEL, GROUP_WIDTH), lambda i, g: (0, off // GROUP_WIDTH + g))

  def x_map(i, g):
    return jnp.minimum(i + g // (N_GROUPS - 1), s // tm - 1), 0

  return pl.pallas_call(
      _qkv_kernel,
      out_shape=[jax.ShapeDtypeStruct((d, s // d, ATT_QKV_WIDTH), BF16) for _, d in ATT_GROUPS],
      grid=(s // tm, N_GROUPS),
      in_specs=[pl.BlockSpec((tm, D_MODEL), x_map),
                w_spec(Q_OFF), w_spec(K_OFF), w_spec(V_OFF)],
      out_specs=[pl.BlockSpec((d, tm // d, ATT_QKV_WIDTH), lambda i, g: (0, i, 0))
                 for _, d in ATT_GROUPS],
      scratch_shapes=[
          pltpu.VMEM((N_GROUPS, tm, D_MODEL), BF16),
          pltpu.VMEM((N_GROUPS - 1, D_MODEL // LANES, tm, LANES), F32),
      ],
      compiler_params=pltpu.CompilerParams(
          dimension_semantics=("arbitrary", "arbitrary"),
          vmem_limit_bytes=VMEM_LIMIT_BYTES,
      ),
      name="qkv_projection",
  )(x2d, w_bf16, w_bf16, w_bf16)


def _attention_tile(cur, prev, bias):
  nq = ATT_BLOCK
  lane = lax.broadcasted_iota(jnp.int32, (nq, LANES), 1)
  low_half = lane < HEAD_DIM
  ones = jnp.ones((2 * nq, LANES), BF16)

  def pair_tile(x):
    return jnp.where(low_half, x[:nq], x[nq:])

  outs, maxes, sums = [], [], []
  for p in range(HEAD_PAIRS):
    q2 = cur(0, p)
    k2 = jnp.concatenate([prev(1, p), cur(1, p)], axis=0)
    v2 = jnp.concatenate([prev(2, p), cur(2, p)], axis=0)
    zero = jnp.zeros_like(q2)
    qs = jnp.concatenate([jnp.where(low_half, q2, zero), jnp.where(low_half, zero, q2)], axis=0)
    sc = lax.dot_general(qs, k2, (((1,), (1,)), ((), ())), preferred_element_type=F32)
    sc = sc + bias
    m = jnp.max(sc, axis=1, keepdims=True)
    e = jnp.exp(sc - m).astype(BF16)
    ol = jnp.dot(e, jnp.concatenate([v2, ones], axis=1), preferred_element_type=F32)
    outs.append(pair_tile(ol[:, :LANES]))
    sums.append(pair_tile(ol[:, LANES:]))
    maxes.append(pair_tile(jnp.broadcast_to(m, (2 * nq, LANES))))
  return outs, maxes, sums


def _step_tiles(d, sb, j):
  nt = ATT_TILES_PER_STEP
  bpr = TILES_PER_SUPER // d
  if bpr >= nt:
    assert bpr % nt == 0
    spr = bpr // nt
    b0 = (j % spr) * nt
    return sb * bpr + b0, (lambda t: j // spr), (lambda t: b0 + t)
  assert bpr == 1
  return sb, (lambda t: j * nt + t), (lambda t: 0)


def _attention_kernel(*refs):
  in_refs = refs[:2 * N_GROUPS]
  o_ref = refs[2 * N_GROUPS]
  acc_ref, max_ref, sum_ref, bias_ref = refs[2 * N_GROUPS + 1:]
  sb = pl.program_id(0)
  j = pl.program_id(1)
  nq = ATT_BLOCK
  nt = ATT_TILES_PER_STEP

  @pl.when(jnp.logical_and(sb == 0, j == 0))
  def _():
    row = lax.broadcasted_iota(jnp.int32, (2 * nq, 2 * nq), 0) % nq
    col = lax.broadcasted_iota(jnp.int32, (2 * nq, 2 * nq), 1)
    for hp in range(2):
      first_col = row if hp else jnp.maximum(row, nq)
      valid = jnp.logical_and(col >= first_col, col - nq <= row)
      bias_ref[hp] = jnp.where(valid, 0.0, -jnp.inf).astype(F32)

  def cols(sec, p):
    return pl.ds(sec * GROUP_WIDTH + p * LANES, LANES)

  for g, (_, d) in enumerate(ATT_GROUPS):
    cur_ref, prev_ref = in_refs[2 * g:2 * g + 2]
    first_blk, res, loc = _step_tiles(d, sb, j)
    for t in range(nt):
      if cur_ref.shape[0] == 1:
        cur = lambda sec, p, t=t: cur_ref[0, t * nq:(t + 1) * nq, cols(sec, p)]
        if t == 0:
          prev = lambda sec, p: prev_ref[0, :, cols(sec, p)]
          has_prev = (first_blk > 0).astype(jnp.int32)
        else:
          prev = lambda sec, p, t=t: cur_ref[0, (t - 1) * nq:t * nq, cols(sec, p)]
          has_prev = 1
      else:
        cur = lambda sec, p, t=t: cur_ref[t, :, cols(sec, p)]
        prev = lambda sec, p, t=t: prev_ref[t, :, cols(sec, p)]
        has_prev = (first_blk > 0).astype(jnp.int32)
      o, m, l = _attention_tile(cur, prev, bias_ref[has_prev])
      r, b = res(t), loc(t)
      if d == 1:
        rows = pl.ds(pl.multiple_of(b * nq, nq), nq)
      else:
        rows = pl.ds(b * nq * d + r, nq, stride=d)
      for p in range(HEAD_PAIRS):
        acc_ref[g, p, rows, :] = o[p]
        max_ref[g, p, rows, :] = m[p]
        sum_ref[g, p, rows, :] = l[p]

  @pl.when(j == TILES_PER_SUPER // nt - 1)
  def _():
    def body(c, carry):
      rows = pl.ds(pl.multiple_of(c * MERGE_ROWS, MERGE_ROWS), MERGE_ROWS)
      for p in range(HEAD_PAIRS):
        ms = [max_ref[g, p, rows, :] for g in range(N_GROUPS)]
        top = functools.reduce(jnp.maximum, ms)
        ws = [jnp.exp(m - top) for m in ms]
        num = functools.reduce(jnp.add, [w * acc_ref[g, p, rows, :] for g, w in enumerate(ws)])
        den = functools.reduce(jnp.add, [w * sum_ref[g, p, rows, :] for g, w in enumerate(ws)])
        o_ref[rows, p * LANES:(p + 1) * LANES] = num / den
      return carry
    lax.fori_loop(0, SUPER_BLOCK // MERGE_ROWS, body, 0, unroll=2)


def _attention(qkv_groups):
  s = qkv_groups[0].shape[1]
  nt = ATT_TILES_PER_STEP
  in_specs, args = [], []
  for (_, d), arr in zip(ATT_GROUPS, qkv_groups):
    if TILES_PER_SUPER // d >= nt:
      def cur_map(sb, j, d=d):
        first_blk, res, _ = _step_tiles(d, sb, j)
        return res(0), first_blk // nt, 0

      def prev_map(sb, j, d=d):
        first_blk, res, _ = _step_tiles(d, sb, j)
        return res(0), jnp.maximum(first_blk - 1, 0), 0

      in_specs += [pl.BlockSpec((1, nt * ATT_BLOCK, ATT_QKV_WIDTH), cur_map),
                   pl.BlockSpec((1, ATT_BLOCK, ATT_QKV_WIDTH), prev_map)]
    else:
      def cur_map(sb, j, d=d):
        first_blk, res, _ = _step_tiles(d, sb, j)
        return res(0) // nt, first_blk, 0

      def prev_map(sb, j, d=d):
        first_blk, res, _ = _step_tiles(d, sb, j)
        return res(0) // nt, jnp.maximum(first_blk - 1, 0), 0

      in_specs += [pl.BlockSpec((nt, ATT_BLOCK, ATT_QKV_WIDTH), cur_map),
                   pl.BlockSpec((nt, ATT_BLOCK, ATT_QKV_WIDTH), prev_map)]
    args += [arr] * 2
  scratch = pltpu.VMEM((N_GROUPS, HEAD_PAIRS, SUPER_BLOCK, LANES), F32)
  return pl.pallas_call(
      _attention_kernel,
      out_shape=jax.ShapeDtypeStruct((s, GROUP_WIDTH), F32),
      grid=(s // SUPER_BLOCK, TILES_PER_SUPER // nt),
      in_specs=in_specs,
      out_specs=pl.BlockSpec((SUPER_BLOCK, GROUP_WIDTH), lambda sb, j: (sb, 0)),
      scratch_shapes=[scratch] * 3 + [pltpu.VMEM((2, 2 * ATT_BLOCK, 2 * ATT_BLOCK), F32)],
      compiler_params=pltpu.CompilerParams(
          dimension_semantics=("arbitrary", "arbitrary"),
          vmem_limit_bytes=VMEM_LIMIT_BYTES,
      ),
      name="dilated_attention",
  )(*args)


def _silu(x):
  return x * jax.nn.sigmoid(x)


def _tail_kernel(alpha, x_ref, att_ref, wga_ref, wh_ref, wb_ref, wc_ref, wgc_ref, wg1_ref, wg2_ref,
                 cw_ref, wco_ref, wao_ref, bg_ref, wo_ref, lg_ref, lb_ref, o_ref, u_ref):
  tm = x_ref.shape[0]
  x = x_ref[...]
  xb = x.astype(BF16)

  def proj(w_ref):
    return jnp.dot(xb, w_ref[...], preferred_element_type=F32)

  @pl.when(pl.program_id(0) == 0)
  def _():
    u_ref[0:CONV_HALO, :] = jnp.zeros((CONV_HALO, CONV_WIDTH), F32)

  u_ref[CONV_HALO:, :] = proj(wc_ref) * proj(wh_ref)
  conv = cw_ref[CONV_K - 1:CONV_K, :] * u_ref[CONV_HALO:, :]
  for k in range(CONV_K - 1):
    off = CONV_HALO - (CONV_K - 1 - k)
    conv = conv + cw_ref[k:k + 1, :] * u_ref[off:off + tm, :]
  u_ref[0:CONV_HALO, :] = u_ref[tm:tm + CONV_HALO, :]
  a_conv = (proj(wb_ref) * conv) * _silu(proj(wgc_ref))
  y_conv = jnp.dot(a_conv.astype(BF16), wco_ref[...], preferred_element_type=F32)

  a_att = att_ref[...] * _silu(proj(wga_ref))
  y_att = jnp.dot(a_att.astype(BF16), wao_ref[...], preferred_element_type=F32)

  g_c = jax.nn.sigmoid(proj(wg1_ref) + bg_ref[:, :D_MODEL])
  g_a = jax.nn.sigmoid(proj(wg2_ref) + bg_ref[:, D_MODEL:])
  merged = g_c * y_conv + g_a * y_att
  out = jnp.dot(merged.astype(BF16), wo_ref[...], preferred_element_type=F32)
  y = alpha * x + out
  mu = jnp.mean(y, axis=-1, keepdims=True)
  yc = y - mu
  var = jnp.mean(yc * yc, axis=-1, keepdims=True)
  o_ref[...] = yc * lax.rsqrt(var + LN_EPS) * lg_ref[...] + lb_ref[...]


def _tail(x2d, att, w_bf16, conv_w, w_conv_out, w_att_out, b_gate, w_o, ln_g, ln_b, alpha):
  s = x2d.shape[0]
  tm = TAIL_ROW_TILE

  def whole(arr):
    return pl.BlockSpec(arr.shape, lambda i: (0,) * arr.ndim, pipeline_mode=pl.Buffered(1))

  def w_cols(off, width):
    assert off % width == 0
    return pl.BlockSpec((D_MODEL, width), lambda i: (0, off // width), pipeline_mode=pl.Buffered(1))

  w_specs = [w_cols(G_ATT_OFF, GROUP_WIDTH)] + [
      w_cols(off, CONV_WIDTH)
      for off in (H_OFF, B_OFF, C_OFF, G_CONV_OFF, GATE_OFF, GATE_OFF + D_MODEL)]
  others = (conv_w, w_conv_out, w_att_out, b_gate, w_o, ln_g, ln_b)
  return pl.pallas_call(
      functools.partial(_tail_kernel, alpha),
      out_shape=jax.ShapeDtypeStruct((s, D_MODEL), F32),
      grid=(s // tm,),
      in_specs=[
          pl.BlockSpec((tm, D_MODEL), lambda i: (i, 0)),
          pl.BlockSpec((tm, GROUP_WIDTH), lambda i: (i, 0)),
      ] + w_specs + [whole(w) for w in others],
      out_specs=pl.BlockSpec((tm, D_MODEL), lambda i: (i, 0)),
      scratch_shapes=[pltpu.VMEM((CONV_HALO + tm, CONV_WIDTH), F32)],
      compiler_params=pltpu.CompilerParams(
          dimension_semantics=("arbitrary",),
          vmem_limit_bytes=VMEM_LIMIT_BYTES,
      ),
      name="conv_merge_norm",
  )(x2d, att, *([w_bf16] * len(w_specs)), *others)


def _layer(x2d, w_in, conv_w, w_conv_out, w_att_out, b_gate, w_o, ln_g, ln_b, alpha):
  assert w_in.shape == (D_MODEL, GATE_OFF + 2 * D_MODEL)
  w_bf16 = w_in.astype(BF16)
  qkv_groups = _qkv_projection(x2d, w_bf16)
  att = _attention(qkv_groups)
  return _tail(x2d, att, w_bf16, conv_w, w_conv_out.astype(BF16), w_att_out.astype(BF16),
               b_gate.reshape(1, -1), w_o.astype(BF16), ln_g.reshape(1, -1), ln_b.reshape(1, -1),
               alpha)


def kernel(x, w_in, conv_w, w_conv_out, w_att_out, b_gate, w_o, ln_g, ln_b):
  batch, seq, d_model = x.shape
  depth = w_in.shape[0]
  assert d_model == D_MODEL and seq % SUPER_BLOCK == 0
  assert HEAD_DIM ** -0.5 == 0.125
  alpha = (2.0 * depth) ** 0.25
  outs = []
  for b in range(batch):
    h = x[b]
    for layer in range(depth):
      h = _layer(h, w_in[layer], conv_w[layer], w_conv_out[layer], w_att_out[layer],
                 b_gate[layer], w_o[layer], ln_g[layer], ln_b[layer], alpha)
    outs.append(h)
  return jnp.stack(outs)
```

```python
import functools

import jax
import jax.numpy as jnp
from jax import lax
from jax.experimental import pallas as pl
from jax.experimental.pallas import tpu as pltpu

D_MODEL = 1024
HEAD_DIM = 64
HEADS_PER_GROUP = 8
ATT_GROUPS = ((128, 1), (512, 4), (2048, 16))
N_GROUPS = len(ATT_GROUPS)
GROUP_WIDTH = HEADS_PER_GROUP * HEAD_DIM
ATT_QKV_WIDTH = N_GROUPS * GROUP_WIDTH
CONV_WIDTH = D_MODEL
CONV_K = 3
LN_EPS = 1e-5
LANES = 128

ATT_BLOCK = 128
assert all(w // d == ATT_BLOCK for w, d in ATT_GROUPS)
SUPER_BLOCK = ATT_BLOCK * max(d for _, d in ATT_GROUPS)
TILES_PER_SUPER = SUPER_BLOCK // ATT_BLOCK
ATT_TILES_PER_STEP = 4

QKV_ROW_TILE = 1024
TAIL_ROW_TILE = 512
TAIL_NORM_CHUNKS = 2
CONV_HALO = 8
MERGE_ROWS = 16
HEAD_PAIRS = HEADS_PER_GROUP // 2
assert 2 * HEAD_DIM == LANES

VMEM_LIMIT_BYTES = 56 * 1024 * 1024

F32 = jnp.float32
BF16 = jnp.bfloat16

Q_OFF, K_OFF, V_OFF = 0, ATT_QKV_WIDTH, 2 * ATT_QKV_WIDTH
G_ATT_OFF = 3 * ATT_QKV_WIDTH
H_OFF = G_ATT_OFF + GROUP_WIDTH
B_OFF = H_OFF + CONV_WIDTH
C_OFF = B_OFF + CONV_WIDTH
G_CONV_OFF = C_OFF + CONV_WIDTH
GATE_OFF = G_CONV_OFF + CONV_WIDTH


def _qkv_kernel(x_ref, w_ref, o1_ref, o2_ref, o3_ref, xp_ref, xl_ref):
  tm = x_ref.shape[0]
  out_refs = (o1_ref, o2_ref, o3_ref)
  dil = [d for _, d in ATT_GROUPS]
  assert dil[0] == 1
  scale = HEAD_DIM ** -0.5
  lane_tiles = [pl.ds(c * LANES, LANES) for c in range(D_MODEL // LANES)]

  def load_rows():
    xp_ref[0] = x_ref[...].astype(BF16)
    for c, lanes in enumerate(lane_tiles):
      xl_ref[0, c] = x_ref[:, lanes]

  def permute_rows(g):
    q = dil[g + 1] // dil[g]
    assert dil[g + 1] == q * dil[g]
    n, n_next = tm // dil[g], tm // dil[g + 1]
    for r in range(dil[g]):
      for a in range(q):
        dst = pl.ds((a * dil[g] + r) * n_next, n_next)
        for c, lanes in enumerate(lane_tiles):
          rows = xl_ref[g, c, pl.ds(r * n + a, n_next, stride=q), :]
          xp_ref[g + 1, dst, lanes] = rows.astype(BF16)
          if g + 2 < N_GROUPS:
            xl_ref[g + 1, c, dst, :] = rows

  def project(g):
    n = tm // dil[g]
    for sec, off in enumerate((Q_OFF, K_OFF, V_OFF)):
      w_cols = pl.ds(off + g * GROUP_WIDTH, GROUP_WIDTH)
      res = jnp.dot(xp_ref[g], w_ref[:, w_cols], preferred_element_type=F32)
      if sec == 0:
        res = res * scale
      res = res.astype(BF16)
      for r in range(dil[g]):
        out_refs[g][r, :, sec * GROUP_WIDTH:(sec + 1) * GROUP_WIDTH] = res[r * n:(r + 1) * n]

  @pl.when(jnp.logical_and(pl.program_id(0) == 0, pl.program_id(1) == 0))
  def _():
    load_rows()

  for g in range(N_GROUPS):
    @pl.when(pl.program_id(1) == g)
    def _(g=g):
      project(g)
      if g + 1 < N_GROUPS:
        permute_rows(g)
      else:
        load_rows()


def _qkv_projection(x2d, w_bf16):
  s = x2d.shape[0]
  tm = QKV_ROW_TILE

  assert (Q_OFF, K_OFF, V_OFF) == (0, ATT_QKV_WIDTH, 2 * ATT_QKV_WIDTH)
  w_spec = pl.BlockSpec((D_MODEL, 3 * ATT_QKV_WIDTH), lambda i, g: (0, 0),
                        pipeline_mode=pl.Buffered(1))

  def x_map(i, g):
    return jnp.minimum(i + g // (N_GROUPS - 1), s // tm - 1), 0

  return pl.pallas_call(
      _qkv_kernel,
      out_shape=[jax.ShapeDtypeStruct((d, s // d, ATT_QKV_WIDTH), BF16) for _, d in ATT_GROUPS],
      grid=(s // tm, N_GROUPS),
      in_specs=[pl.BlockSpec((tm, D_MODEL), x_map), w_spec],
      out_specs=[pl.BlockSpec((d, tm // d, ATT_QKV_WIDTH), lambda i, g: (0, i, 0))
                 for _, d in ATT_GROUPS],
      scratch_shapes=[
          pltpu.VMEM((N_GROUPS, tm, D_MODEL), BF16),
          pltpu.VMEM((N_GROUPS - 1, D_MODEL // LANES, tm, LANES), F32),
      ],
      compiler_params=pltpu.CompilerParams(
          dimension_semantics=("arbitrary", "arbitrary"),
          vmem_limit_bytes=VMEM_LIMIT_BYTES,
      ),
      name="qkv_projection",
  )(x2d, w_bf16)


def _attention_tile(cur, prev, bias):
  nq = ATT_BLOCK
  lane = lax.broadcasted_iota(jnp.int32, (nq, LANES), 1)
  low_half = lane < HEAD_DIM
  ones = jnp.ones((2 * nq, LANES), BF16)

  def pair_tile(x):
    return jnp.where(low_half, x[:nq], x[nq:])

  outs, maxes, sums = [], [], []
  for p in range(HEAD_PAIRS):
    q2 = cur(0, p)
    k2 = jnp.concatenate([prev(1, p), cur(1, p)], axis=0)
    v2 = jnp.concatenate([prev(2, p), cur(2, p)], axis=0)
    zero = jnp.zeros_like(q2)
    qs = jnp.concatenate([jnp.where(low_half, q2, zero), jnp.where(low_half, zero, q2)], axis=0)
    sc = lax.dot_general(qs, k2, (((1,), (1,)), ((), ())), preferred_element_type=F32)
    sc = sc + bias
    m = jnp.max(sc, axis=1, keepdims=True)
    e = jnp.exp(sc - m).astype(BF16)
    ol = jnp.dot(e, jnp.concatenate([v2, ones], axis=1), preferred_element_type=F32)
    outs.append(pair_tile(ol[:, :LANES]))
    sums.append(pair_tile(ol[:, LANES:]))
    maxes.append(pair_tile(jnp.broadcast_to(m, (2 * nq, LANES))))
  return outs, maxes, sums


def _step_tiles(d, sb, j):
  nt = ATT_TILES_PER_STEP
  bpr = TILES_PER_SUPER // d
  if bpr >= nt:
    assert bpr % nt == 0
    spr = bpr // nt
    b0 = (j % spr) * nt
    return sb * bpr + b0, (lambda t: j // spr), (lambda t: b0 + t)
  assert bpr == 1
  return sb, (lambda t: j * nt + t), (lambda t: 0)


def _attention_kernel(*refs):
  in_refs = refs[:2 * N_GROUPS]
  o_ref = refs[2 * N_GROUPS]
  acc_ref, max_ref, sum_ref, bias_ref = refs[2 * N_GROUPS + 1:]
  sb = pl.program_id(0)
  j = pl.program_id(1)
  nq = ATT_BLOCK
  nt = ATT_TILES_PER_STEP

  @pl.when(jnp.logical_and(sb == 0, j == 0))
  def _():
    row = lax.broadcasted_iota(jnp.int32, (2 * nq, 2 * nq), 0) % nq
    col = lax.broadcasted_iota(jnp.int32, (2 * nq, 2 * nq), 1)
    for hp in range(2):
      first_col = row if hp else jnp.maximum(row, nq)
      valid = jnp.logical_and(col >= first_col, col - nq <= row)
      bias_ref[hp] = jnp.where(valid, 0.0, -jnp.inf).astype(F32)

  def cols(sec, p):
    return pl.ds(sec * GROUP_WIDTH + p * LANES, LANES)

  for g, (_, d) in enumerate(ATT_GROUPS):
    cur_ref, prev_ref = in_refs[2 * g:2 * g + 2]
    first_blk, res, loc = _step_tiles(d, sb, j)
    for t in range(nt):
      if cur_ref.shape[0] == 1:
        cur = lambda sec, p, t=t: cur_ref[0, t * nq:(t + 1) * nq, cols(sec, p)]
        if t == 0:
          prev = lambda sec, p: prev_ref[0, :, cols(sec, p)]
          has_prev = (first_blk > 0).astype(jnp.int32)
        else:
          prev = lambda sec, p, t=t: cur_ref[0, (t - 1) * nq:t * nq, cols(sec, p)]
          has_prev = 1
      else:
        cur = lambda sec, p, t=t: cur_ref[t, :, cols(sec, p)]
        prev = lambda sec, p, t=t: prev_ref[t, :, cols(sec, p)]
        has_prev = (first_blk > 0).astype(jnp.int32)
      o, m, l = _attention_tile(cur, prev, bias_ref[has_prev])
      r, b = res(t), loc(t)
      if d == 1:
        rows = pl.ds(pl.multiple_of(b * nq, nq), nq)
      else:
        rows = pl.ds(b * nq * d + r, nq, stride=d)
      for p in range(HEAD_PAIRS):
        acc_ref[g, p, rows, :] = o[p]
        max_ref[g, p, rows, :] = m[p]
        sum_ref[g, p, rows, :] = l[p]

  @pl.when(j == TILES_PER_SUPER // nt - 1)
  def _():
    def body(c, carry):
      rows = pl.ds(pl.multiple_of(c * MERGE_ROWS, MERGE_ROWS), MERGE_ROWS)
      for p in range(HEAD_PAIRS):
        ms = [max_ref[g, p, rows, :] for g in range(N_GROUPS)]
        top = functools.reduce(jnp.maximum, ms)
        ws = [jnp.exp(m - top) for m in ms]
        num = functools.reduce(jnp.add, [w * acc_ref[g, p, rows, :] for g, w in enumerate(ws)])
        den = functools.reduce(jnp.add, [w * sum_ref[g, p, rows, :] for g, w in enumerate(ws)])
        o_ref[rows, p * LANES:(p + 1) * LANES] = num / den
      return carry
    lax.fori_loop(0, SUPER_BLOCK // MERGE_ROWS, body, 0, unroll=2)


def _attention(qkv_groups):
  s = qkv_groups[0].shape[1]
  nt = ATT_TILES_PER_STEP
  in_specs, args = [], []
  for (_, d), arr in zip(ATT_GROUPS, qkv_groups):
    if TILES_PER_SUPER // d >= nt:
      def cur_map(sb, j, d=d):
        first_blk, res, _ = _step_tiles(d, sb, j)
        return res(0), first_blk // nt, 0

      def prev_map(sb, j, d=d):
        first_blk, res, _ = _step_tiles(d, sb, j)
        return res(0), jnp.maximum(first_blk - 1, 0), 0

      in_specs += [pl.BlockSpec((1, nt * ATT_BLOCK, ATT_QKV_WIDTH), cur_map),
                   pl.BlockSpec((1, ATT_BLOCK, ATT_QKV_WIDTH), prev_map)]
    else:
      def cur_map(sb, j, d=d):
        first_blk, res, _ = _step_tiles(d, sb, j)
        return res(0) // nt, first_blk, 0

      def prev_map(sb, j, d=d):
        first_blk, res, _ = _step_tiles(d, sb, j)
        return res(0) // nt, jnp.maximum(first_blk - 1, 0), 0

      in_specs += [pl.BlockSpec((nt, ATT_BLOCK, ATT_QKV_WIDTH), cur_map),
                   pl.BlockSpec((nt, ATT_BLOCK, ATT_QKV_WIDTH), prev_map)]
    args += [arr] * 2
  scratch = pltpu.VMEM((N_GROUPS, HEAD_PAIRS, SUPER_BLOCK, LANES), F32)
  return pl.pallas_call(
      _attention_kernel,
      out_shape=jax.ShapeDtypeStruct((s, GROUP_WIDTH), F32),
      grid=(s // SUPER_BLOCK, TILES_PER_SUPER // nt),
      in_specs=in_specs,
      out_specs=pl.BlockSpec((SUPER_BLOCK, GROUP_WIDTH), lambda sb, j: (sb, 0),
                             pipeline_mode=pl.Buffered(1)),
      scratch_shapes=[scratch] * 3 + [pltpu.VMEM((2, 2 * ATT_BLOCK, 2 * ATT_BLOCK), F32)],
      compiler_params=pltpu.CompilerParams(
          dimension_semantics=("arbitrary", "arbitrary"),
          vmem_limit_bytes=VMEM_LIMIT_BYTES,
      ),
      name="dilated_attention",
  )(*args)


def _silu(x):
  return x * jax.nn.sigmoid(x)


def _tail_kernel(alpha, x_ref, att_ref, wga_ref, wh_ref, wb_ref, wc_ref, wgc_ref, wg1_ref, wg2_ref,
                 cw_ref, wco_ref, wao_ref, bg_ref, wo_ref, lg_ref, lb_ref, o_ref, u_ref):
  tm = x_ref.shape[0]
  x = x_ref[...]
  xb = x.astype(BF16)

  def proj(w_ref):
    return jnp.dot(xb, w_ref[...], preferred_element_type=F32)

  @pl.when(pl.program_id(0) == 0)
  def _():
    u_ref[0:CONV_HALO, :] = jnp.zeros((CONV_HALO, CONV_WIDTH), F32)

  u_ref[CONV_HALO:, :] = proj(wc_ref) * proj(wh_ref)
  conv = cw_ref[CONV_K - 1:CONV_K, :] * u_ref[CONV_HALO:, :]
  for k in range(CONV_K - 1):
    off = CONV_HALO - (CONV_K - 1 - k)
    conv = conv + cw_ref[k:k + 1, :] * u_ref[off:off + tm, :]
  u_ref[0:CONV_HALO, :] = u_ref[tm:tm + CONV_HALO, :]
  a_conv = (proj(wb_ref) * conv) * _silu(proj(wgc_ref))
  y_conv = jnp.dot(a_conv.astype(BF16), wco_ref[...], preferred_element_type=F32)

  a_att = att_ref[...] * _silu(proj(wga_ref))
  y_att = jnp.dot(a_att.astype(BF16), wao_ref[...], preferred_element_type=F32)

  g_c = jax.nn.sigmoid(proj(wg1_ref) + bg_ref[:, :D_MODEL])
  g_a = jax.nn.sigmoid(proj(wg2_ref) + bg_ref[:, D_MODEL:])
  merged = (g_c * y_conv + g_a * y_att).astype(BF16)
  chunk = tm // TAIL_NORM_CHUNKS
  for rows in (slice(k * chunk, (k + 1) * chunk) for k in range(TAIL_NORM_CHUNKS)):
    out = jnp.dot(merged[rows], wo_ref[...], preferred_element_type=F32)
    y = alpha * x_ref[rows, :] + out
    mu = jnp.mean(y, axis=-1, keepdims=True)
    yc = y - mu
    var = jnp.mean(yc * yc, axis=-1, keepdims=True)
    o_ref[rows, :] = yc * lax.rsqrt(var + LN_EPS) * lg_ref[...] + lb_ref[...]


def _tail(x2d, att, w_bf16, conv_w, w_conv_out, w_att_out, b_gate, w_o, ln_g, ln_b, alpha):
  s = x2d.shape[0]
  tm = TAIL_ROW_TILE

  def whole(arr):
    return pl.BlockSpec(arr.shape, lambda i: (0,) * arr.ndim, pipeline_mode=pl.Buffered(1))

  def w_cols(off, width):
    assert off % width == 0
    return pl.BlockSpec((D_MODEL, width), lambda i: (0, off // width), pipeline_mode=pl.Buffered(1))

  w_specs = [w_cols(G_ATT_OFF, GROUP_WIDTH)] + [
      w_cols(off, CONV_WIDTH)
      for off in (H_OFF, B_OFF, C_OFF, G_CONV_OFF, GATE_OFF, GATE_OFF + D_MODEL)]
  others = (conv_w, w_conv_out, w_att_out, b_gate, w_o, ln_g, ln_b)
  return pl.pallas_call(
      functools.partial(_tail_kernel, alpha),
      out_shape=jax.ShapeDtypeStruct((s, D_MODEL), F32),
      grid=(s // tm,),
      in_specs=[
          pl.BlockSpec((tm, D_MODEL), lambda i: (i, 0)),
          pl.BlockSpec((tm, GROUP_WIDTH), lambda i: (i, 0)),
      ] + w_specs + [whole(w) for w in others],
      out_specs=pl.BlockSpec((tm, D_MODEL), lambda i: (i, 0)),
      scratch_shapes=[pltpu.VMEM((CONV_HALO + tm, CONV_WIDTH), F32)],
      compiler_params=pltpu.CompilerParams(
          dimension_semantics=("arbitrary",),
          vmem_limit_bytes=VMEM_LIMIT_BYTES,
      ),
      name="conv_merge_norm",
  )(x2d, att, *([w_bf16] * len(w_specs)), *others)


def _layer(x2d, w_in, conv_w, w_conv_out, w_att_out, b_gate, w_o, ln_g, ln_b, alpha):
  assert w_in.shape == (D_MODEL, GATE_OFF + 2 * D_MODEL)
  w_bf16 = w_in.astype(BF16)
  qkv_groups = _qkv_projection(x2d, w_bf16)
  att = _attention(qkv_groups)
  return _tail(x2d, att, w_bf16, conv_w, w_conv_out.astype(BF16), w_att_out.astype(BF16),
               b_gate.reshape(1, -1), w_o.astype(BF16), ln_g.reshape(1, -1), ln_b.reshape(1, -1),
               alpha)


def kernel(x, w_in, conv_w, w_conv_out, w_att_out, b_gate, w_o, ln_g, ln_b):
  batch, seq, d_model = x.shape
  depth = w_in.shape[0]
  assert d_model == D_MODEL and seq % SUPER_BLOCK == 0
  assert HEAD_DIM ** -0.5 == 0.125
  alpha = (2.0 * depth) ** 0.25
  outs = []
  for b in range(batch):
    h = x[b]
    for layer in range(depth):
      h = _layer(h, w_in[layer], conv_w[layer], w_conv_out[layer], w_att_out[layer],
                 b_gate[layer], w_o[layer], ln_g[layer], ln_b[layer], alpha)
    outs.append(h)
  return jnp.stack(outs)
```

```python
import functools

import jax
import jax.numpy as jnp
from jax import lax
from jax.experimental import pallas as pl
from jax.experimental.pallas import tpu as pltpu

D_MODEL = 1024
HEAD_DIM = 64
HEADS_PER_GROUP = 8
ATT_GROUPS = ((128, 1), (512, 4), (2048, 16))
N_GROUPS = len(ATT_GROUPS)
GROUP_WIDTH = HEADS_PER_GROUP * HEAD_DIM
ATT_QKV_WIDTH = N_GROUPS * GROUP_WIDTH
CONV_WIDTH = D_MODEL
CONV_K = 3
LN_EPS = 1e-5
LANES = 128

ATT_BLOCK = 128
assert all(w // d == ATT_BLOCK for w, d in ATT_GROUPS)
SUPER_BLOCK = ATT_BLOCK * max(d for _, d in ATT_GROUPS)
TILES_PER_SUPER = SUPER_BLOCK // ATT_BLOCK
ATT_TILES_PER_STEP = 4

QKV_ROW_TILE = 1024
QKV_DOT_ROW_CHUNKS = 2
TAIL_ROW_TILE = 1024
TAIL_ROW_CHUNKS = 4
TAIL_STAGES = 6
CONV_HALO = 8
MERGE_ROWS = 16
HEAD_PAIRS = HEADS_PER_GROUP // 2
assert 2 * HEAD_DIM == LANES

VMEM_LIMIT_BYTES = 56 * 1024 * 1024

F32 = jnp.float32
BF16 = jnp.bfloat16

Q_OFF, K_OFF, V_OFF = 0, ATT_QKV_WIDTH, 2 * ATT_QKV_WIDTH
G_ATT_OFF = 3 * ATT_QKV_WIDTH
H_OFF = G_ATT_OFF + GROUP_WIDTH
B_OFF = H_OFF + CONV_WIDTH
C_OFF = B_OFF + CONV_WIDTH
G_CONV_OFF = C_OFF + CONV_WIDTH
GATE_OFF = G_CONV_OFF + CONV_WIDTH


def _qkv_kernel(x_ref, w_ref, *refs):
  tm = x_ref.shape[0]
  out_refs = refs[:N_GROUPS]
  xp_ref = refs[N_GROUPS:2 * N_GROUPS]
  xl_ref = refs[2 * N_GROUPS:]
  dil = [d for _, d in ATT_GROUPS]
  assert dil[0] == 1
  scale = HEAD_DIM ** -0.5
  lane_tiles = [pl.ds(c * LANES, LANES) for c in range(D_MODEL // LANES)]

  def load_rows():
    def piece(c, lanes):
      xp_ref[0][:, lanes] = x_ref[:, lanes].astype(BF16)
      xl_ref[0][c] = x_ref[:, lanes]
    return [functools.partial(piece, c, lanes) for c, lanes in enumerate(lane_tiles)]

  def permute_rows(g):
    q = dil[g + 1] // dil[g]
    assert dil[g + 1] == q * dil[g]
    n, n_next = tm // dil[g], tm // dil[g + 1]

    def piece(r, a):
      dst = pl.ds((a * dil[g] + r) * n_next, n_next)
      for c, lanes in enumerate(lane_tiles):
        rows = xl_ref[g][c, pl.ds(r * n + a, n_next, stride=q), :]
        xp_ref[g + 1][dst, lanes] = rows.astype(BF16)
        if g + 2 < N_GROUPS:
          xl_ref[g + 1][c, dst, :] = rows
    return [functools.partial(piece, r, a) for r in range(dil[g]) for a in range(q)]

  def project(g, pieces):
    n = tm // dil[g]
    row_chunk = tm // QKV_DOT_ROW_CHUNKS
    n_dots = 3 * QKV_DOT_ROW_CHUNKS
    k = 0
    for sec, off in enumerate((Q_OFF, K_OFF, V_OFF)):
      w_cols = pl.ds(off + g * GROUP_WIDTH, GROUP_WIDTH)
      for rc in range(QKV_DOT_ROW_CHUNKS):
        lo = rc * row_chunk
        res = jnp.dot(xp_ref[g][lo:lo + row_chunk, :], w_ref[:, w_cols],
                      preferred_element_type=F32)
        if sec == 0:
          res = res * scale
        res = res.astype(BF16)
        assert row_chunk % n == 0 or n % row_chunk == 0
        step = min(n, row_chunk)
        for s0 in range(0, row_chunk, step):
          r, l0 = divmod(lo + s0, n)
          out_refs[g][r, l0:l0 + step, sec * GROUP_WIDTH:(sec + 1) * GROUP_WIDTH] = (
              res[s0:s0 + step])
        k += 1
        for piece in pieces[(k - 1) * len(pieces) // n_dots:k * len(pieces) // n_dots]:
          piece()

  @pl.when(jnp.logical_and(pl.program_id(0) == 0, pl.program_id(1) == 0))
  def _():
    for piece in load_rows():
      piece()

  for g in range(N_GROUPS):
    @pl.when(pl.program_id(1) == g)
    def _(g=g):
      project(g, permute_rows(g) if g + 1 < N_GROUPS else load_rows())


def _qkv_projection(x2d, w_bf16):
  s = x2d.shape[0]
  tm = QKV_ROW_TILE

  assert (Q_OFF, K_OFF, V_OFF) == (0, ATT_QKV_WIDTH, 2 * ATT_QKV_WIDTH)
  w_spec = pl.BlockSpec((D_MODEL, 3 * ATT_QKV_WIDTH), lambda i, g: (0, 0),
                        pipeline_mode=pl.Buffered(1))

  def x_map(i, g):
    return jnp.minimum(i + g // (N_GROUPS - 1), s // tm - 1), 0

  return pl.pallas_call(
      _qkv_kernel,
      out_shape=[jax.ShapeDtypeStruct((d, s // d, ATT_QKV_WIDTH), BF16) for _, d in ATT_GROUPS],
      grid=(s // tm, N_GROUPS),
      in_specs=[pl.BlockSpec((tm, D_MODEL), x_map), w_spec],
      out_specs=[pl.BlockSpec((d, tm // d, ATT_QKV_WIDTH), lambda i, g: (0, i, 0))
                 for _, d in ATT_GROUPS],
      scratch_shapes=(
          [pltpu.VMEM((tm, D_MODEL), BF16)] * N_GROUPS
          + [pltpu.VMEM((D_MODEL // LANES, tm, LANES), F32)] * (N_GROUPS - 1)),
      compiler_params=pltpu.CompilerParams(
          dimension_semantics=("arbitrary", "arbitrary"),
          vmem_limit_bytes=VMEM_LIMIT_BYTES,
      ),
      name="qkv_projection",
  )(x2d, w_bf16)


def _attention_tile(cur, prev, bias, store):
  nq = ATT_BLOCK
  lane = lax.broadcasted_iota(jnp.int32, (nq, LANES), 1)
  low_half = lane < HEAD_DIM
  ones = jnp.ones((2 * nq, LANES), BF16)

  def pair_tile(x):
    return jnp.where(low_half, x[:nq], x[nq:])

  for p in range(HEAD_PAIRS):
    q2 = cur(0, p)
    k2 = jnp.concatenate([prev(1, p), cur(1, p)], axis=0)
    v2 = jnp.concatenate([prev(2, p), cur(2, p)], axis=0)
    zero = jnp.zeros_like(q2)
    qs = jnp.concatenate([jnp.where(low_half, q2, zero), jnp.where(low_half, zero, q2)], axis=0)
    sc = lax.dot_general(qs, k2, (((1,), (1,)), ((), ())), preferred_element_type=F32)
    sc = sc + bias
    m = jnp.max(sc, axis=1, keepdims=True)
    e = jnp.exp(sc - m).astype(BF16)
    ol = jnp.dot(e, jnp.concatenate([v2, ones], axis=1), preferred_element_type=F32)
    store(p, pair_tile(ol[:, :LANES]), pair_tile(jnp.broadcast_to(m, (2 * nq, LANES))),
          pair_tile(ol[:, LANES:]))


def _step_tiles(d, sb, j):
  nt = ATT_TILES_PER_STEP
  bpr = TILES_PER_SUPER // d
  if bpr >= nt:
    assert bpr % nt == 0
    spr = bpr // nt
    b0 = (j % spr) * nt
    return sb * bpr + b0, (lambda t: j // spr), (lambda t: b0 + t)
  assert bpr == 1
  return sb, (lambda t: j * nt + t), (lambda t: 0)


def _attention_kernel(*refs):
  in_refs = refs[:2 * N_GROUPS]
  o_ref = refs[2 * N_GROUPS]
  acc_ref, max_ref, sum_ref, bias_ref = refs[2 * N_GROUPS + 1:]
  sb = pl.program_id(0)
  j = pl.program_id(1)
  nq = ATT_BLOCK
  nt = ATT_TILES_PER_STEP

  @pl.when(jnp.logical_and(sb == 0, j == 0))
  def _():
    row = lax.broadcasted_iota(jnp.int32, (2 * nq, 2 * nq), 0) % nq
    col = lax.broadcasted_iota(jnp.int32, (2 * nq, 2 * nq), 1)
    for hp in range(2):
      first_col = row if hp else jnp.maximum(row, nq)
      valid = jnp.logical_and(col >= first_col, col - nq <= row)
      bias_ref[hp] = jnp.where(valid, 0.0, -jnp.inf).astype(F32)

  def cols(sec, p):
    return pl.ds(sec * GROUP_WIDTH + p * LANES, LANES)

  for t in range(nt):
    for g, (_, d) in enumerate(ATT_GROUPS):
      cur_ref, prev_ref = in_refs[2 * g:2 * g + 2]
      first_blk, res, loc = _step_tiles(d, sb, j)
      if cur_ref.shape[0] == 1:
        cur = lambda sec, p, t=t: cur_ref[0, t * nq:(t + 1) * nq, cols(sec, p)]
        if t == 0:
          prev = lambda sec, p: prev_ref[0, :, cols(sec, p)]
          has_prev = (first_blk > 0).astype(jnp.int32)
        else:
          prev = lambda sec, p, t=t: cur_ref[0, (t - 1) * nq:t * nq, cols(sec, p)]
          has_prev = 1
      else:
        cur = lambda sec, p, t=t: cur_ref[t, :, cols(sec, p)]
        prev = lambda sec, p, t=t: prev_ref[t, :, cols(sec, p)]
        has_prev = (first_blk > 0).astype(jnp.int32)
      r, b = res(t), loc(t)
      if d == 1:
        rows = pl.ds(pl.multiple_of(b * nq, nq), nq)
      else:
        rows = pl.ds(b * nq * d + r, nq, stride=d)

      def store(p, o, m, l, g=g, rows=rows):
        acc_ref[g, p, rows, :] = o
        max_ref[g, p, rows, :] = m
        sum_ref[g, p, rows, :] = l

      _attention_tile(cur, prev, bias_ref[has_prev], store)

  @pl.when(j == TILES_PER_SUPER // nt - 1)
  def _():
    def body(c, carry):
      rows = pl.ds(pl.multiple_of(c * MERGE_ROWS, MERGE_ROWS), MERGE_ROWS)
      for p in range(HEAD_PAIRS):
        ms = [max_ref[g, p, rows, :] for g in range(N_GROUPS)]
        top = functools.reduce(jnp.maximum, ms)
        ws = [jnp.exp(m - top) for m in ms]
        num = functools.reduce(jnp.add, [w * acc_ref[g, p, rows, :] for g, w in enumerate(ws)])
        den = functools.reduce(jnp.add, [w * sum_ref[g, p, rows, :] for g, w in enumerate(ws)])
        o_ref[rows, p * LANES:(p + 1) * LANES] = num / den
      return carry
    lax.fori_loop(0, SUPER_BLOCK // MERGE_ROWS, body, 0, unroll=2)


def _attention(qkv_groups):
  s = qkv_groups[0].shape[1]
  nt = ATT_TILES_PER_STEP
  in_specs, args = [], []
  for (_, d), arr in zip(ATT_GROUPS, qkv_groups):
    if TILES_PER_SUPER // d >= nt:
      def cur_map(sb, j, d=d):
        first_blk, res, _ = _step_tiles(d, sb, j)
        return res(0), first_blk // nt, 0

      def prev_map(sb, j, d=d):
        first_blk, res, _ = _step_tiles(d, sb, j)
        return res(0), jnp.maximum(first_blk - 1, 0), 0

      in_specs += [pl.BlockSpec((1, nt * ATT_BLOCK, ATT_QKV_WIDTH), cur_map),
                   pl.BlockSpec((1, ATT_BLOCK, ATT_QKV_WIDTH), prev_map)]
    else:
      def cur_map(sb, j, d=d):
        first_blk, res, _ = _step_tiles(d, sb, j)
        return res(0) // nt, first_blk, 0

      def prev_map(sb, j, d=d):
        first_blk, res, _ = _step_tiles(d, sb, j)
        return res(0) // nt, jnp.maximum(first_blk - 1, 0), 0

      in_specs += [pl.BlockSpec((nt, ATT_BLOCK, ATT_QKV_WIDTH), cur_map),
                   pl.BlockSpec((nt, ATT_BLOCK, ATT_QKV_WIDTH), prev_map)]
    args += [arr] * 2
  scratch = pltpu.VMEM((N_GROUPS, HEAD_PAIRS, SUPER_BLOCK, LANES), F32)
  return pl.pallas_call(
      _attention_kernel,
      out_shape=jax.ShapeDtypeStruct((s, GROUP_WIDTH), F32),
      grid=(s // SUPER_BLOCK, TILES_PER_SUPER // nt),
      in_specs=in_specs,
      out_specs=pl.BlockSpec((SUPER_BLOCK, GROUP_WIDTH), lambda sb, j: (sb, 0),
                             pipeline_mode=pl.Buffered(1)),
      scratch_shapes=[scratch] * 3 + [pltpu.VMEM((2, 2 * ATT_BLOCK, 2 * ATT_BLOCK), F32)],
      compiler_params=pltpu.CompilerParams(
          dimension_semantics=("arbitrary", "arbitrary"),
          vmem_limit_bytes=VMEM_LIMIT_BYTES,
      ),
      name="dilated_attention",
  )(*args)


def _silu(x):
  return x * jax.nn.sigmoid(x)


def _tail_kernel(alpha, x_ref, att_ref, wga_ref, wh_ref, wb_ref, wc_ref, wgc_ref, wg1_ref, wg2_ref,
                 cw_ref, wco_ref, wao_ref, bg_ref, wo_ref, lg_ref, lb_ref, o_ref, u_ref):
  tm = x_ref.shape[0]
  chunk = tm // TAIL_ROW_CHUNKS

  def dot(a, w_ref):
    return jnp.dot(a, w_ref[...], preferred_element_type=F32)

  @pl.when(pl.program_id(0) == 0)
  def _():
    u_ref[0:CONV_HALO, :] = jnp.zeros((CONV_HALO, CONV_WIDTH), F32)

  def row_chunk(lo):
    rows = pl.ds(lo, chunk)
    xb = x_ref[rows, :].astype(BF16)
    u_ref[pl.ds(CONV_HALO + lo, chunk), :] = dot(xb, wc_ref) * dot(xb, wh_ref)
    yield
    p_b, p_gc = dot(xb, wb_ref), dot(xb, wgc_ref)
    conv = cw_ref[CONV_K - 1:CONV_K, :] * u_ref[pl.ds(CONV_HALO + lo, chunk), :]
    for k in range(CONV_K - 1):
      off = CONV_HALO + lo - (CONV_K - 1 - k)
      conv = conv + cw_ref[k:k + 1, :] * u_ref[pl.ds(off, chunk), :]
    a_conv = ((p_b * conv) * _silu(p_gc)).astype(BF16)
    yield
    y_conv = dot(a_conv, wco_ref)
    a_att = (att_ref[rows, :] * _silu(dot(xb, wga_ref))).astype(BF16)
    yield
    y_att = dot(a_att, wao_ref)
    m_c = jax.nn.sigmoid(dot(xb, wg1_ref) + bg_ref[:, :D_MODEL]) * y_conv
    yield
    g_a = jax.nn.sigmoid(dot(xb, wg2_ref) + bg_ref[:, D_MODEL:])
    merged = (m_c + g_a * y_att).astype(BF16)
    yield
    y = alpha * x_ref[rows, :] + dot(merged, wo_ref)
    mu = jnp.mean(y, axis=-1, keepdims=True)
    yc = y - mu
    var = jnp.mean(yc * yc, axis=-1, keepdims=True)
    o_ref[rows, :] = yc * lax.rsqrt(var + LN_EPS) * lg_ref[...] + lb_ref[...]
    yield

  streams = [row_chunk(k * chunk) for k in range(TAIL_ROW_CHUNKS)]
  for stage in range(TAIL_STAGES):
    for stream in streams:
      next(stream)
    if stage == 1:
      u_ref[0:CONV_HALO, :] = u_ref[tm:tm + CONV_HALO, :]


def _tail(x2d, att, w_bf16, conv_w, w_conv_out, w_att_out, b_gate, w_o, ln_g, ln_b, alpha):
  s = x2d.shape[0]
  tm = TAIL_ROW_TILE

  def whole(arr):
    return pl.BlockSpec(arr.shape, lambda i: (0,) * arr.ndim, pipeline_mode=pl.Buffered(1))

  def w_cols(off, width):
    assert off % width == 0
    return pl.BlockSpec((D_MODEL, width), lambda i: (0, off // width), pipeline_mode=pl.Buffered(1))

  w_specs = [w_cols(G_ATT_OFF, GROUP_WIDTH)] + [
      w_cols(off, CONV_WIDTH)
      for off in (H_OFF, B_OFF, C_OFF, G_CONV_OFF, GATE_OFF, GATE_OFF + D_MODEL)]
  others = (conv_w, w_conv_out, w_att_out, b_gate, w_o, ln_g, ln_b)
  return pl.pallas_call(
      functools.partial(_tail_kernel, alpha),
      out_shape=jax.ShapeDtypeStruct((s, D_MODEL), F32),
      grid=(s // tm,),
      in_specs=[
          pl.BlockSpec((tm, D_MODEL), lambda i: (i, 0)),
          pl.BlockSpec((tm, GROUP_WIDTH), lambda i: (i, 0)),
      ] + w_specs + [whole(w) for w in others],
      out_specs=pl.BlockSpec((tm, D_MODEL), lambda i: (i, 0)),
      scratch_shapes=[pltpu.VMEM((CONV_HALO + tm, CONV_WIDTH), F32)],
      compiler_params=pltpu.CompilerParams(
          dimension_semantics=("arbitrary",),
          vmem_limit_bytes=VMEM_LIMIT_BYTES,
      ),
      name="conv_merge_norm",
  )(x2d, att, *([w_bf16] * len(w_specs)), *others)


def _layer(x2d, w_in, conv_w, w_conv_out, w_att_out, b_gate, w_o, ln_g, ln_b, alpha):
  assert w_in.shape == (D_MODEL, GATE_OFF + 2 * D_MODEL)
  w_bf16 = w_in.astype(BF16)
  qkv_groups = _qkv_projection(x2d, w_bf16)
  att = _attention(qkv_groups)
  return _tail(x2d, att, w_bf16, conv_w, w_conv_out.astype(BF16), w_att_out.astype(BF16),
               b_gate.reshape(1, -1), w_o.astype(BF16), ln_g.reshape(1, -1), ln_b.reshape(1, -1),
               alpha)


def kernel(x, w_in, conv_w, w_conv_out, w_att_out, b_gate, w_o, ln_g, ln_b):
  batch, seq, d_model = x.shape
  depth = w_in.shape[0]
  assert d_model == D_MODEL and seq % SUPER_BLOCK == 0
  assert HEAD_DIM ** -0.5 == 0.125
  alpha = (2.0 * depth) ** 0.25
  outs = []
  for b in range(batch):
    h = x[b]
    for layer in range(depth):
      h = _layer(h, w_in[layer], conv_w[layer], w_conv_out[layer], w_att_out[layer],
                 b_gate[layer], w_o[layer], ln_g[layer], ln_b[layer], alpha)
    outs.append(h)
  return jnp.stack(outs)
```

```python
import functools

import jax
import jax.numpy as jnp
from jax import lax
from jax.experimental import pallas as pl
from jax.experimental.pallas import tpu as pltpu

D_MODEL = 1024
HEAD_DIM = 64
HEADS_PER_GROUP = 8
ATT_GROUPS = ((128, 1), (512, 4), (2048, 16))
N_GROUPS = len(ATT_GROUPS)
GROUP_WIDTH = HEADS_PER_GROUP * HEAD_DIM
ATT_QKV_WIDTH = N_GROUPS * GROUP_WIDTH
CONV_WIDTH = D_MODEL
CONV_K = 3
LN_EPS = 1e-5
LANES = 128

ATT_BLOCK = 128
assert all(w // d == ATT_BLOCK for w, d in ATT_GROUPS)
SUPER_BLOCK = ATT_BLOCK * max(d for _, d in ATT_GROUPS)
TILES_PER_SUPER = SUPER_BLOCK // ATT_BLOCK
ATT_TILES_PER_STEP = 4

QKV_ROW_TILE = 1024
QKV_DOT_ROW_CHUNKS = 2
TAIL_ROW_TILE = 1024
TAIL_ROW_CHUNKS = 4
TAIL_STAGES = 6
CONV_HALO = 8
MERGE_ROWS = 32
HEAD_PAIRS = HEADS_PER_GROUP // 2
assert 2 * HEAD_DIM == LANES

VMEM_LIMIT_BYTES = 56 * 1024 * 1024

F32 = jnp.float32
BF16 = jnp.bfloat16

Q_OFF, K_OFF, V_OFF = 0, ATT_QKV_WIDTH, 2 * ATT_QKV_WIDTH
G_ATT_OFF = 3 * ATT_QKV_WIDTH
H_OFF = G_ATT_OFF + GROUP_WIDTH
B_OFF = H_OFF + CONV_WIDTH
C_OFF = B_OFF + CONV_WIDTH
G_CONV_OFF = C_OFF + CONV_WIDTH
GATE_OFF = G_CONV_OFF + CONV_WIDTH


def _qkv_kernel(x_ref, w_ref, *refs):
  tm = x_ref.shape[0]
  out_refs = refs[:N_GROUPS]
  xp_ref = refs[N_GROUPS:2 * N_GROUPS]
  xl_ref = refs[2 * N_GROUPS:]
  dil = [d for _, d in ATT_GROUPS]
  assert dil[0] == 1
  scale = HEAD_DIM ** -0.5
  lane_tiles = [pl.ds(c * LANES, LANES) for c in range(D_MODEL // LANES)]

  def load_rows():
    def piece(c, lanes):
      xp_ref[0][:, lanes] = x_ref[:, lanes].astype(BF16)
      xl_ref[0][c] = x_ref[:, lanes]
    return [functools.partial(piece, c, lanes) for c, lanes in enumerate(lane_tiles)]

  def permute_rows(g):
    q = dil[g + 1] // dil[g]
    assert dil[g + 1] == q * dil[g]
    n, n_next = tm // dil[g], tm // dil[g + 1]

    def piece(r, a):
      dst = pl.ds((a * dil[g] + r) * n_next, n_next)
      for c, lanes in enumerate(lane_tiles):
        rows = xl_ref[g][c, pl.ds(r * n + a, n_next, stride=q), :]
        xp_ref[g + 1][dst, lanes] = rows.astype(BF16)
        if g + 2 < N_GROUPS:
          xl_ref[g + 1][c, dst, :] = rows
    return [functools.partial(piece, r, a) for r in range(dil[g]) for a in range(q)]

  def project(g, pieces):
    n = tm // dil[g]
    row_chunk = tm // QKV_DOT_ROW_CHUNKS
    n_dots = 3 * QKV_DOT_ROW_CHUNKS
    k = 0
    for sec, off in enumerate((Q_OFF, K_OFF, V_OFF)):
      w_cols = pl.ds(off + g * GROUP_WIDTH, GROUP_WIDTH)
      for rc in range(QKV_DOT_ROW_CHUNKS):
        lo = rc * row_chunk
        res = jnp.dot(xp_ref[g][lo:lo + row_chunk, :], w_ref[:, w_cols],
                      preferred_element_type=F32)
        if sec == 0:
          res = res * scale
        res = res.astype(BF16)
        assert row_chunk % n == 0 or n % row_chunk == 0
        step = min(n, row_chunk)
        for s0 in range(0, row_chunk, step):
          r, l0 = divmod(lo + s0, n)
          out_refs[g][r, l0:l0 + step, sec * GROUP_WIDTH:(sec + 1) * GROUP_WIDTH] = (
              res[s0:s0 + step])
        k += 1
        for piece in pieces[(k - 1) * len(pieces) // n_dots:k * len(pieces) // n_dots]:
          piece()

  @pl.when(jnp.logical_and(pl.program_id(0) == 0, pl.program_id(1) == 0))
  def _():
    for piece in load_rows():
      piece()

  for g in range(N_GROUPS):
    @pl.when(pl.program_id(1) == g)
    def _(g=g):
      project(g, permute_rows(g) if g + 1 < N_GROUPS else load_rows())


def _qkv_projection(x2d, w_bf16):
  s = x2d.shape[0]
  tm = QKV_ROW_TILE

  assert (Q_OFF, K_OFF, V_OFF) == (0, ATT_QKV_WIDTH, 2 * ATT_QKV_WIDTH)
  w_spec = pl.BlockSpec((D_MODEL, 3 * ATT_QKV_WIDTH), lambda i, g: (0, 0),
                        pipeline_mode=pl.Buffered(1))

  def x_map(i, g):
    return jnp.minimum(i + g // (N_GROUPS - 1), s // tm - 1), 0

  def out_map(k, i, g):
    return 0, jnp.minimum(i + (g > k).astype(jnp.int32), s // tm - 1), 0

  return pl.pallas_call(
      _qkv_kernel,
      out_shape=[jax.ShapeDtypeStruct((d, s // d, ATT_QKV_WIDTH), BF16) for _, d in ATT_GROUPS],
      grid=(s // tm, N_GROUPS),
      in_specs=[pl.BlockSpec((tm, D_MODEL), x_map), w_spec],
      out_specs=[pl.BlockSpec((d, tm // d, ATT_QKV_WIDTH), functools.partial(out_map, k))
                 for k, (_, d) in enumerate(ATT_GROUPS)],
      scratch_shapes=(
          [pltpu.VMEM((tm, D_MODEL), BF16)] * N_GROUPS
          + [pltpu.VMEM((D_MODEL // LANES, tm, LANES), F32)] * (N_GROUPS - 1)),
      compiler_params=pltpu.CompilerParams(
          dimension_semantics=("arbitrary", "arbitrary"),
          vmem_limit_bytes=VMEM_LIMIT_BYTES,
      ),
      name="qkv_projection",
  )(x2d, w_bf16)


def _attention_tile(cur, prev, bias, store):
  nq = ATT_BLOCK
  lane = lax.broadcasted_iota(jnp.int32, (nq, LANES), 1)
  low_half = lane < HEAD_DIM
  ones = jnp.ones((2 * nq, LANES), BF16)

  def pair_tile(x):
    return jnp.where(low_half, x[:nq], x[nq:])

  for p in range(HEAD_PAIRS):
    q2 = cur(0, p)
    k2 = jnp.concatenate([prev(1, p), cur(1, p)], axis=0)
    v2 = jnp.concatenate([prev(2, p), cur(2, p)], axis=0)
    zero = jnp.zeros_like(q2)
    qs = jnp.concatenate([jnp.where(low_half, q2, zero), jnp.where(low_half, zero, q2)], axis=0)
    sc = lax.dot_general(qs, k2, (((1,), (1,)), ((), ())), preferred_element_type=F32)
    sc = sc + bias
    m = jnp.max(sc, axis=1, keepdims=True)
    e = jnp.exp(sc - m).astype(BF16)
    ol = jnp.dot(e, jnp.concatenate([v2, ones], axis=1), preferred_element_type=F32)
    store(p, pair_tile(ol[:, :LANES]), pair_tile(jnp.broadcast_to(m, (2 * nq, LANES))),
          pair_tile(ol[:, LANES:]))


def _step_tiles(d, sb, j):
  nt = ATT_TILES_PER_STEP
  bpr = TILES_PER_SUPER // d
  if bpr >= nt:
    assert bpr % nt == 0
    spr = bpr // nt
    b0 = (j % spr) * nt
    return sb * bpr + b0, (lambda t: j // spr), (lambda t: b0 + t)
  assert bpr == 1
  return sb, (lambda t: j * nt + t), (lambda t: 0)


def _attention_kernel(*refs):
  in_refs = refs[:2 * N_GROUPS]
  o_ref = refs[2 * N_GROUPS]
  acc_ref, max_ref, sum_ref, bias_ref = refs[2 * N_GROUPS + 1:]
  sb = pl.program_id(0)
  j = pl.program_id(1)
  nq = ATT_BLOCK
  nt = ATT_TILES_PER_STEP

  @pl.when(jnp.logical_and(sb == 0, j == 0))
  def _():
    row = lax.broadcasted_iota(jnp.int32, (2 * nq, 2 * nq), 0) % nq
    col = lax.broadcasted_iota(jnp.int32, (2 * nq, 2 * nq), 1)
    for hp in range(2):
      first_col = row if hp else jnp.maximum(row, nq)
      valid = jnp.logical_and(col >= first_col, col - nq <= row)
      bias_ref[hp] = jnp.where(valid, 0.0, -jnp.inf).astype(F32)

  def cols(sec, p):
    return pl.ds(sec * GROUP_WIDTH + p * LANES, LANES)

  for t in range(nt):
    for g, (_, d) in enumerate(ATT_GROUPS):
      cur_ref, prev_ref = in_refs[2 * g:2 * g + 2]
      first_blk, res, loc = _step_tiles(d, sb, j)
      if cur_ref.shape[0] == 1:
        cur = lambda sec, p, t=t: cur_ref[0, t * nq:(t + 1) * nq, cols(sec, p)]
        if t == 0:
          prev = lambda sec, p: prev_ref[0, :, cols(sec, p)]
          has_prev = (first_blk > 0).astype(jnp.int32)
        else:
          prev = lambda sec, p, t=t: cur_ref[0, (t - 1) * nq:t * nq, cols(sec, p)]
          has_prev = 1
      else:
        cur = lambda sec, p, t=t: cur_ref[t, :, cols(sec, p)]
        prev = lambda sec, p, t=t: prev_ref[t, :, cols(sec, p)]
        has_prev = (first_blk > 0).astype(jnp.int32)
      r, b = res(t), loc(t)
      if d == 1:
        rows = pl.ds(pl.multiple_of(b * nq, nq), nq)
      else:
        rows = pl.ds(b * nq * d + r, nq, stride=d)

      def store(p, o, m, l, g=g, rows=rows):
        acc_ref[g, p, rows, :] = o
        max_ref[g, p, rows, :] = m
        sum_ref[g, p, rows, :] = l

      _attention_tile(cur, prev, bias_ref[has_prev], store)

  @pl.when(j == TILES_PER_SUPER // nt - 1)
  def _():
    def body(c, carry):
      rows = pl.ds(pl.multiple_of(c * MERGE_ROWS, MERGE_ROWS), MERGE_ROWS)
      for p in range(HEAD_PAIRS):
        ms = [max_ref[g, p, rows, :] for g in range(N_GROUPS)]
        top = functools.reduce(jnp.maximum, ms)
        ws = [jnp.exp(m - top) for m in ms]
        num = functools.reduce(jnp.add, [w * acc_ref[g, p, rows, :] for g, w in enumerate(ws)])
        den = functools.reduce(jnp.add, [w * sum_ref[g, p, rows, :] for g, w in enumerate(ws)])
        o_ref[rows, p * LANES:(p + 1) * LANES] = num / den
      return carry
    lax.fori_loop(0, SUPER_BLOCK // MERGE_ROWS, body, 0, unroll=2)


def _attention(qkv_groups):
  s = qkv_groups[0].shape[1]
  nt = ATT_TILES_PER_STEP
  in_specs, args = [], []
  for (_, d), arr in zip(ATT_GROUPS, qkv_groups):
    if TILES_PER_SUPER // d >= nt:
      def cur_map(sb, j, d=d):
        first_blk, res, _ = _step_tiles(d, sb, j)
        return res(0), first_blk // nt, 0

      def prev_map(sb, j, d=d):
        first_blk, res, _ = _step_tiles(d, sb, j)
        return res(0), jnp.maximum(first_blk - 1, 0), 0

      in_specs += [pl.BlockSpec((1, nt * ATT_BLOCK, ATT_QKV_WIDTH), cur_map),
                   pl.BlockSpec((1, ATT_BLOCK, ATT_QKV_WIDTH), prev_map)]
    else:
      def cur_map(sb, j, d=d):
        first_blk, res, _ = _step_tiles(d, sb, j)
        return res(0) // nt, first_blk, 0

      def prev_map(sb, j, d=d):
        first_blk, res, _ = _step_tiles(d, sb, j)
        return res(0) // nt, jnp.maximum(first_blk - 1, 0), 0

      in_specs += [pl.BlockSpec((nt, ATT_BLOCK, ATT_QKV_WIDTH), cur_map),
                   pl.BlockSpec((nt, ATT_BLOCK, ATT_QKV_WIDTH), prev_map)]
    args += [arr] * 2
  scratch = pltpu.VMEM((N_GROUPS, HEAD_PAIRS, SUPER_BLOCK, LANES), F32)
  return pl.pallas_call(
      _attention_kernel,
      out_shape=jax.ShapeDtypeStruct((s, GROUP_WIDTH), F32),
      grid=(s // SUPER_BLOCK, TILES_PER_SUPER // nt),
      in_specs=in_specs,
      out_specs=pl.BlockSpec((SUPER_BLOCK, GROUP_WIDTH), lambda sb, j: (sb, 0),
                             pipeline_mode=pl.Buffered(1)),
      scratch_shapes=[scratch] * 3 + [pltpu.VMEM((2, 2 * ATT_BLOCK, 2 * ATT_BLOCK), F32)],
      compiler_params=pltpu.CompilerParams(
          dimension_semantics=("arbitrary", "arbitrary"),
          vmem_limit_bytes=VMEM_LIMIT_BYTES,
      ),
      name="dilated_attention",
  )(*args)


def _silu(x):
  return x * jax.nn.sigmoid(x)


def _tail_kernel(alpha, x_ref, att_ref, wga_ref, wh_ref, wb_ref, wc_ref, wgc_ref, wg1_ref, wg2_ref,
                 cw_ref, wco_ref, wao_ref, bg_ref, wo_ref, lg_ref, lb_ref, o_ref, u_ref):
  tm = x_ref.shape[0]
  chunk = tm // TAIL_ROW_CHUNKS

  def dot(a, w_ref):
    return jnp.dot(a, w_ref[...], preferred_element_type=F32)

  @pl.when(pl.program_id(0) == 0)
  def _():
    u_ref[0:CONV_HALO, :] = jnp.zeros((CONV_HALO, CONV_WIDTH), F32)

  def row_chunk(lo):
    rows = pl.ds(lo, chunk)
    xb = x_ref[rows, :].astype(BF16)
    u_ref[pl.ds(CONV_HALO + lo, chunk), :] = dot(xb, wc_ref) * dot(xb, wh_ref)
    yield
    p_b, p_gc = dot(xb, wb_ref), dot(xb, wgc_ref)
    conv = cw_ref[CONV_K - 1:CONV_K, :] * u_ref[pl.ds(CONV_HALO + lo, chunk), :]
    for k in range(CONV_K - 1):
      off = CONV_HALO + lo - (CONV_K - 1 - k)
      conv = conv + cw_ref[k:k + 1, :] * u_ref[pl.ds(off, chunk), :]
    a_conv = ((p_b * conv) * _silu(p_gc)).astype(BF16)
    yield
    y_conv = dot(a_conv, wco_ref)
    a_att = (att_ref[rows, :] * _silu(dot(xb, wga_ref))).astype(BF16)
    yield
    y_att = dot(a_att, wao_ref)
    m_c = jax.nn.sigmoid(dot(xb, wg1_ref) + bg_ref[:, :D_MODEL]) * y_conv
    yield
    g_a = jax.nn.sigmoid(dot(xb, wg2_ref) + bg_ref[:, D_MODEL:])
    merged = (m_c + g_a * y_att).astype(BF16)
    yield
    y = alpha * x_ref[rows, :] + dot(merged, wo_ref)
    mu = jnp.mean(y, axis=-1, keepdims=True)
    yc = y - mu
    var = jnp.mean(yc * yc, axis=-1, keepdims=True)
    o_ref[rows, :] = yc * lax.rsqrt(var + LN_EPS) * lg_ref[...] + lb_ref[...]
    yield

  streams = [row_chunk(k * chunk) for k in range(TAIL_ROW_CHUNKS)]
  for stage in range(TAIL_STAGES):
    for stream in streams:
      next(stream)
    if stage == 1:
      u_ref[0:CONV_HALO, :] = u_ref[tm:tm + CONV_HALO, :]


def _tail(x2d, att, w_bf16, conv_w, w_conv_out, w_att_out, b_gate, w_o, ln_g, ln_b, alpha):
  s = x2d.shape[0]
  tm = TAIL_ROW_TILE

  def whole(arr):
    return pl.BlockSpec(arr.shape, lambda i: (0,) * arr.ndim, pipeline_mode=pl.Buffered(1))

  def w_cols(off, width):
    assert off % width == 0
    return pl.BlockSpec((D_MODEL, width), lambda i: (0, off // width), pipeline_mode=pl.Buffered(1))

  w_specs = [w_cols(G_ATT_OFF, GROUP_WIDTH)] + [
      w_cols(off, CONV_WIDTH)
      for off in (H_OFF, B_OFF, C_OFF, G_CONV_OFF, GATE_OFF, GATE_OFF + D_MODEL)]
  others = (conv_w, w_conv_out, w_att_out, b_gate, w_o, ln_g, ln_b)
  return pl.pallas_call(
      functools.partial(_tail_kernel, alpha),
      out_shape=jax.ShapeDtypeStruct((s, D_MODEL), F32),
      grid=(s // tm,),
      in_specs=[
          pl.BlockSpec((tm, D_MODEL), lambda i: (i, 0)),
          pl.BlockSpec((tm, GROUP_WIDTH), lambda i: (i, 0)),
      ] + w_specs + [whole(w) for w in others],
      out_specs=pl.BlockSpec((tm, D_MODEL), lambda i: (i, 0)),
      scratch_shapes=[pltpu.VMEM((CONV_HALO + tm, CONV_WIDTH), F32)],
      compiler_params=pltpu.CompilerParams(
          dimension_semantics=("arbitrary",),
          vmem_limit_bytes=VMEM_LIMIT_BYTES,
      ),
      name="conv_merge_norm",
  )(x2d, att, *([w_bf16] * len(w_specs)), *others)


def _layer(x2d, w_in, conv_w, w_conv_out, w_att_out, b_gate, w_o, ln_g, ln_b, alpha):
  assert w_in.shape == (D_MODEL, GATE_OFF + 2 * D_MODEL)
  w_bf16 = w_in.astype(BF16)
  qkv_groups = _qkv_projection(x2d, w_bf16)
  att = _attention(qkv_groups)
  return _tail(x2d, att, w_bf16, conv_w, w_conv_out.astype(BF16), w_att_out.astype(BF16),
               b_gate.reshape(1, -1), w_o.astype(BF16), ln_g.reshape(1, -1), ln_b.reshape(1, -1),
               alpha)


def kernel(x, w_in, conv_w, w_conv_out, w_att_out, b_gate, w_o, ln_g, ln_b):
  batch, seq, d_model = x.shape
  depth = w_in.shape[0]
  assert d_model == D_MODEL and seq % SUPER_BLOCK == 0
  assert HEAD_DIM ** -0.5 == 0.125
  alpha = (2.0 * depth) ** 0.25
  outs = []
  for b in range(batch):
    h = x[b]
    for layer in range(depth):
      h = _layer(h, w_in[layer], conv_w[layer], w_conv_out[layer], w_att_out[layer],
                 b_gate[layer], w_o[layer], ln_g[layer], ln_b[layer], alpha)
    outs.append(h)
  return jnp.stack(outs)
```

```python
import functools

import jax
import jax.numpy as jnp
from jax import lax
from jax.experimental import pallas as pl
from jax.experimental.pallas import tpu as pltpu

D_MODEL = 1024
HEAD_DIM = 64
HEADS_PER_GROUP = 8
ATT_GROUPS = ((128, 1), (512, 4), (2048, 16))
N_GROUPS = len(ATT_GROUPS)
GROUP_WIDTH = HEADS_PER_GROUP * HEAD_DIM
ATT_QKV_WIDTH = N_GROUPS * GROUP_WIDTH
CONV_WIDTH = D_MODEL
CONV_K = 3
LN_EPS = 1e-5
LANES = 128

ATT_BLOCK = 128
assert all(w // d == ATT_BLOCK for w, d in ATT_GROUPS)
SUPER_BLOCK = ATT_BLOCK * max(d for _, d in ATT_GROUPS)
TILES_PER_SUPER = SUPER_BLOCK // ATT_BLOCK
ATT_TILES_PER_STEP = 4

QKV_ROW_TILE = 1024
QKV_DOT_ROW_CHUNKS = 2
TAIL_ROW_TILE = 1024
TAIL_ROW_CHUNKS = 4
TAIL_STAGES = 6
CONV_HALO = 8
MERGE_ROWS = 32
HEAD_PAIRS = HEADS_PER_GROUP // 2
assert 2 * HEAD_DIM == LANES
KV_WIDTH = 2 * GROUP_WIDTH
QKV_COL_OFF = (KV_WIDTH, 0, GROUP_WIDTH)

VMEM_LIMIT_BYTES = 56 * 1024 * 1024

F32 = jnp.float32
BF16 = jnp.bfloat16

Q_OFF, K_OFF, V_OFF = 0, ATT_QKV_WIDTH, 2 * ATT_QKV_WIDTH
G_ATT_OFF = 3 * ATT_QKV_WIDTH
H_OFF = G_ATT_OFF + GROUP_WIDTH
B_OFF = H_OFF + CONV_WIDTH
C_OFF = B_OFF + CONV_WIDTH
G_CONV_OFF = C_OFF + CONV_WIDTH
GATE_OFF = G_CONV_OFF + CONV_WIDTH


def _qkv_kernel(x_ref, w_ref, *refs):
  tm = x_ref.shape[0]
  out_refs = refs[:N_GROUPS]
  xp_ref = refs[N_GROUPS:2 * N_GROUPS]
  xl_ref = refs[2 * N_GROUPS:]
  dil = [d for _, d in ATT_GROUPS]
  assert dil[0] == 1
  scale = HEAD_DIM ** -0.5
  lane_tiles = [pl.ds(c * LANES, LANES) for c in range(D_MODEL // LANES)]

  def load_rows():
    def piece(c, lanes):
      xp_ref[0][:, lanes] = x_ref[:, lanes].astype(BF16)
      xl_ref[0][c] = x_ref[:, lanes]
    return [functools.partial(piece, c, lanes) for c, lanes in enumerate(lane_tiles)]

  def permute_rows(g):
    q = dil[g + 1] // dil[g]
    assert dil[g + 1] == q * dil[g]
    n, n_next = tm // dil[g], tm // dil[g + 1]

    def piece(r, a):
      dst = pl.ds((a * dil[g] + r) * n_next, n_next)
      for c, lanes in enumerate(lane_tiles):
        rows = xl_ref[g][c, pl.ds(r * n + a, n_next, stride=q), :]
        xp_ref[g + 1][dst, lanes] = rows.astype(BF16)
        if g + 2 < N_GROUPS:
          xl_ref[g + 1][c, dst, :] = rows
    return [functools.partial(piece, r, a) for r in range(dil[g]) for a in range(q)]

  def project(g, pieces):
    n = tm // dil[g]
    row_chunk = tm // QKV_DOT_ROW_CHUNKS
    n_dots = 3 * QKV_DOT_ROW_CHUNKS
    k = 0
    for sec, off in enumerate((Q_OFF, K_OFF, V_OFF)):
      w_cols = pl.ds(off + g * GROUP_WIDTH, GROUP_WIDTH)
      for rc in range(QKV_DOT_ROW_CHUNKS):
        lo = rc * row_chunk
        res = jnp.dot(xp_ref[g][lo:lo + row_chunk, :], w_ref[:, w_cols],
                      preferred_element_type=F32)
        if sec == 0:
          res = res * scale
        res = res.astype(BF16)
        assert row_chunk % n == 0 or n % row_chunk == 0
        step = min(n, row_chunk)
        for s0 in range(0, row_chunk, step):
          r, l0 = divmod(lo + s0, n)
          out_refs[g][r, l0:l0 + step, pl.ds(QKV_COL_OFF[sec], GROUP_WIDTH)] = res[s0:s0 + step]
        k += 1
        for piece in pieces[(k - 1) * len(pieces) // n_dots:k * len(pieces) // n_dots]:
          piece()

  @pl.when(jnp.logical_and(pl.program_id(0) == 0, pl.program_id(1) == 0))
  def _():
    for piece in load_rows():
      piece()

  for g in range(N_GROUPS):
    @pl.when(pl.program_id(1) == g)
    def _(g=g):
      project(g, permute_rows(g) if g + 1 < N_GROUPS else load_rows())


def _qkv_projection(x2d, w_bf16):
  s = x2d.shape[0]
  tm = QKV_ROW_TILE

  assert (Q_OFF, K_OFF, V_OFF) == (0, ATT_QKV_WIDTH, 2 * ATT_QKV_WIDTH)
  w_spec = pl.BlockSpec((D_MODEL, 3 * ATT_QKV_WIDTH), lambda i, g: (0, 0),
                        pipeline_mode=pl.Buffered(1))

  def x_map(i, g):
    return jnp.minimum(i + g // (N_GROUPS - 1), s // tm - 1), 0

  return pl.pallas_call(
      _qkv_kernel,
      out_shape=[jax.ShapeDtypeStruct((d, s // d, ATT_QKV_WIDTH), BF16) for _, d in ATT_GROUPS],
      grid=(s // tm, N_GROUPS),
      in_specs=[pl.BlockSpec((tm, D_MODEL), x_map), w_spec],
      out_specs=[pl.BlockSpec((d, tm // d, ATT_QKV_WIDTH), lambda i, g: (0, i, 0))
                 for _, d in ATT_GROUPS],
      scratch_shapes=(
          [pltpu.VMEM((tm, D_MODEL), BF16)] * N_GROUPS
          + [pltpu.VMEM((D_MODEL // LANES, tm, LANES), F32)] * (N_GROUPS - 1)),
      compiler_params=pltpu.CompilerParams(
          dimension_semantics=("arbitrary", "arbitrary"),
          vmem_limit_bytes=VMEM_LIMIT_BYTES,
      ),
      name="qkv_projection",
  )(x2d, w_bf16)


def _attention_tile(cur, prev, bias, store):
  nq = ATT_BLOCK
  lane = lax.broadcasted_iota(jnp.int32, (nq, LANES), 1)
  low_half = lane < HEAD_DIM
  ones = jnp.ones((2 * nq, LANES), BF16)

  def pair_tile(x):
    return jnp.where(low_half, x[:nq], x[nq:])

  for p in range(HEAD_PAIRS):
    q2 = cur(0, p)
    k2 = jnp.concatenate([prev(1, p), cur(1, p)], axis=0)
    v2 = jnp.concatenate([prev(2, p), cur(2, p)], axis=0)
    zero = jnp.zeros_like(q2)
    qs = jnp.concatenate([jnp.where(low_half, q2, zero), jnp.where(low_half, zero, q2)], axis=0)
    sc = lax.dot_general(qs, k2, (((1,), (1,)), ((), ())), preferred_element_type=F32)
    sc = sc + bias
    m = jnp.max(sc, axis=1, keepdims=True)
    e = jnp.exp(sc - m).astype(BF16)
    ol = jnp.dot(e, jnp.concatenate([v2, ones], axis=1), preferred_element_type=F32)
    store(p, pair_tile(ol[:, :LANES]), pair_tile(jnp.broadcast_to(m, (2 * nq, LANES))),
          pair_tile(ol[:, LANES:]))


def _step_tiles(d, sb, j):
  nt = ATT_TILES_PER_STEP
  bpr = TILES_PER_SUPER // d
  if bpr >= nt:
    assert bpr % nt == 0
    spr = bpr // nt
    b0 = (j % spr) * nt
    return sb * bpr + b0, (lambda t: j // spr), (lambda t: b0 + t)
  assert bpr == 1
  return sb, (lambda t: j * nt + t), (lambda t: 0)


def _attention_kernel(*refs):
  in_refs = refs[:2 * N_GROUPS]
  o_ref = refs[2 * N_GROUPS]
  acc_ref, max_ref, sum_ref, bias_ref = refs[2 * N_GROUPS + 1:]
  sb = pl.program_id(0)
  j = pl.program_id(1)
  nq = ATT_BLOCK
  nt = ATT_TILES_PER_STEP

  @pl.when(jnp.logical_and(sb == 0, j == 0))
  def _():
    row = lax.broadcasted_iota(jnp.int32, (2 * nq, 2 * nq), 0) % nq
    col = lax.broadcasted_iota(jnp.int32, (2 * nq, 2 * nq), 1)
    for hp in range(2):
      first_col = row if hp else jnp.maximum(row, nq)
      valid = jnp.logical_and(col >= first_col, col - nq <= row)
      bias_ref[hp] = jnp.where(valid, 0.0, -jnp.inf).astype(F32)

  def cols(sec, p):
    return pl.ds(QKV_COL_OFF[sec] + p * LANES, LANES)

  for t in range(nt):
    for g, (_, d) in enumerate(ATT_GROUPS):
      cur_ref, prev_ref = in_refs[2 * g:2 * g + 2]
      first_blk, res, loc = _step_tiles(d, sb, j)
      if cur_ref.shape[0] == 1:
        cur = lambda sec, p, t=t: cur_ref[0, t * nq:(t + 1) * nq, cols(sec, p)]
        if t == 0:
          prev = lambda sec, p: prev_ref[0, :, cols(sec, p)]
          has_prev = (first_blk > 0).astype(jnp.int32)
        else:
          prev = lambda sec, p, t=t: cur_ref[0, (t - 1) * nq:t * nq, cols(sec, p)]
          has_prev = 1
      else:
        cur = lambda sec, p, t=t: cur_ref[t, :, cols(sec, p)]
        prev = lambda sec, p, t=t: prev_ref[t, :, cols(sec, p)]
        has_prev = (first_blk > 0).astype(jnp.int32)
      r, b = res(t), loc(t)
      if d == 1:
        rows = pl.ds(pl.multiple_of(b * nq, nq), nq)
      else:
        rows = pl.ds(b * nq * d + r, nq, stride=d)

      def store(p, o, m, l, g=g, rows=rows):
        acc_ref[g, p, rows, :] = o
        max_ref[g, p, rows, :] = m
        sum_ref[g, p, rows, :] = l

      _attention_tile(cur, prev, bias_ref[has_prev], store)

  @pl.when(j == TILES_PER_SUPER // nt - 1)
  def _():
    def body(c, carry):
      rows = pl.ds(pl.multiple_of(c * MERGE_ROWS, MERGE_ROWS), MERGE_ROWS)
      for p in range(HEAD_PAIRS):
        ms = [max_ref[g, p, rows, :] for g in range(N_GROUPS)]
        top = functools.reduce(jnp.maximum, ms)
        ws = [jnp.exp(m - top) for m in ms]
        num = functools.reduce(jnp.add, [w * acc_ref[g, p, rows, :] for g, w in enumerate(ws)])
        den = functools.reduce(jnp.add, [w * sum_ref[g, p, rows, :] for g, w in enumerate(ws)])
        o_ref[rows, p * LANES:(p + 1) * LANES] = num / den
      return carry
    lax.fori_loop(0, SUPER_BLOCK // MERGE_ROWS, body, 0, unroll=2)


def _attention(qkv_groups):
  s = qkv_groups[0].shape[1]
  nt = ATT_TILES_PER_STEP
  in_specs, args = [], []
  for (_, d), arr in zip(ATT_GROUPS, qkv_groups):
    if TILES_PER_SUPER // d >= nt:
      def cur_map(sb, j, d=d):
        first_blk, res, _ = _step_tiles(d, sb, j)
        return res(0), first_blk // nt, 0

      def prev_map(sb, j, d=d):
        first_blk, res, _ = _step_tiles(d, sb, j)
        return res(0), jnp.maximum(first_blk - 1, 0), 0

      in_specs += [pl.BlockSpec((1, nt * ATT_BLOCK, ATT_QKV_WIDTH), cur_map),
                   pl.BlockSpec((1, ATT_BLOCK, KV_WIDTH), prev_map)]
    else:
      def cur_map(sb, j, d=d):
        first_blk, res, _ = _step_tiles(d, sb, j)
        return res(0) // nt, first_blk, 0

      def prev_map(sb, j, d=d):
        first_blk, res, _ = _step_tiles(d, sb, j)
        return res(0) // nt, jnp.maximum(first_blk - 1, 0), 0

      in_specs += [pl.BlockSpec((nt, ATT_BLOCK, ATT_QKV_WIDTH), cur_map),
                   pl.BlockSpec((nt, ATT_BLOCK, KV_WIDTH), prev_map)]
    args += [arr] * 2
  scratch = pltpu.VMEM((N_GROUPS, HEAD_PAIRS, SUPER_BLOCK, LANES), F32)
  return pl.pallas_call(
      _attention_kernel,
      out_shape=jax.ShapeDtypeStruct((s, GROUP_WIDTH), F32),
      grid=(s // SUPER_BLOCK, TILES_PER_SUPER // nt),
      in_specs=in_specs,
      out_specs=pl.BlockSpec((SUPER_BLOCK, GROUP_WIDTH), lambda sb, j: (sb, 0),
                             pipeline_mode=pl.Buffered(1)),
      scratch_shapes=[scratch] * 3 + [pltpu.VMEM((2, 2 * ATT_BLOCK, 2 * ATT_BLOCK), F32)],
      compiler_params=pltpu.CompilerParams(
          dimension_semantics=("arbitrary", "arbitrary"),
          vmem_limit_bytes=VMEM_LIMIT_BYTES,
      ),
      name="dilated_attention",
  )(*args)


def _silu(x):
  return x * jax.nn.sigmoid(x)


def _tail_kernel(alpha, x_ref, att_ref, wga_ref, wh_ref, wb_ref, wc_ref, wgc_ref, wg1_ref, wg2_ref,
                 cw_ref, wco_ref, wao_ref, bg_ref, wo_ref, lg_ref, lb_ref, o_ref, u_ref):
  tm = x_ref.shape[0]
  chunk = tm // TAIL_ROW_CHUNKS

  def dot(a, w_ref):
    return jnp.dot(a, w_ref[...], preferred_element_type=F32)

  @pl.when(pl.program_id(0) == 0)
  def _():
    u_ref[0:CONV_HALO, :] = jnp.zeros((CONV_HALO, CONV_WIDTH), F32)

  def row_chunk(lo):
    rows = pl.ds(lo, chunk)
    xb = x_ref[rows, :].astype(BF16)
    u_ref[pl.ds(CONV_HALO + lo, chunk), :] = dot(xb, wc_ref) * dot(xb, wh_ref)
    yield
    p_b, p_gc = dot(xb, wb_ref), dot(xb, wgc_ref)
    conv = cw_ref[CONV_K - 1:CONV_K, :] * u_ref[pl.ds(CONV_HALO + lo, chunk), :]
    for k in range(CONV_K - 1):
      off = CONV_HALO + lo - (CONV_K - 1 - k)
      conv = conv + cw_ref[k:k + 1, :] * u_ref[pl.ds(off, chunk), :]
    a_conv = ((p_b * conv) * _silu(p_gc)).astype(BF16)
    yield
    y_conv = dot(a_conv, wco_ref)
    a_att = (att_ref[rows, :] * _silu(dot(xb, wga_ref))).astype(BF16)
    yield
    y_att = dot(a_att, wao_ref)
    m_c = jax.nn.sigmoid(dot(xb, wg1_ref) + bg_ref[:, :D_MODEL]) * y_conv
    yield
    g_a = jax.nn.sigmoid(dot(xb, wg2_ref) + bg_ref[:, D_MODEL:])
    merged = (m_c + g_a * y_att).astype(BF16)
    yield
    y = alpha * x_ref[rows, :] + dot(merged, wo_ref)
    mu = jnp.mean(y, axis=-1, keepdims=True)
    yc = y - mu
    var = jnp.mean(yc * yc, axis=-1, keepdims=True)
    o_ref[rows, :] = yc * lax.rsqrt(var + LN_EPS) * lg_ref[...] + lb_ref[...]
    yield

  streams = [row_chunk(k * chunk) for k in range(TAIL_ROW_CHUNKS)]
  for stage in range(TAIL_STAGES):
    for stream in streams:
      next(stream)
    if stage == 1:
      u_ref[0:CONV_HALO, :] = u_ref[tm:tm + CONV_HALO, :]


def _tail(x2d, att, w_bf16, conv_w, w_conv_out, w_att_out, b_gate, w_o, ln_g, ln_b, alpha):
  s = x2d.shape[0]
  tm = TAIL_ROW_TILE

  def whole(arr):
    return pl.BlockSpec(arr.shape, lambda i: (0,) * arr.ndim, pipeline_mode=pl.Buffered(1))

  def w_cols(off, width):
    assert off % width == 0
    return pl.BlockSpec((D_MODEL, width), lambda i: (0, off // width), pipeline_mode=pl.Buffered(1))

  w_specs = [w_cols(G_ATT_OFF, GROUP_WIDTH)] + [
      w_cols(off, CONV_WIDTH)
      for off in (H_OFF, B_OFF, C_OFF, G_CONV_OFF, GATE_OFF, GATE_OFF + D_MODEL)]
  others = (conv_w, w_conv_out, w_att_out, b_gate, w_o, ln_g, ln_b)
  return pl.pallas_call(
      functools.partial(_tail_kernel, alpha),
      out_shape=jax.ShapeDtypeStruct((s, D_MODEL), F32),
      grid=(s // tm,),
      in_specs=[
          pl.BlockSpec((tm, D_MODEL), lambda i: (i, 0)),
          pl.BlockSpec((tm, GROUP_WIDTH), lambda i: (i, 0)),
      ] + w_specs + [whole(w) for w in others],
      out_specs=pl.BlockSpec((tm, D_MODEL), lambda i: (i, 0)),
      scratch_shapes=[pltpu.VMEM((CONV_HALO + tm, CONV_WIDTH), F32)],
      compiler_params=pltpu.CompilerParams(
          dimension_semantics=("arbitrary",),
          vmem_limit_bytes=VMEM_LIMIT_BYTES,
      ),
      name="conv_merge_norm",
  )(x2d, att, *([w_bf16] * len(w_specs)), *others)


def _layer(x2d, w_in, conv_w, w_conv_out, w_att_out, b_gate, w_o, ln_g, ln_b, alpha):
  assert w_in.shape == (D_MODEL, GATE_OFF + 2 * D_MODEL)
  w_bf16 = w_in.astype(BF16)
  qkv_groups = _qkv_projection(x2d, w_bf16)
  att = _attention(qkv_groups)
  return _tail(x2d, att, w_bf16, conv_w, w_conv_out.astype(BF16), w_att_out.astype(BF16),
               b_gate.reshape(1, -1), w_o.astype(BF16), ln_g.reshape(1, -1), ln_b.reshape(1, -1),
               alpha)


def kernel(x, w_in, conv_w, w_conv_out, w_att_out, b_gate, w_o, ln_g, ln_b):
  batch, seq, d_model = x.shape
  depth = w_in.shape[0]
  assert d_model == D_MODEL and seq % SUPER_BLOCK == 0
  assert HEAD_DIM ** -0.5 == 0.125
  alpha = (2.0 * depth) ** 0.25
  outs = []
  for b in range(batch):
    h = x[b]
    for layer in range(depth):
      h = _layer(h, w_in[layer], conv_w[layer], w_conv_out[layer], w_att_out[layer],
                 b_gate[layer], w_o[layer], ln_g[layer], ln_b[layer], alpha)
    outs.append(h)
  return jnp.stack(outs)
```

```python
import functools

import jax
import jax.numpy as jnp
from jax import lax
from jax.experimental import pallas as pl
from jax.experimental.pallas import tpu as pltpu

D_MODEL = 1024
HEAD_DIM = 64
HEADS_PER_GROUP = 8
ATT_GROUPS = ((128, 1), (512, 4), (2048, 16))
N_GROUPS = len(ATT_GROUPS)
GROUP_WIDTH = HEADS_PER_GROUP * HEAD_DIM
ATT_QKV_WIDTH = N_GROUPS * GROUP_WIDTH
CONV_WIDTH = D_MODEL
CONV_K = 3
LN_EPS = 1e-5
LANES = 128

ATT_BLOCK = 128
assert all(w // d == ATT_BLOCK for w, d in ATT_GROUPS)
SUPER_BLOCK = ATT_BLOCK * max(d for _, d in ATT_GROUPS)
TILES_PER_SUPER = SUPER_BLOCK // ATT_BLOCK
ATT_TILES_PER_STEP = 4

QKV_ROW_TILE = 1024
QKV_DOT_ROW_CHUNKS = 2
TAIL_ROW_TILE = 1024
TAIL_ROW_CHUNKS = 4
TAIL_STAGES = 6
CONV_HALO = 8
MERGE_ROWS = 32
HEAD_PAIRS = HEADS_PER_GROUP // 2
assert 2 * HEAD_DIM == LANES
KV_WIDTH = 2 * GROUP_WIDTH
QKV_COL_OFF = (KV_WIDTH, 0, GROUP_WIDTH)

VMEM_LIMIT_BYTES = 56 * 1024 * 1024
ATT_VMEM_LIMIT_BYTES = 58 * 1024 * 1024

F32 = jnp.float32
BF16 = jnp.bfloat16

Q_OFF, K_OFF, V_OFF = 0, ATT_QKV_WIDTH, 2 * ATT_QKV_WIDTH
G_ATT_OFF = 3 * ATT_QKV_WIDTH
H_OFF = G_ATT_OFF + GROUP_WIDTH
B_OFF = H_OFF + CONV_WIDTH
C_OFF = B_OFF + CONV_WIDTH
G_CONV_OFF = C_OFF + CONV_WIDTH
GATE_OFF = G_CONV_OFF + CONV_WIDTH


def _qkv_kernel(x_ref, w_ref, *refs):
  tm = x_ref.shape[0]
  out_refs = refs[:N_GROUPS]
  xp_ref = refs[N_GROUPS:2 * N_GROUPS]
  xl_ref = refs[2 * N_GROUPS:]
  dil = [d for _, d in ATT_GROUPS]
  assert dil[0] == 1
  scale = HEAD_DIM ** -0.5
  lane_tiles = [pl.ds(c * LANES, LANES) for c in range(D_MODEL // LANES)]

  def load_rows():
    def piece(c, lanes):
      xp_ref[0][:, lanes] = x_ref[:, lanes].astype(BF16)
      xl_ref[0][c] = x_ref[:, lanes]
    return [functools.partial(piece, c, lanes) for c, lanes in enumerate(lane_tiles)]

  def permute_rows(g):
    q = dil[g + 1] // dil[g]
    assert dil[g + 1] == q * dil[g]
    n, n_next = tm // dil[g], tm // dil[g + 1]

    def piece(r, a):
      dst = pl.ds((a * dil[g] + r) * n_next, n_next)
      for c, lanes in enumerate(lane_tiles):
        rows = xl_ref[g][c, pl.ds(r * n + a, n_next, stride=q), :]
        xp_ref[g + 1][dst, lanes] = rows.astype(BF16)
        if g + 2 < N_GROUPS:
          xl_ref[g + 1][c, dst, :] = rows
    return [functools.partial(piece, r, a) for r in range(dil[g]) for a in range(q)]

  def project(g, pieces):
    n = tm // dil[g]
    row_chunk = tm // QKV_DOT_ROW_CHUNKS
    n_dots = 3 * QKV_DOT_ROW_CHUNKS
    k = 0
    for sec, off in enumerate((Q_OFF, K_OFF, V_OFF)):
      w_cols = pl.ds(off + g * GROUP_WIDTH, GROUP_WIDTH)
      for rc in range(QKV_DOT_ROW_CHUNKS):
        lo = rc * row_chunk
        res = jnp.dot(xp_ref[g][lo:lo + row_chunk, :], w_ref[:, w_cols],
                      preferred_element_type=F32)
        if sec == 0:
          res = res * scale
        res = res.astype(BF16)
        assert row_chunk % n == 0 or n % row_chunk == 0
        step = min(n, row_chunk)
        for s0 in range(0, row_chunk, step):
          r, l0 = divmod(lo + s0, n)
          out_refs[g][r, l0:l0 + step, pl.ds(QKV_COL_OFF[sec], GROUP_WIDTH)] = res[s0:s0 + step]
        k += 1
        for piece in pieces[(k - 1) * len(pieces) // n_dots:k * len(pieces) // n_dots]:
          piece()

  @pl.when(jnp.logical_and(pl.program_id(0) == 0, pl.program_id(1) == 0))
  def _():
    for piece in load_rows():
      piece()

  for g in range(N_GROUPS):
    @pl.when(pl.program_id(1) == g)
    def _(g=g):
      project(g, permute_rows(g) if g + 1 < N_GROUPS else load_rows())


def _qkv_projection(x2d, w_bf16):
  s = x2d.shape[0]
  tm = QKV_ROW_TILE

  assert (Q_OFF, K_OFF, V_OFF) == (0, ATT_QKV_WIDTH, 2 * ATT_QKV_WIDTH)
  w_spec = pl.BlockSpec((D_MODEL, 3 * ATT_QKV_WIDTH), lambda i, g: (0, 0),
                        pipeline_mode=pl.Buffered(1))

  def x_map(i, g):
    return jnp.minimum(i + jnp.minimum(g, 1), s // tm - 1), 0

  return pl.pallas_call(
      _qkv_kernel,
      out_shape=[jax.ShapeDtypeStruct((d, s // d, ATT_QKV_WIDTH), BF16) for _, d in ATT_GROUPS],
      grid=(s // tm, N_GROUPS),
      in_specs=[pl.BlockSpec((tm, D_MODEL), x_map), w_spec],
      out_specs=[pl.BlockSpec((d, tm // d, ATT_QKV_WIDTH), lambda i, g: (0, i, 0))
                 for _, d in ATT_GROUPS],
      scratch_shapes=(
          [pltpu.VMEM((tm, D_MODEL), BF16)] * N_GROUPS
          + [pltpu.VMEM((D_MODEL // LANES, tm, LANES), F32)] * (N_GROUPS - 1)),
      compiler_params=pltpu.CompilerParams(
          dimension_semantics=("arbitrary", "arbitrary"),
          vmem_limit_bytes=VMEM_LIMIT_BYTES,
      ),
      name="qkv_projection",
  )(x2d, w_bf16)


def _attention_tile(cur, prev, bias, store):
  nq = ATT_BLOCK
  lane = lax.broadcasted_iota(jnp.int32, (nq, LANES), 1)
  low_half = lane < HEAD_DIM
  ones = jnp.ones((2 * nq, LANES), BF16)

  def pair_tile(x):
    return jnp.where(low_half, x[:nq], x[nq:])

  for p in range(HEAD_PAIRS):
    q2 = cur(0, p)
    k2 = jnp.concatenate([prev(1, p), cur(1, p)], axis=0)
    v2 = jnp.concatenate([prev(2, p), cur(2, p)], axis=0)
    zero = jnp.zeros_like(q2)
    qs = jnp.concatenate([jnp.where(low_half, q2, zero), jnp.where(low_half, zero, q2)], axis=0)
    sc = lax.dot_general(qs, k2, (((1,), (1,)), ((), ())), preferred_element_type=F32)
    sc = sc + bias
    m = jnp.max(sc, axis=1, keepdims=True)
    e = jnp.exp(sc - m).astype(BF16)
    ol = jnp.dot(e, jnp.concatenate([v2, ones], axis=1), preferred_element_type=F32)
    store(p, pair_tile(ol[:, :LANES]), pair_tile(jnp.broadcast_to(m, (2 * nq, LANES))),
          pair_tile(ol[:, LANES:]))


def _step_tiles(d, sb, j):
  nt = ATT_TILES_PER_STEP
  bpr = TILES_PER_SUPER // d
  if bpr >= nt:
    assert bpr % nt == 0
    spr = bpr // nt
    b0 = (j % spr) * nt
    return sb * bpr + b0, (lambda t: j // spr), (lambda t: b0 + t)
  assert bpr == 1
  return sb, (lambda t: j * nt + t), (lambda t: 0)


def _attention_kernel(*refs):
  in_refs = refs[:2 * N_GROUPS]
  o_ref = refs[2 * N_GROUPS]
  acc_ref, max_ref, sum_ref, bias_ref = refs[2 * N_GROUPS + 1:]
  sb = pl.program_id(0)
  j = pl.program_id(1)
  nq = ATT_BLOCK
  nt = ATT_TILES_PER_STEP

  @pl.when(jnp.logical_and(sb == 0, j == 0))
  def _():
    row = lax.broadcasted_iota(jnp.int32, (2 * nq, 2 * nq), 0) % nq
    col = lax.broadcasted_iota(jnp.int32, (2 * nq, 2 * nq), 1)
    for hp in range(2):
      first_col = row if hp else jnp.maximum(row, nq)
      valid = jnp.logical_and(col >= first_col, col - nq <= row)
      bias_ref[hp] = jnp.where(valid, 0.0, -jnp.inf).astype(F32)

  def cols(sec, p):
    return pl.ds(QKV_COL_OFF[sec] + p * LANES, LANES)

  for t in range(nt):
    for g, (_, d) in enumerate(ATT_GROUPS):
      cur_ref, prev_ref = in_refs[2 * g:2 * g + 2]
      first_blk, res, loc = _step_tiles(d, sb, j)
      if cur_ref.shape[0] == 1:
        cur = lambda sec, p, t=t: cur_ref[0, t * nq:(t + 1) * nq, cols(sec, p)]
        if t == 0:
          prev = lambda sec, p: prev_ref[0, :, cols(sec, p)]
          has_prev = (first_blk > 0).astype(jnp.int32)
        else:
          prev = lambda sec, p, t=t: cur_ref[0, (t - 1) * nq:t * nq, cols(sec, p)]
          has_prev = 1
      else:
        cur = lambda sec, p, t=t: cur_ref[t, :, cols(sec, p)]
        prev = lambda sec, p, t=t: prev_ref[t, :, cols(sec, p)]
        has_prev = (first_blk > 0).astype(jnp.int32)
      r, b = res(t), loc(t)
      if d == 1:
        rows = pl.ds(pl.multiple_of(b * nq, nq), nq)
      else:
        rows = pl.ds(b * nq * d + r, nq, stride=d)

      def store(p, o, m, l, g=g, rows=rows):
        acc_ref[g, p, rows, :] = o
        max_ref[g, p, rows, :] = m
        sum_ref[g, p, rows, :] = l

      _attention_tile(cur, prev, bias_ref[has_prev], store)

  @pl.when(j == TILES_PER_SUPER // nt - 1)
  def _():
    def body(c, carry):
      rows = pl.ds(pl.multiple_of(c * MERGE_ROWS, MERGE_ROWS), MERGE_ROWS)
      for p in range(HEAD_PAIRS):
        ms = [max_ref[g, p, rows, :] for g in range(N_GROUPS)]
        top = functools.reduce(jnp.maximum, ms)
        ws = [jnp.exp(m - top) for m in ms]
        num = functools.reduce(jnp.add, [w * acc_ref[g, p, rows, :] for g, w in enumerate(ws)])
        den = functools.reduce(jnp.add, [w * sum_ref[g, p, rows, :] for g, w in enumerate(ws)])
        o_ref[rows, p * LANES:(p + 1) * LANES] = num / den
      return carry
    lax.fori_loop(0, SUPER_BLOCK // MERGE_ROWS, body, 0, unroll=2)


def _attention(qkv_groups):
  s = qkv_groups[0].shape[1]
  nt = ATT_TILES_PER_STEP
  in_specs, args = [], []
  for (_, d), arr in zip(ATT_GROUPS, qkv_groups):
    if TILES_PER_SUPER // d >= nt:
      def cur_map(sb, j, d=d):
        first_blk, res, _ = _step_tiles(d, sb, j)
        return res(0), first_blk // nt, 0

      def prev_map(sb, j, d=d):
        first_blk, res, _ = _step_tiles(d, sb, j)
        return res(0), jnp.maximum(first_blk - 1, 0), 0

      in_specs += [pl.BlockSpec((1, nt * ATT_BLOCK, ATT_QKV_WIDTH), cur_map),
                   pl.BlockSpec((1, ATT_BLOCK, KV_WIDTH), prev_map)]
    else:
      def cur_map(sb, j, d=d):
        first_blk, res, _ = _step_tiles(d, sb, j)
        return res(0) // nt, first_blk, 0

      def prev_map(sb, j, d=d):
        first_blk, res, _ = _step_tiles(d, sb, j)
        return res(0) // nt, jnp.maximum(first_blk - 1, 0), 0

      in_specs += [pl.BlockSpec((nt, ATT_BLOCK, ATT_QKV_WIDTH), cur_map),
                   pl.BlockSpec((nt, ATT_BLOCK, KV_WIDTH), prev_map)]
    args += [arr] * 2
  scratch = pltpu.VMEM((N_GROUPS, HEAD_PAIRS, SUPER_BLOCK, LANES), F32)
  return pl.pallas_call(
      _attention_kernel,
      out_shape=jax.ShapeDtypeStruct((s, GROUP_WIDTH), F32),
      grid=(s // SUPER_BLOCK, TILES_PER_SUPER // nt),
      in_specs=in_specs,
      out_specs=pl.BlockSpec((SUPER_BLOCK, GROUP_WIDTH), lambda sb, j: (sb, 0)),
      scratch_shapes=[scratch] * 3 + [pltpu.VMEM((2, 2 * ATT_BLOCK, 2 * ATT_BLOCK), F32)],
      compiler_params=pltpu.CompilerParams(
          dimension_semantics=("arbitrary", "arbitrary"),
          vmem_limit_bytes=ATT_VMEM_LIMIT_BYTES,
      ),
      name="dilated_attention",
  )(*args)


def _silu(x):
  return x * jax.nn.sigmoid(x)


def _tail_kernel(alpha, x_ref, att_ref, wga_ref, wh_ref, wb_ref, wc_ref, wgc_ref, wg1_ref, wg2_ref,
                 cw_ref, wco_ref, wao_ref, bg_ref, wo_ref, lg_ref, lb_ref, o_ref, u_ref):
  tm = x_ref.shape[0]
  chunk = tm // TAIL_ROW_CHUNKS

  def dot(a, w_ref):
    return jnp.dot(a, w_ref[...], preferred_element_type=F32)

  @pl.when(pl.program_id(0) == 0)
  def _():
    u_ref[0:CONV_HALO, :] = jnp.zeros((CONV_HALO, CONV_WIDTH), F32)

  def row_chunk(lo):
    rows = pl.ds(lo, chunk)
    xb = x_ref[rows, :].astype(BF16)
    u_ref[pl.ds(CONV_HALO + lo, chunk), :] = dot(xb, wc_ref) * dot(xb, wh_ref)
    yield
    p_b, p_gc = dot(xb, wb_ref), dot(xb, wgc_ref)
    conv = cw_ref[CONV_K - 1:CONV_K, :] * u_ref[pl.ds(CONV_HALO + lo, chunk), :]
    for k in range(CONV_K - 1):
      off = CONV_HALO + lo - (CONV_K - 1 - k)
      conv = conv + cw_ref[k:k + 1, :] * u_ref[pl.ds(off, chunk), :]
    a_conv = ((p_b * conv) * _silu(p_gc)).astype(BF16)
    yield
    y_conv = dot(a_conv, wco_ref)
    a_att = (att_ref[rows, :] * _silu(dot(xb, wga_ref))).astype(BF16)
    yield
    y_att = dot(a_att, wao_ref)
    m_c = jax.nn.sigmoid(dot(xb, wg1_ref) + bg_ref[:, :D_MODEL]) * y_conv
    yield
    g_a = jax.nn.sigmoid(dot(xb, wg2_ref) + bg_ref[:, D_MODEL:])
    merged = (m_c + g_a * y_att).astype(BF16)
    yield
    y = alpha * x_ref[rows, :] + dot(merged, wo_ref)
    mu = jnp.mean(y, axis=-1, keepdims=True)
    yc = y - mu
    var = jnp.mean(yc * yc, axis=-1, keepdims=True)
    o_ref[rows, :] = yc * lax.rsqrt(var + LN_EPS) * lg_ref[...] + lb_ref[...]
    yield

  streams = [row_chunk(k * chunk) for k in range(TAIL_ROW_CHUNKS)]
  for stage in range(TAIL_STAGES):
    for stream in streams:
      next(stream)
    if stage == 1:
      u_ref[0:CONV_HALO, :] = u_ref[tm:tm + CONV_HALO, :]


def _tail(x2d, att, w_bf16, conv_w, w_conv_out, w_att_out, b_gate, w_o, ln_g, ln_b, alpha):
  s = x2d.shape[0]
  tm = TAIL_ROW_TILE

  def whole(arr):
    return pl.BlockSpec(arr.shape, lambda i: (0,) * arr.ndim, pipeline_mode=pl.Buffered(1))

  def w_cols(off, width):
    assert off % width == 0
    return pl.BlockSpec((D_MODEL, width), lambda i: (0, off // width), pipeline_mode=pl.Buffered(1))

  w_specs = [w_cols(G_ATT_OFF, GROUP_WIDTH)] + [
      w_cols(off, CONV_WIDTH)
      for off in (H_OFF, B_OFF, C_OFF, G_CONV_OFF, GATE_OFF, GATE_OFF + D_MODEL)]
  others = (conv_w, w_conv_out, w_att_out, b_gate, w_o, ln_g, ln_b)
  return pl.pallas_call(
      functools.partial(_tail_kernel, alpha),
      out_shape=jax.ShapeDtypeStruct((s, D_MODEL), F32),
      grid=(s // tm,),
      in_specs=[
          pl.BlockSpec((tm, D_MODEL), lambda i: (i, 0)),
          pl.BlockSpec((tm, GROUP_WIDTH), lambda i: (i, 0)),
      ] + w_specs + [whole(w) for w in others],
      out_specs=pl.BlockSpec((tm, D_MODEL), lambda i: (i, 0)),
      scratch_shapes=[pltpu.VMEM((CONV_HALO + tm, CONV_WIDTH), F32)],
      compiler_params=pltpu.CompilerParams(
          dimension_semantics=("arbitrary",),
          vmem_limit_bytes=VMEM_LIMIT_BYTES,
      ),
      name="conv_merge_norm",
  )(x2d, att, *([w_bf16] * len(w_specs)), *others)


def _layer(x2d, w_in, conv_w, w_conv_out, w_att_out, b_gate, w_o, ln_g, ln_b, alpha):
  assert w_in.shape == (D_MODEL, GATE_OFF + 2 * D_MODEL)
  w_bf16 = w_in.astype(BF16)
  qkv_groups = _qkv_projection(x2d, w_bf16)
  att = _attention(qkv_groups)
  return _tail(x2d, att, w_bf16, conv_w, w_conv_out.astype(BF16), w_att_out.astype(BF16),
               b_gate.reshape(1, -1), w_o.astype(BF16), ln_g.reshape(1, -1), ln_b.reshape(1, -1),
               alpha)


def kernel(x, w_in, conv_w, w_conv_out, w_att_out, b_gate, w_o, ln_g, ln_b):
  batch, seq, d_model = x.shape
  depth = w_in.shape[0]
  assert d_model == D_MODEL and seq % SUPER_BLOCK == 0
  assert HEAD_DIM ** -0.5 == 0.125
  alpha = (2.0 * depth) ** 0.25
  outs = []
  for b in range(batch):
    h = x[b]
    for layer in range(depth):
      h = _layer(h, w_in[layer], conv_w[layer], w_conv_out[layer], w_att_out[layer],
                 b_gate[layer], w_o[layer], ln_g[layer], ln_b[layer], alpha)
    outs.append(h)
  return jnp.stack(outs)
```

```python
import functools

import jax
import jax.numpy as jnp
from jax import lax
from jax.experimental import pallas as pl
from jax.experimental.pallas import tpu as pltpu

D_MODEL = 1024
HEAD_DIM = 64
HEADS_PER_GROUP = 8
ATT_GROUPS = ((128, 1), (512, 4), (2048, 16))
N_GROUPS = len(ATT_GROUPS)
GROUP_WIDTH = HEADS_PER_GROUP * HEAD_DIM
ATT_QKV_WIDTH = N_GROUPS * GROUP_WIDTH
CONV_WIDTH = D_MODEL
CONV_K = 3
LN_EPS = 1e-5
LANES = 128

ATT_BLOCK = 128
assert all(w // d == ATT_BLOCK for w, d in ATT_GROUPS)
SUPER_BLOCK = ATT_BLOCK * max(d for _, d in ATT_GROUPS)
TILES_PER_SUPER = SUPER_BLOCK // ATT_BLOCK
ATT_TILES_PER_STEP = 4

QKV_ROW_TILE = 1024
QKV_DOT_ROW_CHUNKS = 2
TAIL_ROW_TILE = 1024
TAIL_ROW_CHUNKS = 4
TAIL_STAGES = 6
CONV_HALO = 8
MERGE_ROWS = 32
HEAD_PAIRS = HEADS_PER_GROUP // 2
assert 2 * HEAD_DIM == LANES
KV_WIDTH = 2 * GROUP_WIDTH
QKV_COL_OFF = (KV_WIDTH, 0, GROUP_WIDTH)
SCORE_SCALE = HEAD_DIM ** -0.5 * 1.4426950408889634

VMEM_LIMIT_BYTES = 56 * 1024 * 1024
ATT_VMEM_LIMIT_BYTES = 58 * 1024 * 1024

F32 = jnp.float32
BF16 = jnp.bfloat16

Q_OFF, K_OFF, V_OFF = 0, ATT_QKV_WIDTH, 2 * ATT_QKV_WIDTH
G_ATT_OFF = 3 * ATT_QKV_WIDTH
H_OFF = G_ATT_OFF + GROUP_WIDTH
B_OFF = H_OFF + CONV_WIDTH
C_OFF = B_OFF + CONV_WIDTH
G_CONV_OFF = C_OFF + CONV_WIDTH
GATE_OFF = G_CONV_OFF + CONV_WIDTH


def _qkv_kernel(x_ref, w_ref, *refs):
  tm = x_ref.shape[0]
  out_refs = refs[:N_GROUPS]
  xp_ref = refs[N_GROUPS:2 * N_GROUPS]
  xl_ref = refs[2 * N_GROUPS:]
  dil = [d for _, d in ATT_GROUPS]
  assert dil[0] == 1
  scale = SCORE_SCALE
  lane_tiles = [pl.ds(c * LANES, LANES) for c in range(D_MODEL // LANES)]

  def load_rows():
    def piece(c, lanes):
      xp_ref[0][:, lanes] = x_ref[:, lanes].astype(BF16)
      xl_ref[0][c] = x_ref[:, lanes]
    return [functools.partial(piece, c, lanes) for c, lanes in enumerate(lane_tiles)]

  def permute_rows(g):
    q = dil[g + 1] // dil[g]
    assert dil[g + 1] == q * dil[g]
    n, n_next = tm // dil[g], tm // dil[g + 1]

    def piece(r, a):
      dst = pl.ds((a * dil[g] + r) * n_next, n_next)
      for c, lanes in enumerate(lane_tiles):
        rows = xl_ref[g][c, pl.ds(r * n + a, n_next, stride=q), :]
        xp_ref[g + 1][dst, lanes] = rows.astype(BF16)
        if g + 2 < N_GROUPS:
          xl_ref[g + 1][c, dst, :] = rows
    return [functools.partial(piece, r, a) for r in range(dil[g]) for a in range(q)]

  def project(g, pieces):
    n = tm // dil[g]
    row_chunk = tm // QKV_DOT_ROW_CHUNKS
    n_dots = 3 * QKV_DOT_ROW_CHUNKS
    k = 0
    for sec, off in enumerate((Q_OFF, K_OFF, V_OFF)):
      w_cols = pl.ds(off + g * GROUP_WIDTH, GROUP_WIDTH)
      for rc in range(QKV_DOT_ROW_CHUNKS):
        lo = rc * row_chunk
        res = jnp.dot(xp_ref[g][lo:lo + row_chunk, :], w_ref[:, w_cols],
                      preferred_element_type=F32)
        if sec == 0:
          res = res * scale
        res = res.astype(BF16)
        assert row_chunk % n == 0 or n % row_chunk == 0
        step = min(n, row_chunk)
        for s0 in range(0, row_chunk, step):
          r, l0 = divmod(lo + s0, n)
          out_refs[g][r, l0:l0 + step, pl.ds(QKV_COL_OFF[sec], GROUP_WIDTH)] = res[s0:s0 + step]
        k += 1
        for piece in pieces[(k - 1) * len(pieces) // n_dots:k * len(pieces) // n_dots]:
          piece()

  @pl.when(jnp.logical_and(pl.program_id(0) == 0, pl.program_id(1) == 0))
  def _():
    for piece in load_rows():
      piece()

  for g in range(N_GROUPS):
    @pl.when(pl.program_id(1) == g)
    def _(g=g):
      project(g, permute_rows(g) if g + 1 < N_GROUPS else load_rows())


def _qkv_projection(x2d, w_bf16):
  s = x2d.shape[0]
  tm = QKV_ROW_TILE

  assert (Q_OFF, K_OFF, V_OFF) == (0, ATT_QKV_WIDTH, 2 * ATT_QKV_WIDTH)
  w_spec = pl.BlockSpec((D_MODEL, 3 * ATT_QKV_WIDTH), lambda i, g: (0, 0),
                        pipeline_mode=pl.Buffered(1))

  def x_map(i, g):
    return jnp.minimum(i + g // (N_GROUPS - 1), s // tm - 1), 0

  return pl.pallas_call(
      _qkv_kernel,
      out_shape=[jax.ShapeDtypeStruct((d, s // d, ATT_QKV_WIDTH), BF16) for _, d in ATT_GROUPS],
      grid=(s // tm, N_GROUPS),
      in_specs=[pl.BlockSpec((tm, D_MODEL), x_map), w_spec],
      out_specs=[pl.BlockSpec((d, tm // d, ATT_QKV_WIDTH), lambda i, g: (0, i, 0))
                 for _, d in ATT_GROUPS],
      scratch_shapes=(
          [pltpu.VMEM((tm, D_MODEL), BF16)] * N_GROUPS
          + [pltpu.VMEM((D_MODEL // LANES, tm, LANES), F32)] * (N_GROUPS - 1)),
      compiler_params=pltpu.CompilerParams(
          dimension_semantics=("arbitrary", "arbitrary"),
          vmem_limit_bytes=VMEM_LIMIT_BYTES,
      ),
      name="qkv_projection",
  )(x2d, w_bf16)


def _attention_tile(cur, prev, bias, store):
  nq = ATT_BLOCK
  lane = lax.broadcasted_iota(jnp.int32, (nq, LANES), 1)
  low_half = lane < HEAD_DIM
  ones = jnp.ones((2 * nq, LANES), BF16)

  def pair_tile(x):
    return jnp.where(low_half, x[:nq], x[nq:])

  for p in range(HEAD_PAIRS):
    q2 = cur(0, p)
    k2 = jnp.concatenate([prev(1, p), cur(1, p)], axis=0)
    v2 = jnp.concatenate([prev(2, p), cur(2, p)], axis=0)
    zero = jnp.zeros_like(q2)
    qs = jnp.concatenate([jnp.where(low_half, q2, zero), jnp.where(low_half, zero, q2)], axis=0)
    sc = lax.dot_general(qs, k2, (((1,), (1,)), ((), ())), preferred_element_type=F32)
    sc = sc + bias
    m = jnp.max(sc, axis=1, keepdims=True)
    e = jnp.exp2(sc - m).astype(BF16)
    ol = jnp.dot(e, jnp.concatenate([v2, ones], axis=1), preferred_element_type=F32)
    store(p, pair_tile(ol[:, :LANES]), pair_tile(jnp.broadcast_to(m, (2 * nq, LANES))),
          pair_tile(ol[:, LANES:]))


def _step_tiles(d, sb, j):
  nt = ATT_TILES_PER_STEP
  bpr = TILES_PER_SUPER // d
  if bpr >= nt:
    assert bpr % nt == 0
    spr = bpr // nt
    b0 = (j % spr) * nt
    return sb * bpr + b0, (lambda t: j // spr), (lambda t: b0 + t)
  assert bpr == 1
  return sb, (lambda t: j * nt + t), (lambda t: 0)


def _attention_kernel(*refs):
  in_refs = refs[:2 * N_GROUPS]
  o_ref = refs[2 * N_GROUPS]
  acc_ref, max_ref, sum_ref, bias_ref = refs[2 * N_GROUPS + 1:]
  sb = pl.program_id(0)
  j = pl.program_id(1)
  nq = ATT_BLOCK
  nt = ATT_TILES_PER_STEP

  @pl.when(jnp.logical_and(sb == 0, j == 0))
  def _():
    row = lax.broadcasted_iota(jnp.int32, (2 * nq, 2 * nq), 0) % nq
    col = lax.broadcasted_iota(jnp.int32, (2 * nq, 2 * nq), 1)
    for hp in range(2):
      first_col = row if hp else jnp.maximum(row, nq)
      valid = jnp.logical_and(col >= first_col, col - nq <= row)
      bias_ref[hp] = jnp.where(valid, 0.0, -jnp.inf).astype(F32)

  def cols(sec, p):
    return pl.ds(QKV_COL_OFF[sec] + p * LANES, LANES)

  for t in range(nt):
    for g, (_, d) in enumerate(ATT_GROUPS):
      cur_ref, prev_ref = in_refs[2 * g:2 * g + 2]
      first_blk, res, loc = _step_tiles(d, sb, j)
      if cur_ref.shape[0] == 1:
        cur = lambda sec, p, t=t: cur_ref[0, t * nq:(t + 1) * nq, cols(sec, p)]
        if t == 0:
          prev = lambda sec, p: prev_ref[0, :, cols(sec, p)]
          has_prev = (first_blk > 0).astype(jnp.int32)
        else:
          prev = lambda sec, p, t=t: cur_ref[0, (t - 1) * nq:t * nq, cols(sec, p)]
          has_prev = 1
      else:
        cur = lambda sec, p, t=t: cur_ref[t, :, cols(sec, p)]
        prev = lambda sec, p, t=t: prev_ref[t, :, cols(sec, p)]
        has_prev = (first_blk > 0).astype(jnp.int32)
      r, b = res(t), loc(t)
      if d == 1:
        rows = pl.ds(pl.multiple_of(b * nq, nq), nq)
      else:
        rows = pl.ds(b * nq * d + r, nq, stride=d)

      def store(p, o, m, l, g=g, rows=rows):
        acc_ref[g, p, rows, :] = o
        max_ref[g, p, rows, :] = m
        sum_ref[g, p, rows, :] = l

      _attention_tile(cur, prev, bias_ref[has_prev], store)

  @pl.when(j == TILES_PER_SUPER // nt - 1)
  def _():
    def body(c, carry):
      rows = pl.ds(pl.multiple_of(c * MERGE_ROWS, MERGE_ROWS), MERGE_ROWS)
      for p in range(HEAD_PAIRS):
        ms = [max_ref[g, p, rows, :] for g in range(N_GROUPS)]
        top = functools.reduce(jnp.maximum, ms)
        ws = [jnp.exp2(m - top) for m in ms]
        num = functools.reduce(jnp.add, [w * acc_ref[g, p, rows, :] for g, w in enumerate(ws)])
        den = functools.reduce(jnp.add, [w * sum_ref[g, p, rows, :] for g, w in enumerate(ws)])
        o_ref[rows, p * LANES:(p + 1) * LANES] = num / den
      return carry
    lax.fori_loop(0, SUPER_BLOCK // MERGE_ROWS, body, 0, unroll=2)


def _attention(qkv_groups):
  s = qkv_groups[0].shape[1]
  nt = ATT_TILES_PER_STEP
  in_specs, args = [], []
  for (_, d), arr in zip(ATT_GROUPS, qkv_groups):
    if TILES_PER_SUPER // d >= nt:
      def cur_map(sb, j, d=d):
        first_blk, res, _ = _step_tiles(d, sb, j)
        return res(0), first_blk // nt, 0

      def prev_map(sb, j, d=d):
        first_blk, res, _ = _step_tiles(d, sb, j)
        return res(0), jnp.maximum(first_blk - 1, 0), 0

      in_specs += [pl.BlockSpec((1, nt * ATT_BLOCK, ATT_QKV_WIDTH), cur_map),
                   pl.BlockSpec((1, ATT_BLOCK, KV_WIDTH), prev_map)]
    else:
      def cur_map(sb, j, d=d):
        first_blk, res, _ = _step_tiles(d, sb, j)
        return res(0) // nt, first_blk, 0

      def prev_map(sb, j, d=d):
        first_blk, res, _ = _step_tiles(d, sb, j)
        return res(0) // nt, jnp.maximum(first_blk - 1, 0), 0

      in_specs += [pl.BlockSpec((nt, ATT_BLOCK, ATT_QKV_WIDTH), cur_map),
                   pl.BlockSpec((nt, ATT_BLOCK, KV_WIDTH), prev_map)]
    args += [arr] * 2
  scratch = pltpu.VMEM((N_GROUPS, HEAD_PAIRS, SUPER_BLOCK, LANES), F32)
  return pl.pallas_call(
      _attention_kernel,
      out_shape=jax.ShapeDtypeStruct((s, GROUP_WIDTH), F32),
      grid=(s // SUPER_BLOCK, TILES_PER_SUPER // nt),
      in_specs=in_specs,
      out_specs=pl.BlockSpec((SUPER_BLOCK, GROUP_WIDTH), lambda sb, j: (sb, 0)),
      scratch_shapes=[scratch] * 3 + [pltpu.VMEM((2, 2 * ATT_BLOCK, 2 * ATT_BLOCK), F32)],
      compiler_params=pltpu.CompilerParams(
          dimension_semantics=("arbitrary", "arbitrary"),
          vmem_limit_bytes=ATT_VMEM_LIMIT_BYTES,
      ),
      name="dilated_attention",
  )(*args)


def _silu(x):
  return x * jax.nn.sigmoid(x)


def _tail_kernel(alpha, x_ref, att_ref, wga_ref, wh_ref, wb_ref, wc_ref, wgc_ref, wg1_ref, wg2_ref,
                 cw_ref, wco_ref, wao_ref, bg_ref, wo_ref, lg_ref, lb_ref, o_ref, u_ref):
  tm = x_ref.shape[0]
  chunk = tm // TAIL_ROW_CHUNKS

  def dot(a, w_ref):
    return jnp.dot(a, w_ref[...], preferred_element_type=F32)

  @pl.when(pl.program_id(0) == 0)
  def _():
    u_ref[0:CONV_HALO, :] = jnp.zeros((CONV_HALO, CONV_WIDTH), F32)

  def row_chunk(lo):
    rows = pl.ds(lo, chunk)
    xb = x_ref[rows, :].astype(BF16)
    u_ref[pl.ds(CONV_HALO + lo, chunk), :] = dot(xb, wc_ref) * dot(xb, wh_ref)
    yield
    p_b, p_gc = dot(xb, wb_ref), dot(xb, wgc_ref)
    conv = cw_ref[CONV_K - 1:CONV_K, :] * u_ref[pl.ds(CONV_HALO + lo, chunk), :]
    for k in range(CONV_K - 1):
      off = CONV_HALO + lo - (CONV_K - 1 - k)
      conv = conv + cw_ref[k:k + 1, :] * u_ref[pl.ds(off, chunk), :]
    a_conv = ((p_b * conv) * _silu(p_gc)).astype(BF16)
    yield
    y_conv = dot(a_conv, wco_ref)
    a_att = (att_ref[rows, :] * _silu(dot(xb, wga_ref))).astype(BF16)
    yield
    y_att = dot(a_att, wao_ref)
    m_c = jax.nn.sigmoid(dot(xb, wg1_ref) + bg_ref[:, :D_MODEL]) * y_conv
    yield
    g_a = jax.nn.sigmoid(dot(xb, wg2_ref) + bg_ref[:, D_MODEL:])
    merged = (m_c + g_a * y_att).astype(BF16)
    yield
    y = alpha * x_ref[rows, :] + dot(merged, wo_ref)
    mu = jnp.mean(y, axis=-1, keepdims=True)
    yc = y - mu
    var = jnp.mean(yc * yc, axis=-1, keepdims=True)
    o_ref[rows, :] = yc * lax.rsqrt(var + LN_EPS) * lg_ref[...] + lb_ref[...]
    yield

  streams = [row_chunk(k * chunk) for k in range(TAIL_ROW_CHUNKS)]
  for stage in range(TAIL_STAGES):
    for stream in streams:
      next(stream)
    if stage == 1:
      u_ref[0:CONV_HALO, :] = u_ref[tm:tm + CONV_HALO, :]


def _tail(x2d, att, w_bf16, conv_w, w_conv_out, w_att_out, b_gate, w_o, ln_g, ln_b, alpha):
  s = x2d.shape[0]
  tm = TAIL_ROW_TILE

  def whole(arr):
    return pl.BlockSpec(arr.shape, lambda i: (0,) * arr.ndim, pipeline_mode=pl.Buffered(1))

  def w_cols(off, width):
    assert off % width == 0
    return pl.BlockSpec((D_MODEL, width), lambda i: (0, off // width), pipeline_mode=pl.Buffered(1))

  w_specs = [w_cols(G_ATT_OFF, GROUP_WIDTH)] + [
      w_cols(off, CONV_WIDTH)
      for off in (H_OFF, B_OFF, C_OFF, G_CONV_OFF, GATE_OFF, GATE_OFF + D_MODEL)]
  others = (conv_w, w_conv_out, w_att_out, b_gate, w_o, ln_g, ln_b)
  return pl.pallas_call(
      functools.partial(_tail_kernel, alpha),
      out_shape=jax.ShapeDtypeStruct((s, D_MODEL), F32),
      grid=(s // tm,),
      in_specs=[
          pl.BlockSpec((tm, D_MODEL), lambda i: (i, 0)),
          pl.BlockSpec((tm, GROUP_WIDTH), lambda i: (i, 0)),
      ] + w_specs + [whole(w) for w in others],
      out_specs=pl.BlockSpec((tm, D_MODEL), lambda i: (i, 0)),
      scratch_shapes=[pltpu.VMEM((CONV_HALO + tm, CONV_WIDTH), F32)],
      compiler_params=pltpu.CompilerParams(
          dimension_semantics=("arbitrary",),
          vmem_limit_bytes=VMEM_LIMIT_BYTES,
      ),
      name="conv_merge_norm",
  )(x2d, att, *([w_bf16] * len(w_specs)), *others)


def _layer(x2d, w_in, conv_w, w_conv_out, w_att_out, b_gate, w_o, ln_g, ln_b, alpha):
  assert w_in.shape == (D_MODEL, GATE_OFF + 2 * D_MODEL)
  w_bf16 = w_in.astype(BF16)
  qkv_groups = _qkv_projection(x2d, w_bf16)
  att = _attention(qkv_groups)
  return _tail(x2d, att, w_bf16, conv_w, w_conv_out.astype(BF16), w_att_out.astype(BF16),
               b_gate.reshape(1, -1), w_o.astype(BF16), ln_g.reshape(1, -1), ln_b.reshape(1, -1),
               alpha)


def kernel(x, w_in, conv_w, w_conv_out, w_att_out, b_gate, w_o, ln_g, ln_b):
  batch, seq, d_model = x.shape
  depth = w_in.shape[0]
  assert d_model == D_MODEL and seq % SUPER_BLOCK == 0
  alpha = (2.0 * depth) ** 0.25
  outs = []
  for b in range(batch):
    h = x[b]
    for layer in range(depth):
      h = _layer(h, w_in[layer], conv_w[layer], w_conv_out[layer], w_att_out[layer],
                 b_gate[layer], w_o[layer], ln_g[layer], ln_b[layer], alpha)
    outs.append(h)
  return jnp.stack(outs)
```

```python
import functools

import jax
import jax.numpy as jnp
from jax import lax
from jax.experimental import pallas as pl
from jax.experimental.pallas import tpu as pltpu

D_MODEL = 1024
HEAD_DIM = 64
HEADS_PER_GROUP = 8
ATT_GROUPS = ((128, 1), (512, 4), (2048, 16))
N_GROUPS = len(ATT_GROUPS)
GROUP_WIDTH = HEADS_PER_GROUP * HEAD_DIM
ATT_QKV_WIDTH = N_GROUPS * GROUP_WIDTH
CONV_WIDTH = D_MODEL
CONV_K = 3
LN_EPS = 1e-5
LANES = 128

ATT_BLOCK = 128
assert all(w // d == ATT_BLOCK for w, d in ATT_GROUPS)
SUPER_BLOCK = ATT_BLOCK * max(d for _, d in ATT_GROUPS)
TILES_PER_SUPER = SUPER_BLOCK // ATT_BLOCK
ATT_TILES_PER_STEP = 8

QKV_ROW_TILE = 1024
QKV_DOT_ROW_CHUNKS = 2
TAIL_ROW_TILE = 512
TAIL_ROW_CHUNKS = 2
TAIL_STREAMS = 2
assert TAIL_ROW_CHUNKS % TAIL_STREAMS == 0
TAIL_STAGES = 6
CONV_HALO = 8
MERGE_ROWS = 32
HEAD_PAIRS = HEADS_PER_GROUP // 2
assert 2 * HEAD_DIM == LANES
KV_WIDTH = 2 * GROUP_WIDTH
QKV_COL_OFF = (KV_WIDTH, 0, GROUP_WIDTH)
SCORE_SCALE = HEAD_DIM ** -0.5 * 1.4426950408889634

HEAD_LANES = LANES // HEADS_PER_GROUP
VMEM_LIMIT_BYTES = 56 * 1024 * 1024

F32 = jnp.float32
BF16 = jnp.bfloat16

Q_OFF, K_OFF, V_OFF = 0, ATT_QKV_WIDTH, 2 * ATT_QKV_WIDTH
G_ATT_OFF = 3 * ATT_QKV_WIDTH
H_OFF = G_ATT_OFF + GROUP_WIDTH
B_OFF = H_OFF + CONV_WIDTH
C_OFF = B_OFF + CONV_WIDTH
G_CONV_OFF = C_OFF + CONV_WIDTH
GATE_OFF = G_CONV_OFF + CONV_WIDTH


def _qkv_kernel(x_ref, w_ref, *refs):
  tm = x_ref.shape[0]
  out_refs = refs[:N_GROUPS]
  xp_ref = refs[N_GROUPS:2 * N_GROUPS]
  xl_ref = refs[2 * N_GROUPS:]
  dil = [d for _, d in ATT_GROUPS]
  assert dil[0] == 1
  scale = SCORE_SCALE
  lane_tiles = [pl.ds(c * LANES, LANES) for c in range(D_MODEL // LANES)]

  def load_rows():
    def piece(c, lanes):
      xp_ref[0][:, lanes] = x_ref[:, lanes].astype(BF16)
      xl_ref[0][c] = x_ref[:, lanes]
    return [functools.partial(piece, c, lanes) for c, lanes in enumerate(lane_tiles)]

  def permute_rows(g):
    q = dil[g + 1] // dil[g]
    assert dil[g + 1] == q * dil[g]
    n, n_next = tm // dil[g], tm // dil[g + 1]

    def piece(r, a):
      dst = pl.ds((a * dil[g] + r) * n_next, n_next)
      for c, lanes in enumerate(lane_tiles):
        rows = xl_ref[g][c, pl.ds(r * n + a, n_next, stride=q), :]
        xp_ref[g + 1][dst, lanes] = rows.astype(BF16)
        if g + 2 < N_GROUPS:
          xl_ref[g + 1][c, dst, :] = rows
    return [functools.partial(piece, r, a) for r in range(dil[g]) for a in range(q)]

  def project(g, pieces):
    n = tm // dil[g]
    row_chunk = tm // QKV_DOT_ROW_CHUNKS
    n_dots = 3 * QKV_DOT_ROW_CHUNKS
    k = 0
    for sec, off in enumerate((Q_OFF, K_OFF, V_OFF)):
      w_cols = pl.ds(off + g * GROUP_WIDTH, GROUP_WIDTH)
      for rc in range(QKV_DOT_ROW_CHUNKS):
        lo = rc * row_chunk
        res = jnp.dot(xp_ref[g][lo:lo + row_chunk, :], w_ref[:, w_cols],
                      preferred_element_type=F32)
        if sec == 0:
          res = res * scale
        res = res.astype(BF16)
        assert row_chunk % n == 0 or n % row_chunk == 0
        step = min(n, row_chunk)
        for s0 in range(0, row_chunk, step):
          r, l0 = divmod(lo + s0, n)
          out_refs[g][r, l0:l0 + step, pl.ds(QKV_COL_OFF[sec], GROUP_WIDTH)] = res[s0:s0 + step]
        k += 1
        for piece in pieces[(k - 1) * len(pieces) // n_dots:k * len(pieces) // n_dots]:
          piece()

  @pl.when(jnp.logical_and(pl.program_id(0) == 0, pl.program_id(1) == 0))
  def _():
    for piece in load_rows():
      piece()

  for g in range(N_GROUPS):
    @pl.when(pl.program_id(1) == g)
    def _(g=g):
      project(g, permute_rows(g) if g + 1 < N_GROUPS else load_rows())


def _qkv_projection(x2d, w_bf16):
  s = x2d.shape[0]
  tm = QKV_ROW_TILE

  assert (Q_OFF, K_OFF, V_OFF) == (0, ATT_QKV_WIDTH, 2 * ATT_QKV_WIDTH)
  w_spec = pl.BlockSpec((D_MODEL, 3 * ATT_QKV_WIDTH), lambda i, g: (0, 0),
                        pipeline_mode=pl.Buffered(1))

  def x_map(i, g):
    return jnp.minimum(i + g // (N_GROUPS - 1), s // tm - 1), 0

  return pl.pallas_call(
      _qkv_kernel,
      out_shape=[jax.ShapeDtypeStruct((d, s // d, ATT_QKV_WIDTH), BF16) for _, d in ATT_GROUPS],
      grid=(s // tm, N_GROUPS),
      in_specs=[pl.BlockSpec((tm, D_MODEL), x_map), w_spec],
      out_specs=[pl.BlockSpec((d, tm // d, ATT_QKV_WIDTH), lambda i, g: (0, i, 0))
                 for _, d in ATT_GROUPS],
      scratch_shapes=(
          [pltpu.VMEM((tm, D_MODEL), BF16)] * N_GROUPS
          + [pltpu.VMEM((D_MODEL // LANES, tm, LANES), F32)] * (N_GROUPS - 1)),
      compiler_params=pltpu.CompilerParams(
          dimension_semantics=("arbitrary", "arbitrary"),
          vmem_limit_bytes=VMEM_LIMIT_BYTES,
      ),
      name="qkv_projection",
  )(x2d, w_bf16)


def _attention_tile(cur, prev, bias, store_o):
  nq = ATT_BLOCK
  lane = lax.broadcasted_iota(jnp.int32, (nq, LANES), 1)
  low_half = lane < HEAD_DIM
  head_of_lane = lane // HEAD_LANES
  ones = jnp.ones((2 * nq, LANES), BF16)
  m_tile = l_tile = None

  for p in range(HEAD_PAIRS):
    q2 = cur(0, p)
    k2 = jnp.concatenate([prev(1, p), cur(1, p)], axis=0)
    v2 = jnp.concatenate([prev(2, p), cur(2, p)], axis=0)
    zero = jnp.zeros_like(q2)
    qs = jnp.concatenate([jnp.where(low_half, q2, zero), jnp.where(low_half, zero, q2)], axis=0)
    sc = lax.dot_general(qs, k2, (((1,), (1,)), ((), ())), preferred_element_type=F32)
    sc = sc + bias
    m = jnp.max(sc, axis=1, keepdims=True)
    e = jnp.exp2(sc - m).astype(BF16)
    ol = jnp.dot(e, jnp.concatenate([v2, ones], axis=1), preferred_element_type=F32)
    store_o(p, jnp.where(low_half, ol[:nq, :LANES], ol[nq:, :LANES]).astype(BF16))
    for k, rows in enumerate((slice(0, nq), slice(nq, 2 * nq))):
      m_h = jnp.broadcast_to(m[rows], (nq, LANES))
      l_h = ol[rows, LANES:]
      if m_tile is None:
        m_tile, l_tile = m_h, l_h
      else:
        own = head_of_lane == 2 * p + k
        m_tile, l_tile = jnp.where(own, m_h, m_tile), jnp.where(own, l_h, l_tile)
  return m_tile, l_tile


def _step_shape(d):
  nt = ATT_TILES_PER_STEP
  bpr = TILES_PER_SUPER // d
  blocks = min(nt, bpr)
  assert nt % blocks == 0 and bpr % blocks == 0
  return nt // blocks, blocks


def _step_tiles(d, sb, j):
  n_res, blocks = _step_shape(d)
  bpr = TILES_PER_SUPER // d
  spr = bpr // blocks
  b0 = (j % spr) * blocks
  res_group = j // spr
  return (res_group, sb * bpr + b0,
          (lambda t: res_group * n_res + t // blocks), (lambda t: b0 + t % blocks))


def _attention_kernel(*refs):
  in_refs = refs[:2 * N_GROUPS]
  o_refs = refs[2 * N_GROUPS:3 * N_GROUPS]
  coef_refs = refs[3 * N_GROUPS:4 * N_GROUPS]
  max_ref, sum_ref, bias_ref = refs[4 * N_GROUPS:]
  sb = pl.program_id(0)
  j = pl.program_id(1)
  nq = ATT_BLOCK
  nt = ATT_TILES_PER_STEP

  @pl.when(jnp.logical_and(sb == 0, j == 0))
  def _():
    row = lax.broadcasted_iota(jnp.int32, (2 * nq, 2 * nq), 0) % nq
    col = lax.broadcasted_iota(jnp.int32, (2 * nq, 2 * nq), 1)
    for hp in range(2):
      first_col = row if hp else jnp.maximum(row, nq)
      valid = jnp.logical_and(col >= first_col, col - nq <= row)
      bias_ref[hp] = jnp.where(valid, 0.0, -jnp.inf).astype(F32)

  def cols(sec, p):
    return pl.ds(QKV_COL_OFF[sec] + p * LANES, LANES)

  for t in range(nt):
    for g, (_, d) in enumerate(ATT_GROUPS):
      cur_ref, prev_ref = in_refs[2 * g:2 * g + 2]
      o_ref = o_refs[g]
      _, first_blk, res, loc = _step_tiles(d, sb, j)
      ri, bi = divmod(t, _step_shape(d)[1])
      tile_rows = (ri, pl.ds(bi * nq, nq))
      cur = lambda sec, p, tile_rows=tile_rows: cur_ref[tile_rows + (cols(sec, p),)]
      if bi == 0:
        prev = lambda sec, p, ri=ri: prev_ref[ri, :, cols(sec, p)]
        has_prev = (first_blk > 0).astype(jnp.int32)
      else:
        prev = lambda sec, p, ri=ri, bi=bi: cur_ref[ri, (bi - 1) * nq:bi * nq, cols(sec, p)]
        has_prev = 1

      def store_o(p, o, o_ref=o_ref, tile_rows=tile_rows):
        o_ref[tile_rows + (pl.ds(p * LANES, LANES),)] = o

      m_tile, l_tile = _attention_tile(cur, prev, bias_ref[has_prev], store_o)
      r, b = res(t), loc(t)
      if d == 1:
        rows = pl.ds(pl.multiple_of(b * nq, nq), nq)
      else:
        rows = pl.ds(b * nq * d + r, nq, stride=d)
      max_ref[g, rows, :] = m_tile
      sum_ref[g, rows, :] = l_tile

  @pl.when(j == TILES_PER_SUPER // nt - 1)
  def _():
    def body(c, carry):
      rows = pl.ds(pl.multiple_of(c * MERGE_ROWS, MERGE_ROWS), MERGE_ROWS)
      ms = [max_ref[g, rows, :] for g in range(N_GROUPS)]
      top = functools.reduce(jnp.maximum, ms)
      ws = [jnp.exp2(m - top) for m in ms]
      den = functools.reduce(jnp.add, [w * sum_ref[g, rows, :] for g, w in enumerate(ws)])
      for g, w in enumerate(ws):
        coef_refs[g][rows, :] = w / den
      return carry
    lax.fori_loop(0, SUPER_BLOCK // MERGE_ROWS, body, 0, unroll=2)


def _attention(qkv_groups):
  s = qkv_groups[0].shape[1]
  nt = ATT_TILES_PER_STEP
  in_specs, args, o_specs = [], [], []
  for (_, d), arr in zip(ATT_GROUPS, qkv_groups):
    n_res, blocks = _step_shape(d)

    def cur_map(sb, j, d=d, blocks=blocks):
      res_group, first_blk, _, _ = _step_tiles(d, sb, j)
      return res_group, first_blk // blocks, 0

    def prev_map(sb, j, d=d):
      res_group, first_blk, _, _ = _step_tiles(d, sb, j)
      return res_group, jnp.maximum(first_blk - 1, 0), 0

    in_specs += [pl.BlockSpec((n_res, blocks * ATT_BLOCK, ATT_QKV_WIDTH), cur_map),
                 pl.BlockSpec((n_res, ATT_BLOCK, KV_WIDTH), prev_map)]
    o_specs.append(pl.BlockSpec((n_res, blocks * ATT_BLOCK, GROUP_WIDTH), cur_map))
    args += [arr] * 2
  stats = pltpu.VMEM((N_GROUPS, SUPER_BLOCK, LANES), F32)
  outs = pl.pallas_call(
      _attention_kernel,
      out_shape=([jax.ShapeDtypeStruct((d, s // d, GROUP_WIDTH), BF16) for _, d in ATT_GROUPS]
                 + [jax.ShapeDtypeStruct((s, LANES), F32)] * N_GROUPS),
      grid=(s // SUPER_BLOCK, TILES_PER_SUPER // nt),
      in_specs=in_specs,
      out_specs=o_specs + [pl.BlockSpec((SUPER_BLOCK, LANES), lambda sb, j: (sb, 0))] * N_GROUPS,
      scratch_shapes=[stats, stats, pltpu.VMEM((2, 2 * ATT_BLOCK, 2 * ATT_BLOCK), F32)],
      compiler_params=pltpu.CompilerParams(
          dimension_semantics=("arbitrary", "arbitrary"),
          vmem_limit_bytes=VMEM_LIMIT_BYTES,
      ),
      name="dilated_attention",
  )(*args)
  return outs[:N_GROUPS], outs[N_GROUPS:]


def _silu(x):
  return x * jax.nn.sigmoid(x)


def _tail_kernel(alpha, x_ref, ao1_ref, ao2_ref, ao3_ref, ac1_ref, ac2_ref, ac3_ref,
                 wga_ref, wh_ref, wb_ref, wc_ref, wgc_ref, wg1_ref, wg2_ref,
                 cw_ref, wco_ref, wao_ref, bg_ref, wo_ref, lg_ref, lb_ref, o_ref, u_ref, perm_ref):
  tm = x_ref.shape[0]
  chunk = tm // TAIL_ROW_CHUNKS
  att_refs = ((ao1_ref, ac1_ref), (ao2_ref, ac2_ref), (ao3_ref, ac3_ref))

  def dot(a, w_ref):
    return jnp.dot(a, w_ref[...], preferred_element_type=F32)

  @pl.when(pl.program_id(0) == 0)
  def _():
    u_ref[0:CONV_HALO, :] = jnp.zeros((CONV_HALO, CONV_WIDTH), F32)
    row = lax.broadcasted_iota(jnp.int32, (chunk, chunk), 0)
    col = lax.broadcasted_iota(jnp.int32, (chunk, chunk), 1)
    for g, (_, d) in enumerate(ATT_GROUPS):
      src = (row % d) * (chunk // d) + row // d
      perm_ref[g] = jnp.where(col == src, 1.0, 0.0).astype(BF16)

  def attention_mix(lo):
    rows = pl.ds(lo, chunk)
    att = None
    for g, (_, d) in enumerate(ATT_GROUPS):
      out_ref, coef_ref = att_refs[g]
      n = chunk // d
      if d == 1:
        o_g = out_ref[0, rows, :].astype(F32)
      else:
        res_major = jnp.concatenate([out_ref[r, pl.ds(lo // d, n), :] for r in range(d)], axis=0)
        o_g = jnp.dot(perm_ref[g], res_major, preferred_element_type=F32)
      coef = coef_ref[rows, :]
      factor = jnp.concatenate(
          [jnp.broadcast_to(coef[:, h * HEAD_LANES:h * HEAD_LANES + 1], (chunk, HEAD_DIM))
           for h in range(HEADS_PER_GROUP)], axis=1)
      att = factor * o_g if att is None else att + factor * o_g
    return att

  def row_chunk(lo):
    rows = pl.ds(lo, chunk)
    xb = x_ref[rows, :].astype(BF16)
    u_ref[pl.ds(CONV_HALO + lo, chunk), :] = dot(xb, wc_ref) * dot(xb, wh_ref)
    yield
    p_b, p_gc = dot(xb, wb_ref), dot(xb, wgc_ref)
    conv = cw_ref[CONV_K - 1:CONV_K, :] * u_ref[pl.ds(CONV_HALO + lo, chunk), :]
    for k in range(CONV_K - 1):
      off = CONV_HALO + lo - (CONV_K - 1 - k)
      conv = conv + cw_ref[k:k + 1, :] * u_ref[pl.ds(off, chunk), :]
    a_conv = ((p_b * conv) * _silu(p_gc)).astype(BF16)
    yield
    y_conv = dot(a_conv, wco_ref)
    a_att = (attention_mix(lo) * _silu(dot(xb, wga_ref))).astype(BF16)
    yield
    y_att = dot(a_att, wao_ref)
    m_c = jax.nn.sigmoid(dot(xb, wg1_ref) + bg_ref[:, :D_MODEL]) * y_conv
    yield
    g_a = jax.nn.sigmoid(dot(xb, wg2_ref) + bg_ref[:, D_MODEL:])
    merged = (m_c + g_a * y_att).astype(BF16)
    yield
    y = alpha * x_ref[rows, :] + dot(merged, wo_ref)
    mu = jnp.mean(y, axis=-1, keepdims=True)
    yc = y - mu
    var = jnp.mean(yc * yc, axis=-1, keepdims=True)
    o_ref[rows, :] = yc * lax.rsqrt(var + LN_EPS) * lg_ref[...] + lb_ref[...]
    yield

  for first in range(0, TAIL_ROW_CHUNKS, TAIL_STREAMS):
    streams = [row_chunk(k * chunk) for k in range(first, first + TAIL_STREAMS)]
    for stage in range(TAIL_STAGES):
      for stream in streams:
        next(stream)
      if stage == 1 and first + TAIL_STREAMS == TAIL_ROW_CHUNKS:
        u_ref[0:CONV_HALO, :] = u_ref[tm:tm + CONV_HALO, :]


def _tail(x2d, att_outs, att_coefs, w_bf16, conv_w, w_conv_out, w_att_out, b_gate, w_o, ln_g, ln_b,
          alpha):
  s = x2d.shape[0]
  tm = TAIL_ROW_TILE
  chunk = tm // TAIL_ROW_CHUNKS
  assert all(chunk % (d * 16) == 0 for _, d in ATT_GROUPS)

  def whole(arr):
    return pl.BlockSpec(arr.shape, lambda i: (0,) * arr.ndim, pipeline_mode=pl.Buffered(1))

  def w_cols(off, width):
    assert off % width == 0
    return pl.BlockSpec((D_MODEL, width), lambda i: (0, off // width), pipeline_mode=pl.Buffered(1))

  w_specs = [w_cols(G_ATT_OFF, GROUP_WIDTH)] + [
      w_cols(off, CONV_WIDTH)
      for off in (H_OFF, B_OFF, C_OFF, G_CONV_OFF, GATE_OFF, GATE_OFF + D_MODEL)]
  others = (conv_w, w_conv_out, w_att_out, b_gate, w_o, ln_g, ln_b)
  return pl.pallas_call(
      functools.partial(_tail_kernel, alpha),
      out_shape=jax.ShapeDtypeStruct((s, D_MODEL), F32),
      grid=(s // tm,),
      in_specs=[pl.BlockSpec((tm, D_MODEL), lambda i: (i, 0))]
      + [pl.BlockSpec((d, tm // d, GROUP_WIDTH), lambda i: (0, i, 0)) for _, d in ATT_GROUPS]
      + [pl.BlockSpec((tm, LANES), lambda i: (i, 0))] * N_GROUPS
      + w_specs + [whole(w) for w in others],
      out_specs=pl.BlockSpec((tm, D_MODEL), lambda i: (i, 0)),
      scratch_shapes=[pltpu.VMEM((CONV_HALO + tm, CONV_WIDTH), F32),
                      pltpu.VMEM((N_GROUPS, chunk, chunk), BF16)],
      compiler_params=pltpu.CompilerParams(
          dimension_semantics=("arbitrary",),
          vmem_limit_bytes=VMEM_LIMIT_BYTES,
      ),
      name="conv_merge_norm",
  )(x2d, *att_outs, *att_coefs, *([w_bf16] * len(w_specs)), *others)


def _layer(x2d, w_in, conv_w, w_conv_out, w_att_out, b_gate, w_o, ln_g, ln_b, alpha):
  assert w_in.shape == (D_MODEL, GATE_OFF + 2 * D_MODEL)
  w_bf16 = w_in.astype(BF16)
  qkv_groups = _qkv_projection(x2d, w_bf16)
  att_outs, att_coefs = _attention(qkv_groups)
  return _tail(x2d, att_outs, att_coefs, w_bf16, conv_w, w_conv_out.astype(BF16), w_att_out.astype(BF16),
               b_gate.reshape(1, -1), w_o.astype(BF16), ln_g.reshape(1, -1), ln_b.reshape(1, -1),
               alpha)


def kernel(x, w_in, conv_w, w_conv_out, w_att_out, b_gate, w_o, ln_g, ln_b):
  batch, seq, d_model = x.shape
  depth = w_in.shape[0]
  assert d_model == D_MODEL and seq % SUPER_BLOCK == 0
  alpha = (2.0 * depth) ** 0.25
  outs = []
  for b in range(batch):
    h = x[b]
    for layer in range(depth):
      h = _layer(h, w_in[layer], conv_w[layer], w_conv_out[layer], w_att_out[layer],
                 b_gate[layer], w_o[layer], ln_g[layer], ln_b[layer], alpha)
    outs.append(h)
  return jnp.stack(outs)
```

```python
import functools

import jax
import jax.numpy as jnp
from jax import lax
from jax.experimental import pallas as pl
from jax.experimental.pallas import tpu as pltpu

D_MODEL = 1024
HEAD_DIM = 64
HEADS_PER_GROUP = 8
ATT_GROUPS = ((128, 1), (512, 4), (2048, 16))
N_GROUPS = len(ATT_GROUPS)
GROUP_WIDTH = HEADS_PER_GROUP * HEAD_DIM
ATT_QKV_WIDTH = N_GROUPS * GROUP_WIDTH
CONV_WIDTH = D_MODEL
CONV_K = 3
LN_EPS = 1e-5
LANES = 128

ATT_BLOCK = 128
assert all(w // d == ATT_BLOCK for w, d in ATT_GROUPS)
SUPER_BLOCK = ATT_BLOCK * max(d for _, d in ATT_GROUPS)
TILES_PER_SUPER = SUPER_BLOCK // ATT_BLOCK
ATT_TILES_PER_STEP = 8

QKV_ROW_TILE = 1024
QKV_DOT_ROW_CHUNKS = 1
TAIL_ROW_TILE = 512
TAIL_ROW_CHUNKS = 2
TAIL_STREAMS = 2
assert TAIL_ROW_CHUNKS % TAIL_STREAMS == 0
TAIL_STAGES = 6
CONV_HALO = 8
MERGE_ROWS = 32
HEAD_PAIRS = HEADS_PER_GROUP // 2
assert 2 * HEAD_DIM == LANES
KV_WIDTH = 2 * GROUP_WIDTH
QKV_COL_OFF = (KV_WIDTH, 0, GROUP_WIDTH)
SCORE_SCALE = HEAD_DIM ** -0.5 * 1.4426950408889634

HEAD_LANES = LANES // HEADS_PER_GROUP
VMEM_LIMIT_BYTES = 56 * 1024 * 1024

F32 = jnp.float32
BF16 = jnp.bfloat16

Q_OFF, K_OFF, V_OFF = 0, ATT_QKV_WIDTH, 2 * ATT_QKV_WIDTH
G_ATT_OFF = 3 * ATT_QKV_WIDTH
H_OFF = G_ATT_OFF + GROUP_WIDTH
B_OFF = H_OFF + CONV_WIDTH
C_OFF = B_OFF + CONV_WIDTH
G_CONV_OFF = C_OFF + CONV_WIDTH
GATE_OFF = G_CONV_OFF + CONV_WIDTH


def _qkv_kernel(x_ref, w_ref, *refs):
  tm = x_ref.shape[0]
  out_refs = refs[:N_GROUPS]
  xp_ref = refs[N_GROUPS:2 * N_GROUPS]
  xl_ref = refs[2 * N_GROUPS:]
  dil = [d for _, d in ATT_GROUPS]
  assert dil[0] == 1
  scale = SCORE_SCALE
  lane_tiles = [pl.ds(c * LANES, LANES) for c in range(D_MODEL // LANES)]

  def load_rows():
    def piece(c, lanes):
      xp_ref[0][:, lanes] = x_ref[:, lanes].astype(BF16)
      xl_ref[0][c] = x_ref[:, lanes]
    return [functools.partial(piece, c, lanes) for c, lanes in enumerate(lane_tiles)]

  def permute_rows(g):
    q = dil[g + 1] // dil[g]
    assert dil[g + 1] == q * dil[g]
    n, n_next = tm // dil[g], tm // dil[g + 1]

    def piece(r, a):
      dst = pl.ds((a * dil[g] + r) * n_next, n_next)
      for c, lanes in enumerate(lane_tiles):
        rows = xl_ref[g][c, pl.ds(r * n + a, n_next, stride=q), :]
        xp_ref[g + 1][dst, lanes] = rows.astype(BF16)
        if g + 2 < N_GROUPS:
          xl_ref[g + 1][c, dst, :] = rows
    return [functools.partial(piece, r, a) for r in range(dil[g]) for a in range(q)]

  def project(g, pieces):
    n = tm // dil[g]
    row_chunk = tm // QKV_DOT_ROW_CHUNKS
    n_dots = 3 * QKV_DOT_ROW_CHUNKS
    k = 0
    for sec, off in enumerate((Q_OFF, K_OFF, V_OFF)):
      w_cols = pl.ds(off + g * GROUP_WIDTH, GROUP_WIDTH)
      for rc in range(QKV_DOT_ROW_CHUNKS):
        lo = rc * row_chunk
        res = jnp.dot(xp_ref[g][lo:lo + row_chunk, :], w_ref[:, w_cols],
                      preferred_element_type=F32)
        if sec == 0:
          res = res * scale
        res = res.astype(BF16)
        assert row_chunk % n == 0 or n % row_chunk == 0
        step = min(n, row_chunk)
        for s0 in range(0, row_chunk, step):
          r, l0 = divmod(lo + s0, n)
          out_refs[g][r, l0:l0 + step, pl.ds(QKV_COL_OFF[sec], GROUP_WIDTH)] = res[s0:s0 + step]
        k += 1
        for piece in pieces[(k - 1) * len(pieces) // n_dots:k * len(pieces) // n_dots]:
          piece()

  @pl.when(jnp.logical_and(pl.program_id(0) == 0, pl.program_id(1) == 0))
  def _():
    for piece in load_rows():
      piece()

  for g in range(N_GROUPS):
    @pl.when(pl.program_id(1) == g)
    def _(g=g):
      project(g, permute_rows(g) if g + 1 < N_GROUPS else load_rows())


def _qkv_projection(x2d, w_bf16):
  s = x2d.shape[0]
  tm = QKV_ROW_TILE

  assert (Q_OFF, K_OFF, V_OFF) == (0, ATT_QKV_WIDTH, 2 * ATT_QKV_WIDTH)
  w_spec = pl.BlockSpec((D_MODEL, 3 * ATT_QKV_WIDTH), lambda i, g: (0, 0),
                        pipeline_mode=pl.Buffered(1))

  def x_map(i, g):
    return jnp.minimum(i + g // (N_GROUPS - 1), s // tm - 1), 0

  return pl.pallas_call(
      _qkv_kernel,
      out_shape=[jax.ShapeDtypeStruct((d, s // d, ATT_QKV_WIDTH), BF16) for _, d in ATT_GROUPS],
      grid=(s // tm, N_GROUPS),
      in_specs=[pl.BlockSpec((tm, D_MODEL), x_map), w_spec],
      out_specs=[pl.BlockSpec((d, tm // d, ATT_QKV_WIDTH), lambda i, g: (0, i, 0))
                 for _, d in ATT_GROUPS],
      scratch_shapes=(
          [pltpu.VMEM((tm, D_MODEL), BF16)] * N_GROUPS
          + [pltpu.VMEM((D_MODEL // LANES, tm, LANES), F32)] * (N_GROUPS - 1)),
      compiler_params=pltpu.CompilerParams(
          dimension_semantics=("arbitrary", "arbitrary"),
          vmem_limit_bytes=VMEM_LIMIT_BYTES,
      ),
      name="qkv_projection",
  )(x2d, w_bf16)


def _attention_tile(cur, prev, bias, store_o):
  nq = ATT_BLOCK
  lane = lax.broadcasted_iota(jnp.int32, (nq, LANES), 1)
  low_half = lane < HEAD_DIM
  head_of_lane = lane // HEAD_LANES
  ones = jnp.ones((2 * nq, LANES), BF16)
  m_tile = l_tile = None

  for p in range(HEAD_PAIRS):
    q2 = cur(0, p)
    k2 = jnp.concatenate([prev(1, p), cur(1, p)], axis=0)
    v2 = jnp.concatenate([prev(2, p), cur(2, p)], axis=0)
    zero = jnp.zeros_like(q2)
    qs = jnp.concatenate([jnp.where(low_half, q2, zero), jnp.where(low_half, zero, q2)], axis=0)
    sc = lax.dot_general(qs, k2, (((1,), (1,)), ((), ())), preferred_element_type=F32)
    sc = sc + bias
    m = jnp.max(sc, axis=1, keepdims=True)
    e = jnp.exp2(sc - m).astype(BF16)
    ol = jnp.dot(e, jnp.concatenate([v2, ones], axis=1), preferred_element_type=F32)
    store_o(p, jnp.where(low_half, ol[:nq, :LANES], ol[nq:, :LANES]).astype(BF16))
    for k, rows in enumerate((slice(0, nq), slice(nq, 2 * nq))):
      m_h = jnp.broadcast_to(m[rows], (nq, LANES))
      l_h = ol[rows, LANES:]
      if m_tile is None:
        m_tile, l_tile = m_h, l_h
      else:
        own = head_of_lane == 2 * p + k
        m_tile, l_tile = jnp.where(own, m_h, m_tile), jnp.where(own, l_h, l_tile)
  return m_tile, l_tile


def _step_shape(d):
  nt = ATT_TILES_PER_STEP
  bpr = TILES_PER_SUPER // d
  blocks = min(nt, bpr)
  assert nt % blocks == 0 and bpr % blocks == 0
  return nt // blocks, blocks


def _step_tiles(d, sb, j):
  n_res, blocks = _step_shape(d)
  bpr = TILES_PER_SUPER // d
  spr = bpr // blocks
  b0 = (j % spr) * blocks
  res_group = j // spr
  return (res_group, sb * bpr + b0,
          (lambda t: res_group * n_res + t // blocks), (lambda t: b0 + t % blocks))


def _attention_kernel(*refs):
  in_refs = refs[:2 * N_GROUPS]
  o_refs = refs[2 * N_GROUPS:3 * N_GROUPS]
  coef_refs = refs[3 * N_GROUPS:4 * N_GROUPS]
  max_ref, sum_ref, bias_ref = refs[4 * N_GROUPS:]
  sb = pl.program_id(0)
  j = pl.program_id(1)
  nq = ATT_BLOCK
  nt = ATT_TILES_PER_STEP

  @pl.when(jnp.logical_and(sb == 0, j == 0))
  def _():
    row = lax.broadcasted_iota(jnp.int32, (2 * nq, 2 * nq), 0) % nq
    col = lax.broadcasted_iota(jnp.int32, (2 * nq, 2 * nq), 1)
    for hp in range(2):
      first_col = row if hp else jnp.maximum(row, nq)
      valid = jnp.logical_and(col >= first_col, col - nq <= row)
      bias_ref[hp] = jnp.where(valid, 0.0, -jnp.inf).astype(F32)

  def cols(sec, p):
    return pl.ds(QKV_COL_OFF[sec] + p * LANES, LANES)

  for t in range(nt):
    for g, (_, d) in enumerate(ATT_GROUPS):
      cur_ref, prev_ref = in_refs[2 * g:2 * g + 2]
      o_ref = o_refs[g]
      _, first_blk, res, loc = _step_tiles(d, sb, j)
      ri, bi = divmod(t, _step_shape(d)[1])
      tile_rows = (ri, pl.ds(bi * nq, nq))
      cur = lambda sec, p, tile_rows=tile_rows: cur_ref[tile_rows + (cols(sec, p),)]
      if bi == 0:
        prev = lambda sec, p, ri=ri: prev_ref[ri, :, cols(sec, p)]
        has_prev = (first_blk > 0).astype(jnp.int32)
      else:
        prev = lambda sec, p, ri=ri, bi=bi: cur_ref[ri, (bi - 1) * nq:bi * nq, cols(sec, p)]
        has_prev = 1

      def store_o(p, o, o_ref=o_ref, tile_rows=tile_rows):
        o_ref[tile_rows + (pl.ds(p * LANES, LANES),)] = o

      m_tile, l_tile = _attention_tile(cur, prev, bias_ref[has_prev], store_o)
      r, b = res(t), loc(t)
      if d == 1:
        rows = pl.ds(pl.multiple_of(b * nq, nq), nq)
      else:
        rows = pl.ds(b * nq * d + r, nq, stride=d)
      max_ref[g, rows, :] = m_tile
      sum_ref[g, rows, :] = l_tile

  @pl.when(j == TILES_PER_SUPER // nt - 1)
  def _():
    def body(c, carry):
      rows = pl.ds(pl.multiple_of(c * MERGE_ROWS, MERGE_ROWS), MERGE_ROWS)
      ms = [max_ref[g, rows, :] for g in range(N_GROUPS)]
      top = functools.reduce(jnp.maximum, ms)
      ws = [jnp.exp2(m - top) for m in ms]
      den = functools.reduce(jnp.add, [w * sum_ref[g, rows, :] for g, w in enumerate(ws)])
      for g, w in enumerate(ws):
        coef_refs[g][rows, :] = w / den
      return carry
    lax.fori_loop(0, SUPER_BLOCK // MERGE_ROWS, body, 0, unroll=2)


def _attention(qkv_groups):
  s = qkv_groups[0].shape[1]
  nt = ATT_TILES_PER_STEP
  in_specs, args, o_specs = [], [], []
  for (_, d), arr in zip(ATT_GROUPS, qkv_groups):
    n_res, blocks = _step_shape(d)

    def cur_map(sb, j, d=d, blocks=blocks):
      res_group, first_blk, _, _ = _step_tiles(d, sb, j)
      return res_group, first_blk // blocks, 0

    def prev_map(sb, j, d=d):
      res_group, first_blk, _, _ = _step_tiles(d, sb, j)
      return res_group, jnp.maximum(first_blk - 1, 0), 0

    in_specs += [pl.BlockSpec((n_res, blocks * ATT_BLOCK, ATT_QKV_WIDTH), cur_map),
                 pl.BlockSpec((n_res, ATT_BLOCK, KV_WIDTH), prev_map)]
    o_specs.append(pl.BlockSpec((n_res, blocks * ATT_BLOCK, GROUP_WIDTH), cur_map))
    args += [arr] * 2
  stats = pltpu.VMEM((N_GROUPS, SUPER_BLOCK, LANES), F32)
  outs = pl.pallas_call(
      _attention_kernel,
      out_shape=([jax.ShapeDtypeStruct((d, s // d, GROUP_WIDTH), BF16) for _, d in ATT_GROUPS]
                 + [jax.ShapeDtypeStruct((s, LANES), F32)] * N_GROUPS),
      grid=(s // SUPER_BLOCK, TILES_PER_SUPER // nt),
      in_specs=in_specs,
      out_specs=o_specs + [pl.BlockSpec((SUPER_BLOCK, LANES), lambda sb, j: (sb, 0))] * N_GROUPS,
      scratch_shapes=[stats, stats, pltpu.VMEM((2, 2 * ATT_BLOCK, 2 * ATT_BLOCK), F32)],
      compiler_params=pltpu.CompilerParams(
          dimension_semantics=("arbitrary", "arbitrary"),
          vmem_limit_bytes=VMEM_LIMIT_BYTES,
      ),
      name="dilated_attention",
  )(*args)
  return outs[:N_GROUPS], outs[N_GROUPS:]


def _silu(x):
  return x * jax.nn.sigmoid(x)


def _tail_kernel(alpha, x_ref, ao1_ref, ao2_ref, ao3_ref, ac1_ref, ac2_ref, ac3_ref,
                 wga_ref, wh_ref, wb_ref, wc_ref, wgc_ref, wg1_ref, wg2_ref,
                 cw_ref, wco_ref, wao_ref, bg_ref, wo_ref, lg_ref, lb_ref, o_ref, u_ref, perm_ref):
  tm = x_ref.shape[0]
  chunk = tm // TAIL_ROW_CHUNKS
  att_refs = ((ao1_ref, ac1_ref), (ao2_ref, ac2_ref), (ao3_ref, ac3_ref))

  def dot(a, w_ref):
    return jnp.dot(a, w_ref[...], preferred_element_type=F32)

  @pl.when(pl.program_id(0) == 0)
  def _():
    u_ref[0:CONV_HALO, :] = jnp.zeros((CONV_HALO, CONV_WIDTH), F32)
    row = lax.broadcasted_iota(jnp.int32, (chunk, chunk), 0)
    col = lax.broadcasted_iota(jnp.int32, (chunk, chunk), 1)
    for g, (_, d) in enumerate(ATT_GROUPS):
      src = (row % d) * (chunk // d) + row // d
      perm_ref[g] = jnp.where(col == src, 1.0, 0.0).astype(BF16)

  def attention_mix(lo):
    rows = pl.ds(lo, chunk)
    att = None
    for g, (_, d) in enumerate(ATT_GROUPS):
      out_ref, coef_ref = att_refs[g]
      n = chunk // d
      if d == 1:
        o_g = out_ref[0, rows, :].astype(F32)
      else:
        res_major = jnp.concatenate([out_ref[r, pl.ds(lo // d, n), :] for r in range(d)], axis=0)
        o_g = jnp.dot(perm_ref[g], res_major, preferred_element_type=F32)
      coef = coef_ref[rows, :]
      factor = jnp.concatenate(
          [jnp.broadcast_to(coef[:, h * HEAD_LANES:h * HEAD_LANES + 1], (chunk, HEAD_DIM))
           for h in range(HEADS_PER_GROUP)], axis=1)
      att = factor * o_g if att is None else att + factor * o_g
    return att

  def row_chunk(lo):
    rows = pl.ds(lo, chunk)
    xb = x_ref[rows, :].astype(BF16)
    u_ref[pl.ds(CONV_HALO + lo, chunk), :] = dot(xb, wc_ref) * dot(xb, wh_ref)
    yield
    p_b, p_gc = dot(xb, wb_ref), dot(xb, wgc_ref)
    conv = cw_ref[CONV_K - 1:CONV_K, :] * u_ref[pl.ds(CONV_HALO + lo, chunk), :]
    for k in range(CONV_K - 1):
      off = CONV_HALO + lo - (CONV_K - 1 - k)
      conv = conv + cw_ref[k:k + 1, :] * u_ref[pl.ds(off, chunk), :]
    a_conv = ((p_b * conv) * _silu(p_gc)).astype(BF16)
    yield
    y_conv = dot(a_conv, wco_ref)
    a_att = (attention_mix(lo) * _silu(dot(xb, wga_ref))).astype(BF16)
    yield
    y_att = dot(a_att, wao_ref)
    m_c = jax.nn.sigmoid(dot(xb, wg1_ref) + bg_ref[:, :D_MODEL]) * y_conv
    yield
    g_a = jax.nn.sigmoid(dot(xb, wg2_ref) + bg_ref[:, D_MODEL:])
    merged = (m_c + g_a * y_att).astype(BF16)
    yield
    y = alpha * x_ref[rows, :] + dot(merged, wo_ref)
    mu = jnp.mean(y, axis=-1, keepdims=True)
    yc = y - mu
    var = jnp.mean(yc * yc, axis=-1, keepdims=True)
    o_ref[rows, :] = yc * lax.rsqrt(var + LN_EPS) * lg_ref[...] + lb_ref[...]
    yield

  for first in range(0, TAIL_ROW_CHUNKS, TAIL_STREAMS):
    streams = [row_chunk(k * chunk) for k in range(first, first + TAIL_STREAMS)]
    for stage in range(TAIL_STAGES):
      for stream in streams:
        next(stream)
      if stage == 1 and first + TAIL_STREAMS == TAIL_ROW_CHUNKS:
        u_ref[0:CONV_HALO, :] = u_ref[tm:tm + CONV_HALO, :]


def _tail(x2d, att_outs, att_coefs, w_bf16, conv_w, w_conv_out, w_att_out, b_gate, w_o, ln_g, ln_b,
          alpha):
  s = x2d.shape[0]
  tm = TAIL_ROW_TILE
  chunk = tm // TAIL_ROW_CHUNKS
  assert all(chunk % (d * 16) == 0 for _, d in ATT_GROUPS)

  def whole(arr):
    return pl.BlockSpec(arr.shape, lambda i: (0,) * arr.ndim, pipeline_mode=pl.Buffered(1))

  def w_cols(off, width):
    assert off % width == 0
    return pl.BlockSpec((D_MODEL, width), lambda i: (0, off // width), pipeline_mode=pl.Buffered(1))

  w_specs = [w_cols(G_ATT_OFF, GROUP_WIDTH)] + [
      w_cols(off, CONV_WIDTH)
      for off in (H_OFF, B_OFF, C_OFF, G_CONV_OFF, GATE_OFF, GATE_OFF + D_MODEL)]
  others = (conv_w, w_conv_out, w_att_out, b_gate, w_o, ln_g, ln_b)
  return pl.pallas_call(
      functools.partial(_tail_kernel, alpha),
      out_shape=jax.ShapeDtypeStruct((s, D_MODEL), F32),
      grid=(s // tm,),
      in_specs=[pl.BlockSpec((tm, D_MODEL), lambda i: (i, 0))]
      + [pl.BlockSpec((d, tm // d, GROUP_WIDTH), lambda i: (0, i, 0)) for _, d in ATT_GROUPS]
      + [pl.BlockSpec((tm, LANES), lambda i: (i, 0))] * N_GROUPS
      + w_specs + [whole(w) for w in others],
      out_specs=pl.BlockSpec((tm, D_MODEL), lambda i: (i, 0)),
      scratch_shapes=[pltpu.VMEM((CONV_HALO + tm, CONV_WIDTH), F32),
                      pltpu.VMEM((N_GROUPS, chunk, chunk), BF16)],
      compiler_params=pltpu.CompilerParams(
          dimension_semantics=("arbitrary",),
          vmem_limit_bytes=VMEM_LIMIT_BYTES,
      ),
      name="conv_merge_norm",
  )(x2d, *att_outs, *att_coefs, *([w_bf16] * len(w_specs)), *others)


def _layer(x2d, w_in, conv_w, w_conv_out, w_att_out, b_gate, w_o, ln_g, ln_b, alpha):
  assert w_in.shape == (D_MODEL, GATE_OFF + 2 * D_MODEL)
  w_bf16 = w_in.astype(BF16)
  qkv_groups = _qkv_projection(x2d, w_bf16)
  att_outs, att_coefs = _attention(qkv_groups)
  return _tail(x2d, att_outs, att_coefs, w_bf16, conv_w, w_conv_out.astype(BF16), w_att_out.astype(BF16),
               b_gate.reshape(1, -1), w_o.astype(BF16), ln_g.reshape(1, -1), ln_b.reshape(1, -1),
               alpha)


def kernel(x, w_in, conv_w, w_conv_out, w_att_out, b_gate, w_o, ln_g, ln_b):
  batch, seq, d_model = x.shape
  depth = w_in.shape[0]
  assert d_model == D_MODEL and seq % SUPER_BLOCK == 0
  alpha = (2.0 * depth) ** 0.25
  outs = []
  for b in range(batch):
    h = x[b]
    for layer in range(depth):
      h = _layer(h, w_in[layer], conv_w[layer], w_conv_out[layer], w_att_out[layer],
                 b_gate[layer], w_o[layer], ln_g[layer], ln_b[layer], alpha)
    outs.append(h)
  return jnp.stack(outs)
```

```python
import functools

import jax
import jax.numpy as jnp
from jax import lax
from jax.experimental import pallas as pl
from jax.experimental.pallas import tpu as pltpu

D_MODEL = 1024
HEAD_DIM = 64
HEADS_PER_GROUP = 8
ATT_GROUPS = ((128, 1), (512, 4), (2048, 16))
N_GROUPS = len(ATT_GROUPS)
GROUP_WIDTH = HEADS_PER_GROUP * HEAD_DIM
ATT_QKV_WIDTH = N_GROUPS * GROUP_WIDTH
CONV_WIDTH = D_MODEL
CONV_K = 3
LN_EPS = 1e-5
LANES = 128

ATT_BLOCK = 128
assert all(w // d == ATT_BLOCK for w, d in ATT_GROUPS)
SUPER_BLOCK = ATT_BLOCK * max(d for _, d in ATT_GROUPS)
TILES_PER_SUPER = SUPER_BLOCK // ATT_BLOCK
ATT_TILES_PER_STEP = 8

QKV_ROW_TILE = 1024
QKV_DOT_ROW_CHUNKS = 2
TAIL_ROW_TILE = 1024
TAIL_ROW_CHUNKS = 4
TAIL_STREAMS = 2
assert TAIL_ROW_CHUNKS % TAIL_STREAMS == 0
TAIL_STAGES = 6
CONV_HALO = 8
MERGE_ROWS = 32
HEAD_PAIRS = HEADS_PER_GROUP // 2
assert 2 * HEAD_DIM == LANES
KV_WIDTH = 2 * GROUP_WIDTH
QKV_COL_OFF = (KV_WIDTH, 0, GROUP_WIDTH)
SCORE_SCALE = HEAD_DIM ** -0.5 * 1.4426950408889634

assert N_GROUPS * HEADS_PER_GROUP <= LANES
VMEM_LIMIT_BYTES = 56 * 1024 * 1024
TAIL_VMEM_LIMIT_BYTES = 58 * 1024 * 1024

F32 = jnp.float32
BF16 = jnp.bfloat16

Q_OFF, K_OFF, V_OFF = 0, ATT_QKV_WIDTH, 2 * ATT_QKV_WIDTH
G_ATT_OFF = 3 * ATT_QKV_WIDTH
H_OFF = G_ATT_OFF + GROUP_WIDTH
B_OFF = H_OFF + CONV_WIDTH
C_OFF = B_OFF + CONV_WIDTH
G_CONV_OFF = C_OFF + CONV_WIDTH
GATE_OFF = G_CONV_OFF + CONV_WIDTH


def _qkv_kernel(x_ref, w_ref, *refs):
  tm = x_ref.shape[0]
  out_refs = refs[:N_GROUPS]
  xp_ref = refs[N_GROUPS:2 * N_GROUPS]
  xl_ref = refs[2 * N_GROUPS:]
  dil = [d for _, d in ATT_GROUPS]
  assert dil[0] == 1
  scale = SCORE_SCALE
  lane_tiles = [pl.ds(c * LANES, LANES) for c in range(D_MODEL // LANES)]

  def load_rows():
    def piece(c, lanes):
      xp_ref[0][:, lanes] = x_ref[:, lanes].astype(BF16)
      xl_ref[0][c] = x_ref[:, lanes]
    return [functools.partial(piece, c, lanes) for c, lanes in enumerate(lane_tiles)]

  def permute_rows(g):
    q = dil[g + 1] // dil[g]
    assert dil[g + 1] == q * dil[g]
    n, n_next = tm // dil[g], tm // dil[g + 1]

    def piece(r, a):
      dst = pl.ds((a * dil[g] + r) * n_next, n_next)
      for c, lanes in enumerate(lane_tiles):
        rows = xl_ref[g][c, pl.ds(r * n + a, n_next, stride=q), :]
        xp_ref[g + 1][dst, lanes] = rows.astype(BF16)
        if g + 2 < N_GROUPS:
          xl_ref[g + 1][c, dst, :] = rows
    return [functools.partial(piece, r, a) for r in range(dil[g]) for a in range(q)]

  def project(g, pieces):
    n = tm // dil[g]
    row_chunk = tm // QKV_DOT_ROW_CHUNKS
    n_dots = 3 * QKV_DOT_ROW_CHUNKS
    k = 0
    for sec, off in enumerate((Q_OFF, K_OFF, V_OFF)):
      w_cols = pl.ds(off + g * GROUP_WIDTH, GROUP_WIDTH)
      for rc in range(QKV_DOT_ROW_CHUNKS):
        lo = rc * row_chunk
        res = jnp.dot(xp_ref[g][lo:lo + row_chunk, :], w_ref[:, w_cols],
                      preferred_element_type=F32)
        if sec == 0:
          res = res * scale
        res = res.astype(BF16)
        assert row_chunk % n == 0 or n % row_chunk == 0
        step = min(n, row_chunk)
        for s0 in range(0, row_chunk, step):
          r, l0 = divmod(lo + s0, n)
          out_refs[g][r, l0:l0 + step, pl.ds(QKV_COL_OFF[sec], GROUP_WIDTH)] = res[s0:s0 + step]
        k += 1
        for piece in pieces[(k - 1) * len(pieces) // n_dots:k * len(pieces) // n_dots]:
          piece()

  @pl.when(jnp.logical_and(pl.program_id(0) == 0, pl.program_id(1) == 0))
  def _():
    for piece in load_rows():
      piece()

  for g in range(N_GROUPS):
    @pl.when(pl.program_id(1) == g)
    def _(g=g):
      project(g, permute_rows(g) if g + 1 < N_GROUPS else load_rows())


def _qkv_projection(x2d, w_bf16):
  s = x2d.shape[0]
  tm = QKV_ROW_TILE

  assert (Q_OFF, K_OFF, V_OFF) == (0, ATT_QKV_WIDTH, 2 * ATT_QKV_WIDTH)
  w_spec = pl.BlockSpec((D_MODEL, 3 * ATT_QKV_WIDTH), lambda i, g: (0, 0),
                        pipeline_mode=pl.Buffered(1))

  def x_map(i, g):
    return jnp.minimum(i + g // (N_GROUPS - 1), s // tm - 1), 0

  return pl.pallas_call(
      _qkv_kernel,
      out_shape=[jax.ShapeDtypeStruct((d, s // d, ATT_QKV_WIDTH), BF16) for _, d in ATT_GROUPS],
      grid=(s // tm, N_GROUPS),
      in_specs=[pl.BlockSpec((tm, D_MODEL), x_map), w_spec],
      out_specs=[pl.BlockSpec((d, tm // d, ATT_QKV_WIDTH), lambda i, g: (0, i, 0))
                 for _, d in ATT_GROUPS],
      scratch_shapes=(
          [pltpu.VMEM((tm, D_MODEL), BF16)] * N_GROUPS
          + [pltpu.VMEM((D_MODEL // LANES, tm, LANES), F32)] * (N_GROUPS - 1)),
      compiler_params=pltpu.CompilerParams(
          dimension_semantics=("arbitrary", "arbitrary"),
          vmem_limit_bytes=VMEM_LIMIT_BYTES,
      ),
      name="qkv_projection",
  )(x2d, w_bf16)


def _attention_tile(cur, prev, bias, store_o):
  nq = ATT_BLOCK
  lane = lax.broadcasted_iota(jnp.int32, (nq, LANES), 1)
  low_half = lane < HEAD_DIM
  head_of_lane = lane % HEADS_PER_GROUP
  ones = jnp.ones((2 * nq, LANES), BF16)
  m_tile = l_tile = None

  for p in range(HEAD_PAIRS):
    q2 = cur(0, p)
    k2 = jnp.concatenate([prev(1, p), cur(1, p)], axis=0)
    v2 = jnp.concatenate([prev(2, p), cur(2, p)], axis=0)
    zero = jnp.zeros_like(q2)
    qs = jnp.concatenate([jnp.where(low_half, q2, zero), jnp.where(low_half, zero, q2)], axis=0)
    sc = lax.dot_general(qs, k2, (((1,), (1,)), ((), ())), preferred_element_type=F32)
    sc = sc + bias
    m = jnp.max(sc, axis=1, keepdims=True)
    e = jnp.exp2(sc - m).astype(BF16)
    ol = jnp.dot(e, jnp.concatenate([v2, ones], axis=1), preferred_element_type=F32)
    store_o(p, jnp.where(low_half, ol[:nq, :LANES], ol[nq:, :LANES]).astype(BF16))
    for k, rows in enumerate((slice(0, nq), slice(nq, 2 * nq))):
      m_h = jnp.broadcast_to(m[rows], (nq, LANES))
      l_h = ol[rows, LANES:]
      if m_tile is None:
        m_tile, l_tile = m_h, l_h
      else:
        own = head_of_lane == 2 * p + k
        m_tile, l_tile = jnp.where(own, m_h, m_tile), jnp.where(own, l_h, l_tile)
  return m_tile, l_tile


def _step_shape(d):
  nt = ATT_TILES_PER_STEP
  bpr = TILES_PER_SUPER // d
  blocks = min(nt, bpr)
  assert nt % blocks == 0 and bpr % blocks == 0
  return nt // blocks, blocks


def _step_tiles(d, sb, j):
  n_res, blocks = _step_shape(d)
  bpr = TILES_PER_SUPER // d
  spr = bpr // blocks
  b0 = (j % spr) * blocks
  res_group = j // spr
  return (res_group, sb * bpr + b0,
          (lambda t: res_group * n_res + t // blocks), (lambda t: b0 + t % blocks))


def _attention_kernel(*refs):
  in_refs = refs[:2 * N_GROUPS]
  o_refs = refs[2 * N_GROUPS:3 * N_GROUPS]
  coef_ref = refs[3 * N_GROUPS]
  max_ref, sum_ref, bias_ref = refs[3 * N_GROUPS + 1:]
  sb = pl.program_id(0)
  j = pl.program_id(1)
  nq = ATT_BLOCK
  nt = ATT_TILES_PER_STEP

  @pl.when(jnp.logical_and(sb == 0, j == 0))
  def _():
    row = lax.broadcasted_iota(jnp.int32, (2 * nq, 2 * nq), 0) % nq
    col = lax.broadcasted_iota(jnp.int32, (2 * nq, 2 * nq), 1)
    for hp in range(2):
      first_col = row if hp else jnp.maximum(row, nq)
      valid = jnp.logical_and(col >= first_col, col - nq <= row)
      bias_ref[hp] = jnp.where(valid, 0.0, -jnp.inf).astype(F32)

  def cols(sec, p):
    return pl.ds(QKV_COL_OFF[sec] + p * LANES, LANES)

  for t in range(nt):
    for g, (_, d) in enumerate(ATT_GROUPS):
      cur_ref, prev_ref = in_refs[2 * g:2 * g + 2]
      o_ref = o_refs[g]
      _, first_blk, res, loc = _step_tiles(d, sb, j)
      ri, bi = divmod(t, _step_shape(d)[1])
      tile_rows = (ri, pl.ds(bi * nq, nq))
      cur = lambda sec, p, tile_rows=tile_rows: cur_ref[tile_rows + (cols(sec, p),)]
      if bi == 0:
        prev = lambda sec, p, ri=ri: prev_ref[ri, :, cols(sec, p)]
        has_prev = (first_blk > 0).astype(jnp.int32)
      else:
        prev = lambda sec, p, ri=ri, bi=bi: cur_ref[ri, (bi - 1) * nq:bi * nq, cols(sec, p)]
        has_prev = 1

      def store_o(p, o, o_ref=o_ref, tile_rows=tile_rows):
        o_ref[tile_rows + (pl.ds(p * LANES, LANES),)] = o

      m_tile, l_tile = _attention_tile(cur, prev, bias_ref[has_prev], store_o)
      r, b = res(t), loc(t)
      if d == 1:
        rows = pl.ds(pl.multiple_of(b * nq, nq), nq)
      else:
        rows = pl.ds(b * nq * d + r, nq, stride=d)
      max_ref[g, rows, :] = m_tile
      sum_ref[g, rows, :] = l_tile

  @pl.when(j == TILES_PER_SUPER // nt - 1)
  def _():
    def body(c, carry):
      rows = pl.ds(pl.multiple_of(c * MERGE_ROWS, MERGE_ROWS), MERGE_ROWS)
      ms = [max_ref[g, rows, :] for g in range(N_GROUPS)]
      top = functools.reduce(jnp.maximum, ms)
      ws = [jnp.exp2(m - top) for m in ms]
      den = functools.reduce(jnp.add, [w * sum_ref[g, rows, :] for g, w in enumerate(ws)])
      group_of_lane = lax.broadcasted_iota(jnp.int32, ws[0].shape, 1) // HEADS_PER_GROUP
      w = functools.reduce(lambda acc, g: jnp.where(group_of_lane == g, ws[g], acc),
                           range(1, N_GROUPS), ws[0])
      coef_ref[rows, :] = w / den
      return carry
    lax.fori_loop(0, SUPER_BLOCK // MERGE_ROWS, body, 0, unroll=2)


def _attention(qkv_groups):
  s = qkv_groups[0].shape[1]
  nt = ATT_TILES_PER_STEP
  in_specs, args, o_specs = [], [], []
  for (_, d), arr in zip(ATT_GROUPS, qkv_groups):
    n_res, blocks = _step_shape(d)

    def cur_map(sb, j, d=d, blocks=blocks):
      res_group, first_blk, _, _ = _step_tiles(d, sb, j)
      return res_group, first_blk // blocks, 0

    def prev_map(sb, j, d=d):
      res_group, first_blk, _, _ = _step_tiles(d, sb, j)
      return res_group, jnp.maximum(first_blk - 1, 0), 0

    in_specs += [pl.BlockSpec((n_res, blocks * ATT_BLOCK, ATT_QKV_WIDTH), cur_map),
                 pl.BlockSpec((n_res, ATT_BLOCK, KV_WIDTH), prev_map)]
    o_specs.append(pl.BlockSpec((n_res, blocks * ATT_BLOCK, GROUP_WIDTH), cur_map))
    args += [arr] * 2
  stats = pltpu.VMEM((N_GROUPS, SUPER_BLOCK, LANES), F32)
  outs = pl.pallas_call(
      _attention_kernel,
      out_shape=([jax.ShapeDtypeStruct((d, s // d, GROUP_WIDTH), BF16) for _, d in ATT_GROUPS]
                 + [jax.ShapeDtypeStruct((s, LANES), F32)]),
      grid=(s // SUPER_BLOCK, TILES_PER_SUPER // nt),
      in_specs=in_specs,
      out_specs=o_specs + [pl.BlockSpec((SUPER_BLOCK, LANES), lambda sb, j: (sb, 0))],
      scratch_shapes=[stats, stats, pltpu.VMEM((2, 2 * ATT_BLOCK, 2 * ATT_BLOCK), F32)],
      compiler_params=pltpu.CompilerParams(
          dimension_semantics=("arbitrary", "arbitrary"),
          vmem_limit_bytes=VMEM_LIMIT_BYTES,
      ),
      name="dilated_attention",
  )(*args)
  return outs[:N_GROUPS], outs[N_GROUPS]


def _silu(x):
  return x * jax.nn.sigmoid(x)


def _tail_kernel(alpha, x_ref, ao1_ref, ao2_ref, ao3_ref, coef_ref,
                 wga_ref, wh_ref, wb_ref, wc_ref, wgc_ref, wg1_ref, wg2_ref,
                 cw_ref, wco_ref, wao_ref, bg_ref, wo_ref, lg_ref, lb_ref, o_ref, u_ref, perm_ref):
  tm = x_ref.shape[0]
  chunk = tm // TAIL_ROW_CHUNKS
  out_refs = (ao1_ref, ao2_ref, ao3_ref)

  def dot(a, w_ref):
    return jnp.dot(a, w_ref[...], preferred_element_type=F32)

  @pl.when(pl.program_id(0) == 0)
  def _():
    u_ref[0:CONV_HALO, :] = jnp.zeros((CONV_HALO, CONV_WIDTH), F32)
    row = lax.broadcasted_iota(jnp.int32, (chunk, chunk), 0)
    col = lax.broadcasted_iota(jnp.int32, (chunk, chunk), 1)
    for g, (_, d) in enumerate(ATT_GROUPS):
      src = (row % d) * (chunk // d) + row // d
      perm_ref[g] = jnp.where(col == src, 1.0, 0.0).astype(BF16)

  def attention_mix(lo):
    rows = pl.ds(lo, chunk)
    coef = coef_ref[rows, :]
    att = None
    for g, (_, d) in enumerate(ATT_GROUPS):
      out_ref = out_refs[g]
      n = chunk // d
      if d == 1:
        o_g = out_ref[0, rows, :].astype(F32)
      else:
        res_major = jnp.concatenate([out_ref[r, pl.ds(lo // d, n), :] for r in range(d)], axis=0)
        o_g = jnp.dot(perm_ref[g], res_major, preferred_element_type=F32)
      factor = jnp.concatenate(
          [jnp.broadcast_to(coef[:, g * HEADS_PER_GROUP + h:g * HEADS_PER_GROUP + h + 1],
                            (chunk, HEAD_DIM)) for h in range(HEADS_PER_GROUP)], axis=1)
      att = factor * o_g if att is None else att + factor * o_g
    return att

  def row_chunk(lo):
    rows = pl.ds(lo, chunk)
    xb = x_ref[rows, :].astype(BF16)
    u_ref[pl.ds(CONV_HALO + lo, chunk), :] = dot(xb, wc_ref) * dot(xb, wh_ref)
    yield
    p_b, p_gc = dot(xb, wb_ref), dot(xb, wgc_ref)
    conv = cw_ref[CONV_K - 1:CONV_K, :] * u_ref[pl.ds(CONV_HALO + lo, chunk), :]
    for k in range(CONV_K - 1):
      off = CONV_HALO + lo - (CONV_K - 1 - k)
      conv = conv + cw_ref[k:k + 1, :] * u_ref[pl.ds(off, chunk), :]
    a_conv = ((p_b * conv) * _silu(p_gc)).astype(BF16)
    yield
    y_conv = dot(a_conv, wco_ref)
    a_att = (attention_mix(lo) * _silu(dot(xb, wga_ref))).astype(BF16)
    yield
    y_att = dot(a_att, wao_ref)
    m_c = jax.nn.sigmoid(dot(xb, wg1_ref) + bg_ref[:, :D_MODEL]) * y_conv
    yield
    g_a = jax.nn.sigmoid(dot(xb, wg2_ref) + bg_ref[:, D_MODEL:])
    merged = (m_c + g_a * y_att).astype(BF16)
    yield
    y = alpha * x_ref[rows, :] + dot(merged, wo_ref)
    mu = jnp.mean(y, axis=-1, keepdims=True)
    yc = y - mu
    var = jnp.mean(yc * yc, axis=-1, keepdims=True)
    o_ref[rows, :] = yc * lax.rsqrt(var + LN_EPS) * lg_ref[...] + lb_ref[...]
    yield

  for first in range(0, TAIL_ROW_CHUNKS, TAIL_STREAMS):
    streams = [row_chunk(k * chunk) for k in range(first, first + TAIL_STREAMS)]
    for stage in range(TAIL_STAGES):
      for stream in streams:
        next(stream)
      if stage == 1 and first + TAIL_STREAMS == TAIL_ROW_CHUNKS:
        u_ref[0:CONV_HALO, :] = u_ref[tm:tm + CONV_HALO, :]


def _tail(x2d, att_outs, att_coefs, w_bf16, conv_w, w_conv_out, w_att_out, b_gate, w_o, ln_g, ln_b,
          alpha):
  s = x2d.shape[0]
  tm = TAIL_ROW_TILE
  chunk = tm // TAIL_ROW_CHUNKS
  assert all(chunk % (d * 16) == 0 for _, d in ATT_GROUPS)

  def whole(arr):
    return pl.BlockSpec(arr.shape, lambda i: (0,) * arr.ndim, pipeline_mode=pl.Buffered(1))

  def w_cols(off, width):
    assert off % width == 0
    return pl.BlockSpec((D_MODEL, width), lambda i: (0, off // width), pipeline_mode=pl.Buffered(1))

  w_specs = [w_cols(G_ATT_OFF, GROUP_WIDTH)] + [
      w_cols(off, CONV_WIDTH)
      for off in (H_OFF, B_OFF, C_OFF, G_CONV_OFF, GATE_OFF, GATE_OFF + D_MODEL)]
  others = (conv_w, w_conv_out, w_att_out, b_gate, w_o, ln_g, ln_b)
  return pl.pallas_call(
      functools.partial(_tail_kernel, alpha),
      out_shape=jax.ShapeDtypeStruct((s, D_MODEL), F32),
      grid=(s // tm,),
      in_specs=[pl.BlockSpec((tm, D_MODEL), lambda i: (i, 0))]
      + [pl.BlockSpec((d, tm // d, GROUP_WIDTH), lambda i: (0, i, 0)) for _, d in ATT_GROUPS]
      + [pl.BlockSpec((tm, LANES), lambda i: (i, 0))]
      + w_specs + [whole(w) for w in others],
      out_specs=pl.BlockSpec((tm, D_MODEL), lambda i: (i, 0)),
      scratch_shapes=[pltpu.VMEM((CONV_HALO + tm, CONV_WIDTH), F32),
                      pltpu.VMEM((N_GROUPS, chunk, chunk), BF16)],
      compiler_params=pltpu.CompilerParams(
          dimension_semantics=("arbitrary",),
          vmem_limit_bytes=TAIL_VMEM_LIMIT_BYTES,
      ),
      name="conv_merge_norm",
  )(x2d, *att_outs, att_coefs, *([w_bf16] * len(w_specs)), *others)


def _layer(x2d, w_in, conv_w, w_conv_out, w_att_out, b_gate, w_o, ln_g, ln_b, alpha):
  assert w_in.shape == (D_MODEL, GATE_OFF + 2 * D_MODEL)
  w_bf16 = w_in.astype(BF16)
  qkv_groups = _qkv_projection(x2d, w_bf16)
  att_outs, att_coefs = _attention(qkv_groups)
  return _tail(x2d, att_outs, att_coefs, w_bf16, conv_w, w_conv_out.astype(BF16), w_att_out.astype(BF16),
               b_gate.reshape(1, -1), w_o.astype(BF16), ln_g.reshape(1, -1), ln_b.reshape(1, -1),
               alpha)


def kernel(x, w_in, conv_w, w_conv_out, w_att_out, b_gate, w_o, ln_g, ln_b):
  batch, seq, d_model = x.shape
  depth = w_in.shape[0]
  assert d_model == D_MODEL and seq % SUPER_BLOCK == 0
  alpha = (2.0 * depth) ** 0.25
  outs = []
  for b in range(batch):
    h = x[b]
    for layer in range(depth):
      h = _layer(h, w_in[layer], conv_w[layer], w_conv_out[layer], w_att_out[layer],
                 b_gate[layer], w_o[layer], ln_g[layer], ln_b[layer], alpha)
    outs.append(h)
  return jnp.stack(outs)
```

```python
import functools

import jax
import jax.numpy as jnp
from jax import lax
from jax.experimental import pallas as pl
from jax.experimental.pallas import tpu as pltpu

D_MODEL = 1024
HEAD_DIM = 64
HEADS_PER_GROUP = 8
ATT_GROUPS = ((128, 1), (512, 4), (2048, 16))
N_GROUPS = len(ATT_GROUPS)
GROUP_WIDTH = HEADS_PER_GROUP * HEAD_DIM
ATT_QKV_WIDTH = N_GROUPS * GROUP_WIDTH
CONV_WIDTH = D_MODEL
CONV_K = 3
LN_EPS = 1e-5
LANES = 128

ATT_BLOCK = 128
assert all(w // d == ATT_BLOCK for w, d in ATT_GROUPS)
SUPER_BLOCK = ATT_BLOCK * max(d for _, d in ATT_GROUPS)
TILES_PER_SUPER = SUPER_BLOCK // ATT_BLOCK
ATT_TILES_PER_STEP = 8

QKV_ROW_TILE = 1024
QKV_DOT_ROW_CHUNKS = 2
TAIL_ROW_TILE = 512
TAIL_ROW_CHUNKS = 2
TAIL_STREAMS = 2
assert TAIL_ROW_CHUNKS % TAIL_STREAMS == 0
TAIL_STAGES = 6
CONV_HALO = 8
MERGE_ROWS = 32
HEAD_PAIRS = HEADS_PER_GROUP // 2
assert 2 * HEAD_DIM == LANES
KV_WIDTH = 2 * GROUP_WIDTH
QKV_COL_OFF = (KV_WIDTH, 0, GROUP_WIDTH)
SCORE_SCALE = HEAD_DIM ** -0.5 * 1.4426950408889634

assert N_GROUPS * HEADS_PER_GROUP <= LANES
VMEM_LIMIT_BYTES = 56 * 1024 * 1024

F32 = jnp.float32
BF16 = jnp.bfloat16

Q_OFF, K_OFF, V_OFF = 0, ATT_QKV_WIDTH, 2 * ATT_QKV_WIDTH
G_ATT_OFF = 3 * ATT_QKV_WIDTH
H_OFF = G_ATT_OFF + GROUP_WIDTH
B_OFF = H_OFF + CONV_WIDTH
C_OFF = B_OFF + CONV_WIDTH
G_CONV_OFF = C_OFF + CONV_WIDTH
GATE_OFF = G_CONV_OFF + CONV_WIDTH


def _qkv_kernel(x_ref, w_ref, *refs):
  tm = x_ref.shape[0]
  out_refs = refs[:N_GROUPS]
  xp_ref = refs[N_GROUPS:2 * N_GROUPS]
  xl_ref = refs[2 * N_GROUPS:]
  dil = [d for _, d in ATT_GROUPS]
  assert dil[0] == 1
  scale = SCORE_SCALE
  lane_tiles = [pl.ds(c * LANES, LANES) for c in range(D_MODEL // LANES)]

  def load_rows():
    def piece(c, lanes):
      xp_ref[0][:, lanes] = x_ref[:, lanes].astype(BF16)
      xl_ref[0][c] = x_ref[:, lanes]
    return [functools.partial(piece, c, lanes) for c, lanes in enumerate(lane_tiles)]

  def permute_rows(g):
    q = dil[g + 1] // dil[g]
    assert dil[g + 1] == q * dil[g]
    n, n_next = tm // dil[g], tm // dil[g + 1]

    def piece(r, a):
      dst = pl.ds((a * dil[g] + r) * n_next, n_next)
      for c, lanes in enumerate(lane_tiles):
        rows = xl_ref[g][c, pl.ds(r * n + a, n_next, stride=q), :]
        xp_ref[g + 1][dst, lanes] = rows.astype(BF16)
        if g + 2 < N_GROUPS:
          xl_ref[g + 1][c, dst, :] = rows
    return [functools.partial(piece, r, a) for r in range(dil[g]) for a in range(q)]

  def project(g, pieces):
    n = tm // dil[g]
    row_chunk = tm // QKV_DOT_ROW_CHUNKS
    n_dots = 3 * QKV_DOT_ROW_CHUNKS
    k = 0
    for sec, off in enumerate((Q_OFF, K_OFF, V_OFF)):
      w_cols = pl.ds(off + g * GROUP_WIDTH, GROUP_WIDTH)
      for rc in range(QKV_DOT_ROW_CHUNKS):
        lo = rc * row_chunk
        res = jnp.dot(xp_ref[g][lo:lo + row_chunk, :], w_ref[:, w_cols],
                      preferred_element_type=F32)
        if sec == 0:
          res = res * scale
        res = res.astype(BF16)
        assert row_chunk % n == 0 or n % row_chunk == 0
        step = min(n, row_chunk)
        for s0 in range(0, row_chunk, step):
          r, l0 = divmod(lo + s0, n)
          out_refs[g][r, l0:l0 + step, pl.ds(QKV_COL_OFF[sec], GROUP_WIDTH)] = res[s0:s0 + step]
        k += 1
        for piece in pieces[(k - 1) * len(pieces) // n_dots:k * len(pieces) // n_dots]:
          piece()

  @pl.when(jnp.logical_and(pl.program_id(0) == 0, pl.program_id(1) == 0))
  def _():
    for piece in load_rows():
      piece()

  for g in range(N_GROUPS):
    @pl.when(pl.program_id(1) == g)
    def _(g=g):
      project(g, permute_rows(g) if g + 1 < N_GROUPS else load_rows())


def _qkv_projection(x2d, w_bf16):
  s = x2d.shape[0]
  tm = QKV_ROW_TILE

  assert (Q_OFF, K_OFF, V_OFF) == (0, ATT_QKV_WIDTH, 2 * ATT_QKV_WIDTH)
  w_spec = pl.BlockSpec((D_MODEL, 3 * ATT_QKV_WIDTH), lambda i, g: (0, 0),
                        pipeline_mode=pl.Buffered(1))

  def x_map(i, g):
    return jnp.minimum(i + g // (N_GROUPS - 1), s // tm - 1), 0

  return pl.pallas_call(
      _qkv_kernel,
      out_shape=[jax.ShapeDtypeStruct((d, s // d, ATT_QKV_WIDTH), BF16) for _, d in ATT_GROUPS],
      grid=(s // tm, N_GROUPS),
      in_specs=[pl.BlockSpec((tm, D_MODEL), x_map), w_spec],
      out_specs=[pl.BlockSpec((d, tm // d, ATT_QKV_WIDTH), lambda i, g: (0, i, 0))
                 for _, d in ATT_GROUPS],
      scratch_shapes=(
          [pltpu.VMEM((tm, D_MODEL), BF16)] * N_GROUPS
          + [pltpu.VMEM((D_MODEL // LANES, tm, LANES), F32)] * (N_GROUPS - 1)),
      compiler_params=pltpu.CompilerParams(
          dimension_semantics=("arbitrary", "arbitrary"),
          vmem_limit_bytes=VMEM_LIMIT_BYTES,
      ),
      name="qkv_projection",
  )(x2d, w_bf16)


def _attention_tile(cur, prev, bias, store_o):
  nq = ATT_BLOCK
  lane = lax.broadcasted_iota(jnp.int32, (nq, LANES), 1)
  low_half = lane < HEAD_DIM
  head_of_lane = lane % HEADS_PER_GROUP
  ones = jnp.ones((2 * nq, LANES), BF16)
  m_tile = l_tile = None

  for p in range(HEAD_PAIRS):
    q2 = cur(0, p)
    k2 = jnp.concatenate([prev(1, p), cur(1, p)], axis=0)
    v2 = jnp.concatenate([prev(2, p), cur(2, p)], axis=0)
    zero = jnp.zeros_like(q2)
    qs = jnp.concatenate([jnp.where(low_half, q2, zero), jnp.where(low_half, zero, q2)], axis=0)
    sc = lax.dot_general(qs, k2, (((1,), (1,)), ((), ())), preferred_element_type=F32)
    sc = sc + bias
    m = jnp.max(sc, axis=1, keepdims=True)
    e = jnp.exp2(sc - m).astype(BF16)
    ol = jnp.dot(e, jnp.concatenate([v2, ones], axis=1), preferred_element_type=F32)
    store_o(p, jnp.where(low_half, ol[:nq, :LANES], ol[nq:, :LANES]).astype(BF16))
    for k, rows in enumerate((slice(0, nq), slice(nq, 2 * nq))):
      m_h = jnp.broadcast_to(m[rows], (nq, LANES))
      l_h = ol[rows, LANES:]
      if m_tile is None:
        m_tile, l_tile = m_h, l_h
      else:
        own = head_of_lane == 2 * p + k
        m_tile, l_tile = jnp.where(own, m_h, m_tile), jnp.where(own, l_h, l_tile)
  return m_tile, l_tile


def _step_shape(d):
  nt = ATT_TILES_PER_STEP
  bpr = TILES_PER_SUPER // d
  blocks = min(nt, bpr)
  assert nt % blocks == 0 and bpr % blocks == 0
  return nt // blocks, blocks


def _carries_prev(d):
  return TILES_PER_SUPER // d == 1


def _step_tiles(d, sb, j):
  n_res, blocks = _step_shape(d)
  bpr = TILES_PER_SUPER // d
  spr = bpr // blocks
  b0 = (j % spr) * blocks
  res_group = j // spr
  return (res_group, sb * bpr + b0,
          (lambda t: res_group * n_res + t // blocks), (lambda t: b0 + t % blocks))


def _attention_kernel(*refs):
  refs = list(refs)
  cur_refs, prev_refs = [], []
  for _, d in ATT_GROUPS:
    cur_refs.append(refs.pop(0))
    prev_refs.append(None if _carries_prev(d) else refs.pop(0))
  o_refs = [refs.pop(0) for _ in ATT_GROUPS]
  coef_ref = refs.pop(0)
  max_ref, sum_ref, bias_ref = refs[:3]
  carry_refs = dict(zip([g for g, (_, d) in enumerate(ATT_GROUPS) if _carries_prev(d)], refs[3:]))
  sb = pl.program_id(0)
  j = pl.program_id(1)
  nq = ATT_BLOCK
  nt = ATT_TILES_PER_STEP

  @pl.when(jnp.logical_and(sb == 0, j == 0))
  def _():
    row = lax.broadcasted_iota(jnp.int32, (2 * nq, 2 * nq), 0) % nq
    col = lax.broadcasted_iota(jnp.int32, (2 * nq, 2 * nq), 1)
    for hp in range(2):
      first_col = row if hp else jnp.maximum(row, nq)
      valid = jnp.logical_and(col >= first_col, col - nq <= row)
      bias_ref[hp] = jnp.where(valid, 0.0, -jnp.inf).astype(F32)
    for carry_ref in carry_refs.values():
      carry_ref[...] = jnp.zeros(carry_ref.shape, BF16)

  def cols(sec, p):
    return pl.ds(QKV_COL_OFF[sec] + p * LANES, LANES)

  for t in range(nt):
    for g, (_, d) in enumerate(ATT_GROUPS):
      cur_ref, prev_ref, o_ref = cur_refs[g], prev_refs[g], o_refs[g]
      _, first_blk, res, loc = _step_tiles(d, sb, j)
      ri, bi = divmod(t, _step_shape(d)[1])
      tile_rows = (ri, pl.ds(bi * nq, nq))
      cur = lambda sec, p, tile_rows=tile_rows: cur_ref[tile_rows + (cols(sec, p),)]
      if bi == 0:
        if prev_ref is None:
          prev = lambda sec, p, g=g, r=res(t): carry_refs[g][r, :, cols(sec, p)]
        else:
          prev = lambda sec, p, ri=ri: prev_ref[ri, :, cols(sec, p)]
        has_prev = (first_blk > 0).astype(jnp.int32)
      else:
        prev = lambda sec, p, ri=ri, bi=bi: cur_ref[ri, (bi - 1) * nq:bi * nq, cols(sec, p)]
        has_prev = 1

      def store_o(p, o, o_ref=o_ref, tile_rows=tile_rows):
        o_ref[tile_rows + (pl.ds(p * LANES, LANES),)] = o

      m_tile, l_tile = _attention_tile(cur, prev, bias_ref[has_prev], store_o)
      r, b = res(t), loc(t)
      if d == 1:
        rows = pl.ds(pl.multiple_of(b * nq, nq), nq)
      else:
        rows = pl.ds(b * nq * d + r, nq, stride=d)
      max_ref[g, rows, :] = m_tile
      sum_ref[g, rows, :] = l_tile
      if prev_ref is None:
        carry_refs[g][r] = cur_ref[ri, :, 0:KV_WIDTH]

  @pl.when(j == TILES_PER_SUPER // nt - 1)
  def _():
    def body(c, carry):
      rows = pl.ds(pl.multiple_of(c * MERGE_ROWS, MERGE_ROWS), MERGE_ROWS)
      ms = [max_ref[g, rows, :] for g in range(N_GROUPS)]
      top = functools.reduce(jnp.maximum, ms)
      ws = [jnp.exp2(m - top) for m in ms]
      den = functools.reduce(jnp.add, [w * sum_ref[g, rows, :] for g, w in enumerate(ws)])
      group_of_lane = lax.broadcasted_iota(jnp.int32, ws[0].shape, 1) // HEADS_PER_GROUP
      w = functools.reduce(lambda acc, g: jnp.where(group_of_lane == g, ws[g], acc),
                           range(1, N_GROUPS), ws[0])
      coef_ref[rows, :] = w / den
      return carry
    lax.fori_loop(0, SUPER_BLOCK // MERGE_ROWS, body, 0, unroll=2)


def _attention(qkv_groups):
  s = qkv_groups[0].shape[1]
  nt = ATT_TILES_PER_STEP
  in_specs, args, o_specs = [], [], []
  for (_, d), arr in zip(ATT_GROUPS, qkv_groups):
    n_res, blocks = _step_shape(d)

    def cur_map(sb, j, d=d, blocks=blocks):
      res_group, first_blk, _, _ = _step_tiles(d, sb, j)
      return res_group, first_blk // blocks, 0

    def prev_map(sb, j, d=d):
      res_group, first_blk, _, _ = _step_tiles(d, sb, j)
      return res_group, jnp.maximum(first_blk - 1, 0), 0

    in_specs.append(pl.BlockSpec((n_res, blocks * ATT_BLOCK, ATT_QKV_WIDTH), cur_map))
    args.append(arr)
    if not _carries_prev(d):
      in_specs.append(pl.BlockSpec((n_res, ATT_BLOCK, KV_WIDTH), prev_map))
      args.append(arr)
    o_specs.append(pl.BlockSpec((n_res, blocks * ATT_BLOCK, GROUP_WIDTH), cur_map))
  stats = pltpu.VMEM((N_GROUPS, SUPER_BLOCK, LANES), F32)
  carries = [pltpu.VMEM((d, ATT_BLOCK, KV_WIDTH), BF16) for _, d in ATT_GROUPS if _carries_prev(d)]
  outs = pl.pallas_call(
      _attention_kernel,
      out_shape=([jax.ShapeDtypeStruct((d, s // d, GROUP_WIDTH), BF16) for _, d in ATT_GROUPS]
                 + [jax.ShapeDtypeStruct((s, LANES), F32)]),
      grid=(s // SUPER_BLOCK, TILES_PER_SUPER // nt),
      in_specs=in_specs,
      out_specs=o_specs + [pl.BlockSpec((SUPER_BLOCK, LANES), lambda sb, j: (sb, 0))],
      scratch_shapes=[stats, stats, pltpu.VMEM((2, 2 * ATT_BLOCK, 2 * ATT_BLOCK), F32)] + carries,
      compiler_params=pltpu.CompilerParams(
          dimension_semantics=("arbitrary", "arbitrary"),
          vmem_limit_bytes=VMEM_LIMIT_BYTES,
      ),
      name="dilated_attention",
  )(*args)
  return outs[:N_GROUPS], outs[N_GROUPS]


def _silu(x):
  return x * jax.nn.sigmoid(x)


def _tail_kernel(alpha, x_ref, ao1_ref, ao2_ref, ao3_ref, coef_ref,
                 wga_ref, wh_ref, wb_ref, wc_ref, wgc_ref, wg1_ref, wg2_ref,
                 cw_ref, wco_ref, wao_ref, bg_ref, wo_ref, lg_ref, lb_ref, o_ref, u_ref, perm_ref):
  tm = x_ref.shape[0]
  chunk = tm // TAIL_ROW_CHUNKS
  out_refs = (ao1_ref, ao2_ref, ao3_ref)

  def dot(a, w_ref):
    return jnp.dot(a, w_ref[...], preferred_element_type=F32)

  @pl.when(pl.program_id(0) == 0)
  def _():
    u_ref[0:CONV_HALO, :] = jnp.zeros((CONV_HALO, CONV_WIDTH), F32)
    row = lax.broadcasted_iota(jnp.int32, (chunk, chunk), 0)
    col = lax.broadcasted_iota(jnp.int32, (chunk, chunk), 1)
    for g, (_, d) in enumerate(ATT_GROUPS):
      src = (row % d) * (chunk // d) + row // d
      perm_ref[g] = jnp.where(col == src, 1.0, 0.0).astype(BF16)

  def attention_mix(lo):
    rows = pl.ds(lo, chunk)
    coef = coef_ref[rows, :]
    att = None
    for g, (_, d) in enumerate(ATT_GROUPS):
      out_ref = out_refs[g]
      n = chunk // d
      if d == 1:
        o_g = out_ref[0, rows, :].astype(F32)
      else:
        res_major = jnp.concatenate([out_ref[r, pl.ds(lo // d, n), :] for r in range(d)], axis=0)
        o_g = jnp.dot(perm_ref[g], res_major, preferred_element_type=F32)
      factor = jnp.concatenate(
          [jnp.broadcast_to(coef[:, g * HEADS_PER_GROUP + h:g * HEADS_PER_GROUP + h + 1],
                            (chunk, HEAD_DIM)) for h in range(HEADS_PER_GROUP)], axis=1)
      att = factor * o_g if att is None else att + factor * o_g
    return att

  def row_chunk(lo):
    rows = pl.ds(lo, chunk)
    xb = x_ref[rows, :].astype(BF16)
    u_ref[pl.ds(CONV_HALO + lo, chunk), :] = dot(xb, wc_ref) * dot(xb, wh_ref)
    yield
    p_b, p_gc = dot(xb, wb_ref), dot(xb, wgc_ref)
    conv = cw_ref[CONV_K - 1:CONV_K, :] * u_ref[pl.ds(CONV_HALO + lo, chunk), :]
    for k in range(CONV_K - 1):
      off = CONV_HALO + lo - (CONV_K - 1 - k)
      conv = conv + cw_ref[k:k + 1, :] * u_ref[pl.ds(off, chunk), :]
    a_conv = ((p_b * conv) * _silu(p_gc)).astype(BF16)
    yield
    y_conv = dot(a_conv, wco_ref)
    a_att = (attention_mix(lo) * _silu(dot(xb, wga_ref))).astype(BF16)
    yield
    y_att = dot(a_att, wao_ref)
    m_c = jax.nn.sigmoid(dot(xb, wg1_ref) + bg_ref[:, :D_MODEL]) * y_conv
    yield
    g_a = jax.nn.sigmoid(dot(xb, wg2_ref) + bg_ref[:, D_MODEL:])
    merged = (m_c + g_a * y_att).astype(BF16)
    yield
    y = alpha * x_ref[rows, :] + dot(merged, wo_ref)
    mu = jnp.mean(y, axis=-1, keepdims=True)
    yc = y - mu
    var = jnp.mean(yc * yc, axis=-1, keepdims=True)
    o_ref[rows, :] = yc * lax.rsqrt(var + LN_EPS) * lg_ref[...] + lb_ref[...]
    yield

  for first in range(0, TAIL_ROW_CHUNKS, TAIL_STREAMS):
    streams = [row_chunk(k * chunk) for k in range(first, first + TAIL_STREAMS)]
    for stage in range(TAIL_STAGES):
      for stream in streams:
        next(stream)
      if stage == 1 and first + TAIL_STREAMS == TAIL_ROW_CHUNKS:
        u_ref[0:CONV_HALO, :] = u_ref[tm:tm + CONV_HALO, :]


def _tail(x2d, att_outs, att_coefs, w_bf16, conv_w, w_conv_out, w_att_out, b_gate, w_o, ln_g, ln_b,
          alpha):
  s = x2d.shape[0]
  tm = TAIL_ROW_TILE
  chunk = tm // TAIL_ROW_CHUNKS
  assert all(chunk % (d * 16) == 0 for _, d in ATT_GROUPS)

  def whole(arr):
    return pl.BlockSpec(arr.shape, lambda i: (0,) * arr.ndim, pipeline_mode=pl.Buffered(1))

  def w_cols(off, width):
    assert off % width == 0
    return pl.BlockSpec((D_MODEL, width), lambda i: (0, off // width), pipeline_mode=pl.Buffered(1))

  w_specs = [w_cols(G_ATT_OFF, GROUP_WIDTH)] + [
      w_cols(off, CONV_WIDTH)
      for off in (H_OFF, B_OFF, C_OFF, G_CONV_OFF, GATE_OFF, GATE_OFF + D_MODEL)]
  others = (conv_w, w_conv_out, w_att_out, b_gate, w_o, ln_g, ln_b)
  return pl.pallas_call(
      functools.partial(_tail_kernel, alpha),
      out_shape=jax.ShapeDtypeStruct((s, D_MODEL), F32),
      grid=(s // tm,),
      in_specs=[pl.BlockSpec((tm, D_MODEL), lambda i: (i, 0))]
      + [pl.BlockSpec((d, tm // d, GROUP_WIDTH), lambda i: (0, i, 0)) for _, d in ATT_GROUPS]
      + [pl.BlockSpec((tm, LANES), lambda i: (i, 0))]
      + w_specs + [whole(w) for w in others],
      out_specs=pl.BlockSpec((tm, D_MODEL), lambda i: (i, 0)),
      scratch_shapes=[pltpu.VMEM((CONV_HALO + tm, CONV_WIDTH), F32),
                      pltpu.VMEM((N_GROUPS, chunk, chunk), BF16)],
      compiler_params=pltpu.CompilerParams(
          dimension_semantics=("arbitrary",),
          vmem_limit_bytes=VMEM_LIMIT_BYTES,
      ),
      name="conv_merge_norm",
  )(x2d, *att_outs, att_coefs, *([w_bf16] * len(w_specs)), *others)


def _layer(x2d, w_in, conv_w, w_conv_out, w_att_out, b_gate, w_o, ln_g, ln_b, alpha):
  assert w_in.shape == (D_MODEL, GATE_OFF + 2 * D_MODEL)
  w_bf16 = w_in.astype(BF16)
  qkv_groups = _qkv_projection(x2d, w_bf16)
  att_outs, att_coefs = _attention(qkv_groups)
  return _tail(x2d, att_outs, att_coefs, w_bf16, conv_w, w_conv_out.astype(BF16), w_att_out.astype(BF16),
               b_gate.reshape(1, -1), w_o.astype(BF16), ln_g.reshape(1, -1), ln_b.reshape(1, -1),
               alpha)


def kernel(x, w_in, conv_w, w_conv_out, w_att_out, b_gate, w_o, ln_g, ln_b):
  batch, seq, d_model = x.shape
  depth = w_in.shape[0]
  assert d_model == D_MODEL and seq % SUPER_BLOCK == 0
  alpha = (2.0 * depth) ** 0.25
  outs = []
  for b in range(batch):
    h = x[b]
    for layer in range(depth):
      h = _layer(h, w_in[layer], conv_w[layer], w_conv_out[layer], w_att_out[layer],
                 b_gate[layer], w_o[layer], ln_g[layer], ln_b[layer], alpha)
    outs.append(h)
  return jnp.stack(outs)
```

```python
import functools

import jax
import jax.numpy as jnp
from jax import lax
from jax.experimental import pallas as pl
from jax.experimental.pallas import tpu as pltpu

D_MODEL = 1024
HEAD_DIM = 64
HEADS_PER_GROUP = 8
ATT_GROUPS = ((128, 1), (512, 4), (2048, 16))
N_GROUPS = len(ATT_GROUPS)
GROUP_WIDTH = HEADS_PER_GROUP * HEAD_DIM
ATT_QKV_WIDTH = N_GROUPS * GROUP_WIDTH
CONV_WIDTH = D_MODEL
CONV_K = 3
LN_EPS = 1e-5
LANES = 128

ATT_BLOCK = 128
assert all(w // d == ATT_BLOCK for w, d in ATT_GROUPS)
SUPER_BLOCK = ATT_BLOCK * max(d for _, d in ATT_GROUPS)
TILES_PER_SUPER = SUPER_BLOCK // ATT_BLOCK
ATT_TILES_PER_STEP = 8

QKV_ROW_TILE = 1024
QKV_DOT_ROW_CHUNKS = 2
TAIL_ROW_TILE = 512
TAIL_ROW_CHUNKS = 2
TAIL_STREAMS = 2
assert TAIL_ROW_CHUNKS % TAIL_STREAMS == 0
TAIL_STAGES = 6
CONV_HALO = 8
MERGE_ROWS = 32
HEAD_PAIRS = HEADS_PER_GROUP // 2
assert 2 * HEAD_DIM == LANES
KV_WIDTH = 2 * GROUP_WIDTH
QKV_COL_OFF = (KV_WIDTH, 0, GROUP_WIDTH)
SCORE_SCALE = HEAD_DIM ** -0.5 * 1.4426950408889634

assert N_GROUPS * HEADS_PER_GROUP <= LANES
VMEM_LIMIT_BYTES = 56 * 1024 * 1024

F32 = jnp.float32
BF16 = jnp.bfloat16

Q_OFF, K_OFF, V_OFF = 0, ATT_QKV_WIDTH, 2 * ATT_QKV_WIDTH
G_ATT_OFF = 3 * ATT_QKV_WIDTH
H_OFF = G_ATT_OFF + GROUP_WIDTH
B_OFF = H_OFF + CONV_WIDTH
C_OFF = B_OFF + CONV_WIDTH
G_CONV_OFF = C_OFF + CONV_WIDTH
GATE_OFF = G_CONV_OFF + CONV_WIDTH


def _qkv_kernel(x_ref, w_ref, *refs):
  tm = x_ref.shape[0]
  out_refs = refs[:N_GROUPS]
  xp_ref = refs[N_GROUPS:2 * N_GROUPS]
  xl_ref = refs[2 * N_GROUPS:]
  dil = [d for _, d in ATT_GROUPS]
  assert dil[0] == 1
  scale = SCORE_SCALE
  lane_tiles = [pl.ds(c * LANES, LANES) for c in range(D_MODEL // LANES)]

  def load_rows():
    def piece(c, lanes):
      xp_ref[0][:, lanes] = x_ref[:, lanes].astype(BF16)
      xl_ref[0][c] = x_ref[:, lanes]
    return [functools.partial(piece, c, lanes) for c, lanes in enumerate(lane_tiles)]

  def permute_rows(g):
    q = dil[g + 1] // dil[g]
    assert dil[g + 1] == q * dil[g]
    n, n_next = tm // dil[g], tm // dil[g + 1]

    def piece(r, a):
      dst = pl.ds((a * dil[g] + r) * n_next, n_next)
      for c, lanes in enumerate(lane_tiles):
        rows = xl_ref[g][c, pl.ds(r * n + a, n_next, stride=q), :]
        xp_ref[g + 1][dst, lanes] = rows.astype(BF16)
        if g + 2 < N_GROUPS:
          xl_ref[g + 1][c, dst, :] = rows
    return [functools.partial(piece, r, a) for r in range(dil[g]) for a in range(q)]

  def project(g, pieces):
    n = tm // dil[g]
    row_chunk = tm // QKV_DOT_ROW_CHUNKS
    n_dots = 3 * QKV_DOT_ROW_CHUNKS
    k = 0
    for sec, off in enumerate((Q_OFF, K_OFF, V_OFF)):
      w_cols = pl.ds(off + g * GROUP_WIDTH, GROUP_WIDTH)
      for rc in range(QKV_DOT_ROW_CHUNKS):
        lo = rc * row_chunk
        res = jnp.dot(xp_ref[g][lo:lo + row_chunk, :], w_ref[:, w_cols],
                      preferred_element_type=F32)
        if sec == 0:
          res = res * scale
        res = res.astype(BF16)
        assert row_chunk % n == 0 or n % row_chunk == 0
        step = min(n, row_chunk)
        for s0 in range(0, row_chunk, step):
          r, l0 = divmod(lo + s0, n)
          out_refs[g][r, l0:l0 + step, pl.ds(QKV_COL_OFF[sec], GROUP_WIDTH)] = res[s0:s0 + step]
        k += 1
        for piece in pieces[(k - 1) * len(pieces) // n_dots:k * len(pieces) // n_dots]:
          piece()

  @pl.when(jnp.logical_and(pl.program_id(0) == 0, pl.program_id(1) == 0))
  def _():
    for piece in load_rows():
      piece()

  for g in range(N_GROUPS):
    @pl.when(pl.program_id(1) == g)
    def _(g=g):
      project(g, permute_rows(g) if g + 1 < N_GROUPS else load_rows())


def _qkv_projection(x2d, w_bf16):
  s = x2d.shape[0]
  tm = QKV_ROW_TILE

  assert (Q_OFF, K_OFF, V_OFF) == (0, ATT_QKV_WIDTH, 2 * ATT_QKV_WIDTH)
  w_spec = pl.BlockSpec((D_MODEL, 3 * ATT_QKV_WIDTH), lambda i, g: (0, 0),
                        pipeline_mode=pl.Buffered(1))

  def x_map(i, g):
    return jnp.minimum(i + g // (N_GROUPS - 1), s // tm - 1), 0

  return pl.pallas_call(
      _qkv_kernel,
      out_shape=[jax.ShapeDtypeStruct((d, s // d, ATT_QKV_WIDTH), BF16) for _, d in ATT_GROUPS],
      grid=(s // tm, N_GROUPS),
      in_specs=[pl.BlockSpec((tm, D_MODEL), x_map), w_spec],
      out_specs=[pl.BlockSpec((d, tm // d, ATT_QKV_WIDTH), lambda i, g: (0, i, 0))
                 for _, d in ATT_GROUPS],
      scratch_shapes=(
          [pltpu.VMEM((tm, D_MODEL), BF16)] * N_GROUPS
          + [pltpu.VMEM((D_MODEL // LANES, tm, LANES), F32)] * (N_GROUPS - 1)),
      compiler_params=pltpu.CompilerParams(
          dimension_semantics=("arbitrary", "arbitrary"),
          vmem_limit_bytes=VMEM_LIMIT_BYTES,
      ),
      name="qkv_projection",
  )(x2d, w_bf16)


def _attention_tile(cur, prev, bias, store_o):
  nq = ATT_BLOCK
  lane = lax.broadcasted_iota(jnp.int32, (nq, LANES), 1)
  low_half = lane < HEAD_DIM
  head_of_lane = lane % HEADS_PER_GROUP
  ones = jnp.ones((2 * nq, LANES), BF16)
  m_tile = l_tile = None

  for p in range(HEAD_PAIRS):
    q2 = cur(0, p)
    k2 = jnp.concatenate([prev(1, p), cur(1, p)], axis=0)
    v2 = jnp.concatenate([prev(2, p), cur(2, p)], axis=0)
    zero = jnp.zeros_like(q2)
    qs = jnp.concatenate([jnp.where(low_half, q2, zero), jnp.where(low_half, zero, q2)], axis=0)
    sc = lax.dot_general(qs, k2, (((1,), (1,)), ((), ())), preferred_element_type=F32)
    sc = sc + bias
    m = jnp.max(sc, axis=1, keepdims=True)
    e = jnp.exp2(sc - m).astype(BF16)
    ol = jnp.dot(e, jnp.concatenate([v2, ones], axis=1), preferred_element_type=F32)
    store_o(p, jnp.where(low_half, ol[:nq, :LANES], ol[nq:, :LANES]).astype(BF16))
    for k, rows in enumerate((slice(0, nq), slice(nq, 2 * nq))):
      m_h = jnp.broadcast_to(m[rows], (nq, LANES))
      l_h = ol[rows, LANES:]
      if m_tile is None:
        m_tile, l_tile = m_h, l_h
      else:
        own = head_of_lane == 2 * p + k
        m_tile, l_tile = jnp.where(own, m_h, m_tile), jnp.where(own, l_h, l_tile)
  return m_tile, l_tile


def _step_shape(d):
  nt = ATT_TILES_PER_STEP
  bpr = TILES_PER_SUPER // d
  blocks = min(nt, bpr)
  assert nt % blocks == 0 and bpr % blocks == 0
  return nt // blocks, blocks


def _step_tiles(d, sb, j):
  n_res, blocks = _step_shape(d)
  bpr = TILES_PER_SUPER // d
  spr = bpr // blocks
  b0 = (j % spr) * blocks
  res_group = j // spr
  return (res_group, sb * bpr + b0,
          (lambda t: res_group * n_res + t // blocks), (lambda t: b0 + t % blocks))


def _attention_kernel(*refs):
  in_refs = refs[:2 * N_GROUPS]
  o_refs = refs[2 * N_GROUPS:3 * N_GROUPS]
  coef_ref = refs[3 * N_GROUPS]
  max_ref, sum_ref, bias_ref = refs[3 * N_GROUPS + 1:]
  sb = pl.program_id(0)
  j = pl.program_id(1)
  nq = ATT_BLOCK
  nt = ATT_TILES_PER_STEP

  @pl.when(jnp.logical_and(sb == 0, j == 0))
  def _():
    row = lax.broadcasted_iota(jnp.int32, (2 * nq, 2 * nq), 0) % nq
    col = lax.broadcasted_iota(jnp.int32, (2 * nq, 2 * nq), 1)
    for hp in range(2):
      first_col = row if hp else jnp.maximum(row, nq)
      valid = jnp.logical_and(col >= first_col, col - nq <= row)
      bias_ref[hp] = jnp.where(valid, 0.0, -jnp.inf).astype(F32)

  def cols(sec, p):
    return pl.ds(QKV_COL_OFF[sec] + p * LANES, LANES)

  for t in range(nt):
    for g, (_, d) in enumerate(ATT_GROUPS):
      cur_ref, prev_ref = in_refs[2 * g:2 * g + 2]
      o_ref = o_refs[g]
      _, first_blk, res, loc = _step_tiles(d, sb, j)
      ri, bi = divmod(t, _step_shape(d)[1])
      tile_rows = (ri, pl.ds(bi * nq, nq))
      cur = lambda sec, p, tile_rows=tile_rows: cur_ref[tile_rows + (cols(sec, p),)]
      if bi == 0:
        prev = lambda sec, p, ri=ri: prev_ref[ri, :, cols(sec, p)]
        has_prev = (first_blk > 0).astype(jnp.int32)
      else:
        prev = lambda sec, p, ri=ri, bi=bi: cur_ref[ri, (bi - 1) * nq:bi * nq, cols(sec, p)]
        has_prev = 1

      def store_o(p, o, o_ref=o_ref, tile_rows=tile_rows):
        o_ref[tile_rows + (pl.ds(p * LANES, LANES),)] = o

      m_tile, l_tile = _attention_tile(cur, prev, bias_ref[has_prev], store_o)
      r, b = res(t), loc(t)
      if d == 1:
        rows = pl.ds(pl.multiple_of(b * nq, nq), nq)
      else:
        rows = pl.ds(b * nq * d + r, nq, stride=d)
      max_ref[g, rows, :] = m_tile
      sum_ref[g, rows, :] = l_tile

  @pl.when(j == TILES_PER_SUPER // nt - 1)
  def _():
    def body(c, carry):
      rows = pl.ds(pl.multiple_of(c * MERGE_ROWS, MERGE_ROWS), MERGE_ROWS)
      ms = [max_ref[g, rows, :] for g in range(N_GROUPS)]
      top = functools.reduce(jnp.maximum, ms)
      ws = [jnp.exp2(m - top) for m in ms]
      den = functools.reduce(jnp.add, [w * sum_ref[g, rows, :] for g, w in enumerate(ws)])
      group_of_lane = lax.broadcasted_iota(jnp.int32, ws[0].shape, 1) // HEADS_PER_GROUP
      w = functools.reduce(lambda acc, g: jnp.where(group_of_lane == g, ws[g], acc),
                           range(1, N_GROUPS), ws[0])
      coef_ref[rows, :] = w / den
      return carry
    lax.fori_loop(0, SUPER_BLOCK // MERGE_ROWS, body, 0, unroll=2)


def _attention(qkv_groups):
  s = qkv_groups[0].shape[1]
  nt = ATT_TILES_PER_STEP
  in_specs, args, o_specs = [], [], []
  for (_, d), arr in zip(ATT_GROUPS, qkv_groups):
    n_res, blocks = _step_shape(d)

    def cur_map(sb, j, d=d, blocks=blocks):
      res_group, first_blk, _, _ = _step_tiles(d, sb, j)
      return res_group, first_blk // blocks, 0

    def prev_map(sb, j, d=d):
      res_group, first_blk, _, _ = _step_tiles(d, sb, j)
      return res_group, jnp.maximum(first_blk - 1, 0), 0

    in_specs += [pl.BlockSpec((n_res, blocks * ATT_BLOCK, ATT_QKV_WIDTH), cur_map),
                 pl.BlockSpec((n_res, ATT_BLOCK, KV_WIDTH), prev_map)]
    o_specs.append(pl.BlockSpec((n_res, blocks * ATT_BLOCK, GROUP_WIDTH), cur_map))
    args += [arr] * 2
  stats = pltpu.VMEM((N_GROUPS, SUPER_BLOCK, LANES), F32)
  outs = pl.pallas_call(
      _attention_kernel,
      out_shape=([jax.ShapeDtypeStruct((d, s // d, GROUP_WIDTH), BF16) for _, d in ATT_GROUPS]
                 + [jax.ShapeDtypeStruct((s, LANES), F32)]),
      grid=(s // SUPER_BLOCK, TILES_PER_SUPER // nt),
      in_specs=in_specs,
      out_specs=o_specs + [pl.BlockSpec((SUPER_BLOCK, LANES), lambda sb, j: (sb, 0))],
      scratch_shapes=[stats, stats, pltpu.VMEM((2, 2 * ATT_BLOCK, 2 * ATT_BLOCK), F32)],
      compiler_params=pltpu.CompilerParams(
          dimension_semantics=("arbitrary", "arbitrary"),
          vmem_limit_bytes=VMEM_LIMIT_BYTES,
      ),
      name="dilated_attention",
  )(*args)
  return outs[:N_GROUPS], outs[N_GROUPS]


def _silu(x):
  return x * jax.nn.sigmoid(x)


def _tail_kernel(alpha, x_ref, ao1_ref, ao2_ref, ao3_ref, coef_ref,
                 wga_ref, wh_ref, wb_ref, wc_ref, wgc_ref, wg1_ref, wg2_ref,
                 cw_ref, wco32_ref, wao32_ref, bg_ref, wo32_ref, lg_ref, lb_ref, o_ref, u_ref, perm_ref,
                 wco_ref, wao_ref, wo_ref):
  tm = x_ref.shape[0]
  chunk = tm // TAIL_ROW_CHUNKS
  out_refs = (ao1_ref, ao2_ref, ao3_ref)

  def dot(a, w_ref):
    return jnp.dot(a, w_ref[...], preferred_element_type=F32)

  @pl.when(pl.program_id(0) == 0)
  def _():
    u_ref[0:CONV_HALO, :] = jnp.zeros((CONV_HALO, CONV_WIDTH), F32)
    for src_ref, dst_ref in ((wco32_ref, wco_ref), (wao32_ref, wao_ref), (wo32_ref, wo_ref)):
      dst_ref[...] = src_ref[...].astype(BF16)
    row = lax.broadcasted_iota(jnp.int32, (chunk, chunk), 0)
    col = lax.broadcasted_iota(jnp.int32, (chunk, chunk), 1)
    for g, (_, d) in enumerate(ATT_GROUPS):
      src = (row % d) * (chunk // d) + row // d
      perm_ref[g] = jnp.where(col == src, 1.0, 0.0).astype(BF16)

  def attention_mix(lo):
    rows = pl.ds(lo, chunk)
    coef = coef_ref[rows, :]
    att = None
    for g, (_, d) in enumerate(ATT_GROUPS):
      out_ref = out_refs[g]
      n = chunk // d
      if d == 1:
        o_g = out_ref[0, rows, :].astype(F32)
      else:
        res_major = jnp.concatenate([out_ref[r, pl.ds(lo // d, n), :] for r in range(d)], axis=0)
        o_g = jnp.dot(perm_ref[g], res_major, preferred_element_type=F32)
      factor = jnp.concatenate(
          [jnp.broadcast_to(coef[:, g * HEADS_PER_GROUP + h:g * HEADS_PER_GROUP + h + 1],
                            (chunk, HEAD_DIM)) for h in range(HEADS_PER_GROUP)], axis=1)
      att = factor * o_g if att is None else att + factor * o_g
    return att

  def row_chunk(lo):
    rows = pl.ds(lo, chunk)
    xb = x_ref[rows, :].astype(BF16)
    u_ref[pl.ds(CONV_HALO + lo, chunk), :] = dot(xb, wc_ref) * dot(xb, wh_ref)
    yield
    p_b, p_gc = dot(xb, wb_ref), dot(xb, wgc_ref)
    conv = cw_ref[CONV_K - 1:CONV_K, :] * u_ref[pl.ds(CONV_HALO + lo, chunk), :]
    for k in range(CONV_K - 1):
      off = CONV_HALO + lo - (CONV_K - 1 - k)
      conv = conv + cw_ref[k:k + 1, :] * u_ref[pl.ds(off, chunk), :]
    a_conv = ((p_b * conv) * _silu(p_gc)).astype(BF16)
    yield
    y_conv = dot(a_conv, wco_ref)
    a_att = (attention_mix(lo) * _silu(dot(xb, wga_ref))).astype(BF16)
    yield
    y_att = dot(a_att, wao_ref)
    m_c = jax.nn.sigmoid(dot(xb, wg1_ref) + bg_ref[:, :D_MODEL]) * y_conv
    yield
    g_a = jax.nn.sigmoid(dot(xb, wg2_ref) + bg_ref[:, D_MODEL:])
    merged = (m_c + g_a * y_att).astype(BF16)
    yield
    y = alpha * x_ref[rows, :] + dot(merged, wo_ref)
    mu = jnp.mean(y, axis=-1, keepdims=True)
    yc = y - mu
    var = jnp.mean(yc * yc, axis=-1, keepdims=True)
    o_ref[rows, :] = yc * lax.rsqrt(var + LN_EPS) * lg_ref[...] + lb_ref[...]
    yield

  for first in range(0, TAIL_ROW_CHUNKS, TAIL_STREAMS):
    streams = [row_chunk(k * chunk) for k in range(first, first + TAIL_STREAMS)]
    for stage in range(TAIL_STAGES):
      for stream in streams:
        next(stream)
      if stage == 1 and first + TAIL_STREAMS == TAIL_ROW_CHUNKS:
        u_ref[0:CONV_HALO, :] = u_ref[tm:tm + CONV_HALO, :]


def _tail(x2d, att_outs, att_coefs, w_bf16, conv_w, w_conv_out, w_att_out, b_gate, w_o, ln_g, ln_b,
          alpha):
  s = x2d.shape[0]
  tm = TAIL_ROW_TILE
  chunk = tm // TAIL_ROW_CHUNKS
  assert all(chunk % (d * 16) == 0 for _, d in ATT_GROUPS)

  def whole(arr):
    return pl.BlockSpec(arr.shape, lambda i: (0,) * arr.ndim, pipeline_mode=pl.Buffered(1))

  def w_cols(off, width):
    assert off % width == 0
    return pl.BlockSpec((D_MODEL, width), lambda i: (0, off // width), pipeline_mode=pl.Buffered(1))

  w_specs = [w_cols(G_ATT_OFF, GROUP_WIDTH)] + [
      w_cols(off, CONV_WIDTH)
      for off in (H_OFF, B_OFF, C_OFF, G_CONV_OFF, GATE_OFF, GATE_OFF + D_MODEL)]
  others = (conv_w, w_conv_out, w_att_out, b_gate, w_o, ln_g, ln_b)
  return pl.pallas_call(
      functools.partial(_tail_kernel, alpha),
      out_shape=jax.ShapeDtypeStruct((s, D_MODEL), F32),
      grid=(s // tm,),
      in_specs=[pl.BlockSpec((tm, D_MODEL), lambda i: (i, 0))]
      + [pl.BlockSpec((d, tm // d, GROUP_WIDTH), lambda i: (0, i, 0)) for _, d in ATT_GROUPS]
      + [pl.BlockSpec((tm, LANES), lambda i: (i, 0))]
      + w_specs + [whole(w) for w in others],
      out_specs=pl.BlockSpec((tm, D_MODEL), lambda i: (i, 0)),
      scratch_shapes=[pltpu.VMEM((CONV_HALO + tm, CONV_WIDTH), F32),
                      pltpu.VMEM((N_GROUPS, chunk, chunk), BF16)]
      + [pltpu.VMEM(w.shape, BF16) for w in (w_conv_out, w_att_out, w_o)],
      compiler_params=pltpu.CompilerParams(
          dimension_semantics=("arbitrary",),
          vmem_limit_bytes=VMEM_LIMIT_BYTES,
      ),
      name="conv_merge_norm",
  )(x2d, *att_outs, att_coefs, *([w_bf16] * len(w_specs)), *others)


def _layer(x2d, w_in, conv_w, w_conv_out, w_att_out, b_gate, w_o, ln_g, ln_b, alpha):
  assert w_in.shape == (D_MODEL, GATE_OFF + 2 * D_MODEL)
  w_bf16 = w_in.astype(BF16)
  qkv_groups = _qkv_projection(x2d, w_bf16)
  att_outs, att_coefs = _attention(qkv_groups)
  return _tail(x2d, att_outs, att_coefs, w_bf16, conv_w, w_conv_out, w_att_out,
               b_gate.reshape(1, -1), w_o, ln_g.reshape(1, -1), ln_b.reshape(1, -1), alpha)


def kernel(x, w_in, conv_w, w_conv_out, w_att_out, b_gate, w_o, ln_g, ln_b):
  batch, seq, d_model = x.shape
  depth = w_in.shape[0]
  assert d_model == D_MODEL and seq % SUPER_BLOCK == 0
  alpha = (2.0 * depth) ** 0.25
  outs = []
  for b in range(batch):
    h = x[b]
    for layer in range(depth):
      h = _layer(h, w_in[layer], conv_w[layer], w_conv_out[layer], w_att_out[layer],
                 b_gate[layer], w_o[layer], ln_g[layer], ln_b[layer], alpha)
    outs.append(h)
  return jnp.stack(outs)
```

```python
import functools

import jax
import jax.numpy as jnp
from jax import lax
from jax.experimental import pallas as pl
from jax.experimental.pallas import tpu as pltpu

D_MODEL = 1024
HEAD_DIM = 64
HEADS_PER_GROUP = 8
ATT_GROUPS = ((128, 1), (512, 4), (2048, 16))
N_GROUPS = len(ATT_GROUPS)
GROUP_WIDTH = HEADS_PER_GROUP * HEAD_DIM
ATT_QKV_WIDTH = N_GROUPS * GROUP_WIDTH
CONV_WIDTH = D_MODEL
CONV_K = 3
LN_EPS = 1e-5
LANES = 128

ATT_BLOCK = 128
assert all(w // d == ATT_BLOCK for w, d in ATT_GROUPS)
SUPER_BLOCK = ATT_BLOCK * max(d for _, d in ATT_GROUPS)
TILES_PER_SUPER = SUPER_BLOCK // ATT_BLOCK
ATT_TILES_PER_STEP = 8

QKV_ROW_TILE = 1024
QKV_DOT_ROW_CHUNKS = 2
TAIL_ROW_TILE = 512
TAIL_ROW_CHUNKS = 2
TAIL_STREAMS = 2
assert TAIL_ROW_CHUNKS % TAIL_STREAMS == 0
TAIL_STAGES = 6
CONV_HALO = 8
MERGE_ROWS = 32
HEAD_PAIRS = HEADS_PER_GROUP // 2
assert 2 * HEAD_DIM == LANES
KV_WIDTH = 2 * GROUP_WIDTH
QKV_COL_OFF = (KV_WIDTH, 0, GROUP_WIDTH)
SCORE_SCALE = HEAD_DIM ** -0.5 * 1.4426950408889634

assert N_GROUPS * HEADS_PER_GROUP <= LANES
VMEM_LIMIT_BYTES = 56 * 1024 * 1024

F32 = jnp.float32
BF16 = jnp.bfloat16

Q_OFF, K_OFF, V_OFF = 0, ATT_QKV_WIDTH, 2 * ATT_QKV_WIDTH
G_ATT_OFF = 3 * ATT_QKV_WIDTH
H_OFF = G_ATT_OFF + GROUP_WIDTH
B_OFF = H_OFF + CONV_WIDTH
C_OFF = B_OFF + CONV_WIDTH
G_CONV_OFF = C_OFF + CONV_WIDTH
GATE_OFF = G_CONV_OFF + CONV_WIDTH


def _qkv_kernel(x_ref, w_ref, *refs):
  tm = x_ref.shape[0]
  out_refs = refs[:N_GROUPS]
  xp_ref = refs[N_GROUPS:2 * N_GROUPS]
  xl_ref = refs[2 * N_GROUPS:]
  dil = [d for _, d in ATT_GROUPS]
  assert dil[0] == 1
  scale = SCORE_SCALE
  lane_tiles = [pl.ds(c * LANES, LANES) for c in range(D_MODEL // LANES)]

  def load_rows():
    def piece(c, lanes):
      xp_ref[0][:, lanes] = x_ref[:, lanes].astype(BF16)
      xl_ref[0][c] = x_ref[:, lanes]
    return [functools.partial(piece, c, lanes) for c, lanes in enumerate(lane_tiles)]

  def permute_rows(g):
    q = dil[g + 1] // dil[g]
    assert dil[g + 1] == q * dil[g]
    n, n_next = tm // dil[g], tm // dil[g + 1]

    def piece(r, a):
      dst = pl.ds((a * dil[g] + r) * n_next, n_next)
      for c, lanes in enumerate(lane_tiles):
        rows = xl_ref[g][c, pl.ds(r * n + a, n_next, stride=q), :]
        xp_ref[g + 1][dst, lanes] = rows.astype(BF16)
        if g + 2 < N_GROUPS:
          xl_ref[g + 1][c, dst, :] = rows
    return [functools.partial(piece, r, a) for r in range(dil[g]) for a in range(q)]

  def project(g, pieces):
    n = tm // dil[g]
    row_chunk = tm // QKV_DOT_ROW_CHUNKS
    n_dots = 3 * QKV_DOT_ROW_CHUNKS
    k = 0
    for sec, off in enumerate((Q_OFF, K_OFF, V_OFF)):
      w_cols = pl.ds(off + g * GROUP_WIDTH, GROUP_WIDTH)
      for rc in range(QKV_DOT_ROW_CHUNKS):
        lo = rc * row_chunk
        res = jnp.dot(xp_ref[g][lo:lo + row_chunk, :], w_ref[:, w_cols],
                      preferred_element_type=F32)
        if sec == 0:
          res = res * scale
        res = res.astype(BF16)
        assert row_chunk % n == 0 or n % row_chunk == 0
        step = min(n, row_chunk)
        for s0 in range(0, row_chunk, step):
          r, l0 = divmod(lo + s0, n)
          out_refs[g][r, l0:l0 + step, pl.ds(QKV_COL_OFF[sec], GROUP_WIDTH)] = res[s0:s0 + step]
        k += 1
        for piece in pieces[(k - 1) * len(pieces) // n_dots:k * len(pieces) // n_dots]:
          piece()

  @pl.when(jnp.logical_and(pl.program_id(0) == 0, pl.program_id(1) == 0))
  def _():
    for piece in load_rows():
      piece()

  for g in range(N_GROUPS):
    @pl.when(pl.program_id(1) == g)
    def _(g=g):
      project(g, permute_rows(g) if g + 1 < N_GROUPS else load_rows())


def _qkv_projection(x2d, w_qkv):
  s = x2d.shape[0]
  tm = QKV_ROW_TILE

  assert (Q_OFF, K_OFF, V_OFF) == (0, ATT_QKV_WIDTH, 2 * ATT_QKV_WIDTH)
  assert w_qkv.shape == (D_MODEL, 3 * ATT_QKV_WIDTH)
  w_spec = pl.BlockSpec(w_qkv.shape, lambda i, g: (0, 0), pipeline_mode=pl.Buffered(1))

  def x_map(i, g):
    return jnp.minimum(i + g // (N_GROUPS - 1), s // tm - 1), 0

  return pl.pallas_call(
      _qkv_kernel,
      out_shape=[jax.ShapeDtypeStruct((d, s // d, ATT_QKV_WIDTH), BF16) for _, d in ATT_GROUPS],
      grid=(s // tm, N_GROUPS),
      in_specs=[pl.BlockSpec((tm, D_MODEL), x_map), w_spec],
      out_specs=[pl.BlockSpec((d, tm // d, ATT_QKV_WIDTH), lambda i, g: (0, i, 0))
                 for _, d in ATT_GROUPS],
      scratch_shapes=(
          [pltpu.VMEM((tm, D_MODEL), BF16)] * N_GROUPS
          + [pltpu.VMEM((D_MODEL // LANES, tm, LANES), F32)] * (N_GROUPS - 1)),
      compiler_params=pltpu.CompilerParams(
          dimension_semantics=("arbitrary", "arbitrary"),
          vmem_limit_bytes=VMEM_LIMIT_BYTES,
      ),
      name="qkv_projection",
  )(x2d, w_qkv)


def _attention_tile(cur, prev, bias, store_o):
  nq = ATT_BLOCK
  lane = lax.broadcasted_iota(jnp.int32, (nq, LANES), 1)
  low_half = lane < HEAD_DIM
  head_of_lane = lane % HEADS_PER_GROUP
  ones = jnp.ones((2 * nq, LANES), BF16)
  m_tile = l_tile = None

  for p in range(HEAD_PAIRS):
    q2 = cur(0, p)
    k2 = jnp.concatenate([prev(1, p), cur(1, p)], axis=0)
    v2 = jnp.concatenate([prev(2, p), cur(2, p)], axis=0)
    zero = jnp.zeros_like(q2)
    qs = jnp.concatenate([jnp.where(low_half, q2, zero), jnp.where(low_half, zero, q2)], axis=0)
    sc = lax.dot_general(qs, k2, (((1,), (1,)), ((), ())), preferred_element_type=F32)
    sc = sc + bias
    m = jnp.max(sc, axis=1, keepdims=True)
    e = jnp.exp2(sc - m).astype(BF16)
    ol = jnp.dot(e, jnp.concatenate([v2, ones], axis=1), preferred_element_type=F32)
    store_o(p, jnp.where(low_half, ol[:nq, :LANES], ol[nq:, :LANES]).astype(BF16))
    for k, rows in enumerate((slice(0, nq), slice(nq, 2 * nq))):
      m_h = jnp.broadcast_to(m[rows], (nq, LANES))
      l_h = ol[rows, LANES:]
      if m_tile is None:
        m_tile, l_tile = m_h, l_h
      else:
        own = head_of_lane == 2 * p + k
        m_tile, l_tile = jnp.where(own, m_h, m_tile), jnp.where(own, l_h, l_tile)
  return m_tile, l_tile


def _step_shape(d):
  nt = ATT_TILES_PER_STEP
  bpr = TILES_PER_SUPER // d
  blocks = min(nt, bpr)
  assert nt % blocks == 0 and bpr % blocks == 0
  return nt // blocks, blocks


def _step_tiles(d, sb, j):
  n_res, blocks = _step_shape(d)
  bpr = TILES_PER_SUPER // d
  spr = bpr // blocks
  b0 = (j % spr) * blocks
  res_group = j // spr
  return (res_group, sb * bpr + b0,
          (lambda t: res_group * n_res + t // blocks), (lambda t: b0 + t % blocks))


def _attention_kernel(*refs):
  in_refs = refs[:2 * N_GROUPS]
  o_refs = refs[2 * N_GROUPS:3 * N_GROUPS]
  coef_ref = refs[3 * N_GROUPS]
  max_ref, sum_ref, bias_ref = refs[3 * N_GROUPS + 1:]
  sb = pl.program_id(0)
  j = pl.program_id(1)
  nq = ATT_BLOCK
  nt = ATT_TILES_PER_STEP

  @pl.when(jnp.logical_and(sb == 0, j == 0))
  def _():
    row = lax.broadcasted_iota(jnp.int32, (2 * nq, 2 * nq), 0) % nq
    col = lax.broadcasted_iota(jnp.int32, (2 * nq, 2 * nq), 1)
    for hp in range(2):
      first_col = row if hp else jnp.maximum(row, nq)
      valid = jnp.logical_and(col >= first_col, col - nq <= row)
      bias_ref[hp] = jnp.where(valid, 0.0, -jnp.inf).astype(F32)

  def cols(sec, p):
    return pl.ds(QKV_COL_OFF[sec] + p * LANES, LANES)

  for t in range(nt):
    for g, (_, d) in enumerate(ATT_GROUPS):
      cur_ref, prev_ref = in_refs[2 * g:2 * g + 2]
      o_ref = o_refs[g]
      _, first_blk, res, loc = _step_tiles(d, sb, j)
      ri, bi = divmod(t, _step_shape(d)[1])
      tile_rows = (ri, pl.ds(bi * nq, nq))
      cur = lambda sec, p, tile_rows=tile_rows: cur_ref[tile_rows + (cols(sec, p),)]
      if bi == 0:
        prev = lambda sec, p, ri=ri: prev_ref[ri, :, cols(sec, p)]
        has_prev = (first_blk > 0).astype(jnp.int32)
      else:
        prev = lambda sec, p, ri=ri, bi=bi: cur_ref[ri, (bi - 1) * nq:bi * nq, cols(sec, p)]
        has_prev = 1

      def store_o(p, o, o_ref=o_ref, tile_rows=tile_rows):
        o_ref[tile_rows + (pl.ds(p * LANES, LANES),)] = o

      m_tile, l_tile = _attention_tile(cur, prev, bias_ref[has_prev], store_o)
      r, b = res(t), loc(t)
      if d == 1:
        rows = pl.ds(pl.multiple_of(b * nq, nq), nq)
      else:
        rows = pl.ds(b * nq * d + r, nq, stride=d)
      max_ref[g, rows, :] = m_tile
      sum_ref[g, rows, :] = l_tile

  @pl.when(j == TILES_PER_SUPER // nt - 1)
  def _():
    def body(c, carry):
      rows = pl.ds(pl.multiple_of(c * MERGE_ROWS, MERGE_ROWS), MERGE_ROWS)
      ms = [max_ref[g, rows, :] for g in range(N_GROUPS)]
      top = functools.reduce(jnp.maximum, ms)
      ws = [jnp.exp2(m - top) for m in ms]
      den = functools.reduce(jnp.add, [w * sum_ref[g, rows, :] for g, w in enumerate(ws)])
      group_of_lane = lax.broadcasted_iota(jnp.int32, ws[0].shape, 1) // HEADS_PER_GROUP
      w = functools.reduce(lambda acc, g: jnp.where(group_of_lane == g, ws[g], acc),
                           range(1, N_GROUPS), ws[0])
      coef_ref[rows, :] = w / den
      return carry
    lax.fori_loop(0, SUPER_BLOCK // MERGE_ROWS, body, 0, unroll=2)


def _attention(qkv_groups):
  s = qkv_groups[0].shape[1]
  nt = ATT_TILES_PER_STEP
  in_specs, args, o_specs = [], [], []
  for (_, d), arr in zip(ATT_GROUPS, qkv_groups):
    n_res, blocks = _step_shape(d)

    def cur_map(sb, j, d=d, blocks=blocks):
      res_group, first_blk, _, _ = _step_tiles(d, sb, j)
      return res_group, first_blk // blocks, 0

    def prev_map(sb, j, d=d):
      res_group, first_blk, _, _ = _step_tiles(d, sb, j)
      return res_group, jnp.maximum(first_blk - 1, 0), 0

    in_specs += [pl.BlockSpec((n_res, blocks * ATT_BLOCK, ATT_QKV_WIDTH), cur_map),
                 pl.BlockSpec((n_res, ATT_BLOCK, KV_WIDTH), prev_map)]
    o_specs.append(pl.BlockSpec((n_res, blocks * ATT_BLOCK, GROUP_WIDTH), cur_map))
    args += [arr] * 2
  stats = pltpu.VMEM((N_GROUPS, SUPER_BLOCK, LANES), F32)
  outs = pl.pallas_call(
      _attention_kernel,
      out_shape=([jax.ShapeDtypeStruct((d, s // d, GROUP_WIDTH), BF16) for _, d in ATT_GROUPS]
                 + [jax.ShapeDtypeStruct((s, LANES), F32)]),
      grid=(s // SUPER_BLOCK, TILES_PER_SUPER // nt),
      in_specs=in_specs,
      out_specs=o_specs + [pl.BlockSpec((SUPER_BLOCK, LANES), lambda sb, j: (sb, 0))],
      scratch_shapes=[stats, stats, pltpu.VMEM((2, 2 * ATT_BLOCK, 2 * ATT_BLOCK), F32)],
      compiler_params=pltpu.CompilerParams(
          dimension_semantics=("arbitrary", "arbitrary"),
          vmem_limit_bytes=VMEM_LIMIT_BYTES,
      ),
      name="dilated_attention",
  )(*args)
  return outs[:N_GROUPS], outs[N_GROUPS]


def _silu(x):
  return x * jax.nn.sigmoid(x)


def _tail_kernel(alpha, x_ref, ao1_ref, ao2_ref, ao3_ref, coef_ref, w_ref,
                 cw_ref, wco32_ref, wao32_ref, bg_ref, wo32_ref, lg_ref, lb_ref, o_ref, u_ref, perm_ref,
                 wco_ref, wao_ref, wo_ref):
  tm = x_ref.shape[0]
  chunk = tm // TAIL_ROW_CHUNKS
  out_refs = (ao1_ref, ao2_ref, ao3_ref)

  def dot(a, w_ref):
    return jnp.dot(a, w_ref[...], preferred_element_type=F32)

  def proj(xb, off, width=CONV_WIDTH):
    return jnp.dot(xb, w_ref[:, pl.ds(off - G_ATT_OFF, width)], preferred_element_type=F32)

  @pl.when(pl.program_id(0) == 0)
  def _():
    u_ref[0:CONV_HALO, :] = jnp.zeros((CONV_HALO, CONV_WIDTH), F32)
    for src_ref, dst_ref in ((wco32_ref, wco_ref), (wao32_ref, wao_ref), (wo32_ref, wo_ref)):
      dst_ref[...] = src_ref[...].astype(BF16)
    row = lax.broadcasted_iota(jnp.int32, (chunk, chunk), 0)
    col = lax.broadcasted_iota(jnp.int32, (chunk, chunk), 1)
    for g, (_, d) in enumerate(ATT_GROUPS):
      src = (row % d) * (chunk // d) + row // d
      perm_ref[g] = jnp.where(col == src, 1.0, 0.0).astype(BF16)

  def attention_mix(lo):
    rows = pl.ds(lo, chunk)
    coef = coef_ref[rows, :]
    att = None
    for g, (_, d) in enumerate(ATT_GROUPS):
      out_ref = out_refs[g]
      n = chunk // d
      if d == 1:
        o_g = out_ref[0, rows, :].astype(F32)
      else:
        res_major = jnp.concatenate([out_ref[r, pl.ds(lo // d, n), :] for r in range(d)], axis=0)
        o_g = jnp.dot(perm_ref[g], res_major, preferred_element_type=F32)
      factor = jnp.concatenate(
          [jnp.broadcast_to(coef[:, g * HEADS_PER_GROUP + h:g * HEADS_PER_GROUP + h + 1],
                            (chunk, HEAD_DIM)) for h in range(HEADS_PER_GROUP)], axis=1)
      att = factor * o_g if att is None else att + factor * o_g
    return att

  def row_chunk(lo):
    rows = pl.ds(lo, chunk)
    xb = x_ref[rows, :].astype(BF16)
    u_ref[pl.ds(CONV_HALO + lo, chunk), :] = proj(xb, C_OFF) * proj(xb, H_OFF)
    yield
    p_b, p_gc = proj(xb, B_OFF), proj(xb, G_CONV_OFF)
    conv = cw_ref[CONV_K - 1:CONV_K, :] * u_ref[pl.ds(CONV_HALO + lo, chunk), :]
    for k in range(CONV_K - 1):
      off = CONV_HALO + lo - (CONV_K - 1 - k)
      conv = conv + cw_ref[k:k + 1, :] * u_ref[pl.ds(off, chunk), :]
    a_conv = ((p_b * conv) * _silu(p_gc)).astype(BF16)
    yield
    y_conv = dot(a_conv, wco_ref)
    a_att = (attention_mix(lo) * _silu(proj(xb, G_ATT_OFF, GROUP_WIDTH))).astype(BF16)
    yield
    y_att = dot(a_att, wao_ref)
    m_c = jax.nn.sigmoid(proj(xb, GATE_OFF) + bg_ref[:, :D_MODEL]) * y_conv
    yield
    g_a = jax.nn.sigmoid(proj(xb, GATE_OFF + D_MODEL) + bg_ref[:, D_MODEL:])
    merged = (m_c + g_a * y_att).astype(BF16)
    yield
    y = alpha * x_ref[rows, :] + dot(merged, wo_ref)
    mu = jnp.mean(y, axis=-1, keepdims=True)
    yc = y - mu
    var = jnp.mean(yc * yc, axis=-1, keepdims=True)
    o_ref[rows, :] = yc * lax.rsqrt(var + LN_EPS) * lg_ref[...] + lb_ref[...]
    yield

  for first in range(0, TAIL_ROW_CHUNKS, TAIL_STREAMS):
    streams = [row_chunk(k * chunk) for k in range(first, first + TAIL_STREAMS)]
    for stage in range(TAIL_STAGES):
      for stream in streams:
        next(stream)
      if stage == 1 and first + TAIL_STREAMS == TAIL_ROW_CHUNKS:
        u_ref[0:CONV_HALO, :] = u_ref[tm:tm + CONV_HALO, :]


def _tail(x2d, att_outs, att_coefs, w_rest, conv_w, w_conv_out, w_att_out, b_gate, w_o, ln_g, ln_b,
          alpha):
  s = x2d.shape[0]
  tm = TAIL_ROW_TILE
  chunk = tm // TAIL_ROW_CHUNKS
  assert all(chunk % (d * 16) == 0 for _, d in ATT_GROUPS)

  def whole(arr):
    return pl.BlockSpec(arr.shape, lambda i: (0,) * arr.ndim, pipeline_mode=pl.Buffered(1))

  others = (w_rest, conv_w, w_conv_out, w_att_out, b_gate, w_o, ln_g, ln_b)
  return pl.pallas_call(
      functools.partial(_tail_kernel, alpha),
      out_shape=jax.ShapeDtypeStruct((s, D_MODEL), F32),
      grid=(s // tm,),
      in_specs=[pl.BlockSpec((tm, D_MODEL), lambda i: (i, 0))]
      + [pl.BlockSpec((d, tm // d, GROUP_WIDTH), lambda i: (0, i, 0)) for _, d in ATT_GROUPS]
      + [pl.BlockSpec((tm, LANES), lambda i: (i, 0))]
      + [whole(w) for w in others],
      out_specs=pl.BlockSpec((tm, D_MODEL), lambda i: (i, 0)),
      scratch_shapes=[pltpu.VMEM((CONV_HALO + tm, CONV_WIDTH), F32),
                      pltpu.VMEM((N_GROUPS, chunk, chunk), BF16)]
      + [pltpu.VMEM(w.shape, BF16) for w in (w_conv_out, w_att_out, w_o)],
      compiler_params=pltpu.CompilerParams(
          dimension_semantics=("arbitrary",),
          vmem_limit_bytes=VMEM_LIMIT_BYTES,
      ),
      name="conv_merge_norm",
  )(x2d, *att_outs, att_coefs, *others)


def _layer(x2d, w_in, conv_w, w_conv_out, w_att_out, b_gate, w_o, ln_g, ln_b, alpha):
  assert w_in.shape == (D_MODEL, GATE_OFF + 2 * D_MODEL)
  w_qkv = w_in[:, :G_ATT_OFF].astype(BF16)
  w_rest = w_in[:, G_ATT_OFF:].astype(BF16)
  qkv_groups = _qkv_projection(x2d, w_qkv)
  att_outs, att_coefs = _attention(qkv_groups)
  return _tail(x2d, att_outs, att_coefs, w_rest, conv_w, w_conv_out, w_att_out,
               b_gate.reshape(1, -1), w_o, ln_g.reshape(1, -1), ln_b.reshape(1, -1), alpha)


def kernel(x, w_in, conv_w, w_conv_out, w_att_out, b_gate, w_o, ln_g, ln_b):
  batch, seq, d_model = x.shape
  depth = w_in.shape[0]
  assert d_model == D_MODEL and seq % SUPER_BLOCK == 0
  alpha = (2.0 * depth) ** 0.25
  outs = []
  for b in range(batch):
    h = x[b]
    for layer in range(depth):
      h = _layer(h, w_in[layer], conv_w[layer], w_conv_out[layer], w_att_out[layer],
                 b_gate[layer], w_o[layer], ln_g[layer], ln_b[layer], alpha)
    outs.append(h)
  return jnp.stack(outs)
```

```python
import functools

import jax
import jax.numpy as jnp
from jax import lax
from jax.experimental import pallas as pl
from jax.experimental.pallas import tpu as pltpu

D_MODEL = 1024
HEAD_DIM = 64
HEADS_PER_GROUP = 8
ATT_GROUPS = ((128, 1), (512, 4), (2048, 16))
N_GROUPS = len(ATT_GROUPS)
GROUP_WIDTH = HEADS_PER_GROUP * HEAD_DIM
ATT_QKV_WIDTH = N_GROUPS * GROUP_WIDTH
CONV_WIDTH = D_MODEL
CONV_K = 3
LN_EPS = 1e-5
LANES = 128

ATT_BLOCK = 128
assert all(w // d == ATT_BLOCK for w, d in ATT_GROUPS)
SUPER_BLOCK = ATT_BLOCK * max(d for _, d in ATT_GROUPS)
TILES_PER_SUPER = SUPER_BLOCK // ATT_BLOCK
ATT_TILES_PER_STEP = 8

QKV_ROW_TILE = 1024
QKV_DOT_ROW_CHUNKS = 2
TAIL_ROW_TILE = 512
TAIL_ROW_CHUNKS = 2
TAIL_STREAMS = 2
assert TAIL_ROW_CHUNKS % TAIL_STREAMS == 0
TAIL_STAGES = 6
CONV_HALO = 8
MERGE_ROWS = 32
HEAD_PAIRS = HEADS_PER_GROUP // 2
assert 2 * HEAD_DIM == LANES
KV_WIDTH = 2 * GROUP_WIDTH
QKV_COL_OFF = (KV_WIDTH, 0, GROUP_WIDTH)
SCORE_SCALE = HEAD_DIM ** -0.5 * 1.4426950408889634

assert N_GROUPS * HEADS_PER_GROUP <= LANES
VMEM_LIMIT_BYTES = 56 * 1024 * 1024

F32 = jnp.float32
BF16 = jnp.bfloat16

Q_OFF, K_OFF, V_OFF = 0, ATT_QKV_WIDTH, 2 * ATT_QKV_WIDTH
G_ATT_OFF = 3 * ATT_QKV_WIDTH
H_OFF = G_ATT_OFF + GROUP_WIDTH
B_OFF = H_OFF + CONV_WIDTH
C_OFF = B_OFF + CONV_WIDTH
G_CONV_OFF = C_OFF + CONV_WIDTH
GATE_OFF = G_CONV_OFF + CONV_WIDTH


def _qkv_kernel(x_ref, w_ref, *refs):
  tm = x_ref.shape[0]
  out_refs = refs[:N_GROUPS]
  xp_ref = refs[N_GROUPS:2 * N_GROUPS]
  xl_ref = refs[2 * N_GROUPS:]
  dil = [d for _, d in ATT_GROUPS]
  assert dil[0] == 1
  scale = SCORE_SCALE
  lane_tiles = [pl.ds(c * LANES, LANES) for c in range(D_MODEL // LANES)]

  def load_rows():
    def piece(c, lanes):
      xp_ref[0][:, lanes] = x_ref[:, lanes].astype(BF16)
      xl_ref[0][c] = x_ref[:, lanes]
    return [functools.partial(piece, c, lanes) for c, lanes in enumerate(lane_tiles)]

  def permute_rows(g):
    q = dil[g + 1] // dil[g]
    assert dil[g + 1] == q * dil[g]
    n, n_next = tm // dil[g], tm // dil[g + 1]

    def piece(r, a):
      dst = pl.ds((a * dil[g] + r) * n_next, n_next)
      for c, lanes in enumerate(lane_tiles):
        rows = xl_ref[g][c, pl.ds(r * n + a, n_next, stride=q), :]
        xp_ref[g + 1][dst, lanes] = rows.astype(BF16)
        if g + 2 < N_GROUPS:
          xl_ref[g + 1][c, dst, :] = rows
    return [functools.partial(piece, r, a) for r in range(dil[g]) for a in range(q)]

  def project(g, pieces):
    n = tm // dil[g]
    row_chunk = tm // QKV_DOT_ROW_CHUNKS
    n_dots = 3 * QKV_DOT_ROW_CHUNKS
    k = 0
    for sec, off in enumerate((Q_OFF, K_OFF, V_OFF)):
      w_cols = pl.ds(off + g * GROUP_WIDTH, GROUP_WIDTH)
      for rc in range(QKV_DOT_ROW_CHUNKS):
        lo = rc * row_chunk
        res = jnp.dot(xp_ref[g][lo:lo + row_chunk, :], w_ref[:, w_cols],
                      preferred_element_type=F32)
        if sec == 0:
          res = res * scale
        res = res.astype(BF16)
        assert row_chunk % n == 0 or n % row_chunk == 0
        step = min(n, row_chunk)
        for s0 in range(0, row_chunk, step):
          r, l0 = divmod(lo + s0, n)
          out_refs[g][r, l0:l0 + step, pl.ds(QKV_COL_OFF[sec], GROUP_WIDTH)] = res[s0:s0 + step]
        k += 1
        for piece in pieces[(k - 1) * len(pieces) // n_dots:k * len(pieces) // n_dots]:
          piece()

  @pl.when(jnp.logical_and(pl.program_id(0) == 0, pl.program_id(1) == 0))
  def _():
    for piece in load_rows():
      piece()

  for g in range(N_GROUPS):
    @pl.when(pl.program_id(1) == g)
    def _(g=g):
      project(g, permute_rows(g) if g + 1 < N_GROUPS else load_rows())


def _qkv_projection(x2d, w_bf16):
  s = x2d.shape[0]
  tm = QKV_ROW_TILE

  assert (Q_OFF, K_OFF, V_OFF) == (0, ATT_QKV_WIDTH, 2 * ATT_QKV_WIDTH)
  w_spec = pl.BlockSpec((D_MODEL, 3 * ATT_QKV_WIDTH), lambda i, g: (0, 0),
                        pipeline_mode=pl.Buffered(1))

  def x_map(i, g):
    return jnp.minimum(i + g // (N_GROUPS - 1), s // tm - 1), 0

  return pl.pallas_call(
      _qkv_kernel,
      out_shape=[jax.ShapeDtypeStruct((d, s // d, ATT_QKV_WIDTH), BF16) for _, d in ATT_GROUPS],
      grid=(s // tm, N_GROUPS),
      in_specs=[pl.BlockSpec((tm, D_MODEL), x_map), w_spec],
      out_specs=[pl.BlockSpec((d, tm // d, ATT_QKV_WIDTH), lambda i, g: (0, i, 0))
                 for _, d in ATT_GROUPS],
      scratch_shapes=(
          [pltpu.VMEM((tm, D_MODEL), BF16)] * N_GROUPS
          + [pltpu.VMEM((D_MODEL // LANES, tm, LANES), F32)] * (N_GROUPS - 1)),
      compiler_params=pltpu.CompilerParams(
          dimension_semantics=("arbitrary", "arbitrary"),
          vmem_limit_bytes=VMEM_LIMIT_BYTES,
      ),
      name="qkv_projection",
  )(x2d, w_bf16)


def _attention_tile(cur, prev, bias, store_o):
  nq = ATT_BLOCK
  lane = lax.broadcasted_iota(jnp.int32, (nq, LANES), 1)
  low_half = lane < HEAD_DIM
  head_of_lane = lane % HEADS_PER_GROUP
  ones = jnp.ones((2 * nq, LANES), BF16)
  m_tile = l_tile = None

  for p in range(HEAD_PAIRS):
    q2 = cur(0, p)
    k2 = jnp.concatenate([prev(1, p), cur(1, p)], axis=0)
    v2 = jnp.concatenate([prev(2, p), cur(2, p)], axis=0)
    zero = jnp.zeros_like(q2)
    qs = jnp.concatenate([jnp.where(low_half, q2, zero), jnp.where(low_half, zero, q2)], axis=0)
    sc = lax.dot_general(qs, k2, (((1,), (1,)), ((), ())), preferred_element_type=F32)
    sc = sc + bias
    m = jnp.max(sc, axis=1, keepdims=True)
    e = jnp.exp2(sc - m).astype(BF16)
    ol = jnp.dot(e, jnp.concatenate([v2, ones], axis=1), preferred_element_type=F32)
    store_o(p, jnp.where(low_half, ol[:nq, :LANES], ol[nq:, :LANES]).astype(BF16))
    for k, rows in enumerate((slice(0, nq), slice(nq, 2 * nq))):
      m_h = jnp.broadcast_to(m[rows], (nq, LANES))
      l_h = ol[rows, LANES:]
      if m_tile is None:
        m_tile, l_tile = m_h, l_h
      else:
        own = head_of_lane == 2 * p + k
        m_tile, l_tile = jnp.where(own, m_h, m_tile), jnp.where(own, l_h, l_tile)
  return m_tile, l_tile


def _step_shape(d):
  nt = ATT_TILES_PER_STEP
  bpr = TILES_PER_SUPER // d
  blocks = min(nt, bpr)
  assert nt % blocks == 0 and bpr % blocks == 0
  return nt // blocks, blocks


def _step_tiles(d, sb, j):
  n_res, blocks = _step_shape(d)
  bpr = TILES_PER_SUPER // d
  spr = bpr // blocks
  b0 = (j % spr) * blocks
  res_group = j // spr
  return (res_group, sb * bpr + b0,
          (lambda t: res_group * n_res + t // blocks), (lambda t: b0 + t % blocks))


def _attention_kernel(*refs):
  in_refs = refs[:2 * N_GROUPS]
  o_refs = refs[2 * N_GROUPS:3 * N_GROUPS]
  coef_ref = refs[3 * N_GROUPS]
  max_ref, sum_ref, bias_ref = refs[3 * N_GROUPS + 1:]
  sb = pl.program_id(0)
  j = pl.program_id(1)
  nq = ATT_BLOCK
  nt = ATT_TILES_PER_STEP

  @pl.when(jnp.logical_and(sb == 0, j == 0))
  def _():
    row = lax.broadcasted_iota(jnp.int32, (2 * nq, 2 * nq), 0) % nq
    col = lax.broadcasted_iota(jnp.int32, (2 * nq, 2 * nq), 1)
    for hp in range(2):
      first_col = row if hp else jnp.maximum(row, nq)
      valid = jnp.logical_and(col >= first_col, col - nq <= row)
      bias_ref[hp] = jnp.where(valid, 0.0, -jnp.inf).astype(F32)

  def cols(sec, p):
    return pl.ds(QKV_COL_OFF[sec] + p * LANES, LANES)

  for t in range(nt):
    for g, (_, d) in enumerate(ATT_GROUPS):
      cur_ref, prev_ref = in_refs[2 * g:2 * g + 2]
      o_ref = o_refs[g]
      _, first_blk, res, loc = _step_tiles(d, sb, j)
      ri, bi = divmod(t, _step_shape(d)[1])
      tile_rows = (ri, pl.ds(bi * nq, nq))
      cur = lambda sec, p, tile_rows=tile_rows: cur_ref[tile_rows + (cols(sec, p),)]
      if bi == 0:
        prev = lambda sec, p, ri=ri: prev_ref[ri, :, cols(sec, p)]
        has_prev = (first_blk > 0).astype(jnp.int32)
      else:
        prev = lambda sec, p, ri=ri, bi=bi: cur_ref[ri, (bi - 1) * nq:bi * nq, cols(sec, p)]
        has_prev = 1

      def store_o(p, o, o_ref=o_ref, tile_rows=tile_rows):
        o_ref[tile_rows + (pl.ds(p * LANES, LANES),)] = o

      m_tile, l_tile = _attention_tile(cur, prev, bias_ref[has_prev], store_o)
      r, b = res(t), loc(t)
      if d == 1:
        rows = pl.ds(pl.multiple_of(b * nq, nq), nq)
      else:
        rows = pl.ds(b * nq * d + r, nq, stride=d)
      max_ref[g, rows, :] = m_tile
      sum_ref[g, rows, :] = l_tile

  @pl.when(j == TILES_PER_SUPER // nt - 1)
  def _():
    def body(c, carry):
      rows = pl.ds(pl.multiple_of(c * MERGE_ROWS, MERGE_ROWS), MERGE_ROWS)
      ms = [max_ref[g, rows, :] for g in range(N_GROUPS)]
      top = functools.reduce(jnp.maximum, ms)
      ws = [jnp.exp2(m - top) for m in ms]
      den = functools.reduce(jnp.add, [w * sum_ref[g, rows, :] for g, w in enumerate(ws)])
      group_of_lane = lax.broadcasted_iota(jnp.int32, ws[0].shape, 1) // HEADS_PER_GROUP
      w = functools.reduce(lambda acc, g: jnp.where(group_of_lane == g, ws[g], acc),
                           range(1, N_GROUPS), ws[0])
      coef_ref[rows, :] = w / den
      return carry
    lax.fori_loop(0, SUPER_BLOCK // MERGE_ROWS, body, 0, unroll=2)


def _attention(qkv_groups):
  s = qkv_groups[0].shape[1]
  nt = ATT_TILES_PER_STEP
  in_specs, args, o_specs = [], [], []
  for (_, d), arr in zip(ATT_GROUPS, qkv_groups):
    n_res, blocks = _step_shape(d)

    def cur_map(sb, j, d=d, blocks=blocks):
      res_group, first_blk, _, _ = _step_tiles(d, sb, j)
      return res_group, first_blk // blocks, 0

    def prev_map(sb, j, d=d):
      res_group, first_blk, _, _ = _step_tiles(d, sb, j)
      return res_group, jnp.maximum(first_blk - 1, 0), 0

    in_specs += [pl.BlockSpec((n_res, blocks * ATT_BLOCK, ATT_QKV_WIDTH), cur_map),
                 pl.BlockSpec((n_res, ATT_BLOCK, KV_WIDTH), prev_map)]
    o_specs.append(pl.BlockSpec((n_res, blocks * ATT_BLOCK, GROUP_WIDTH), cur_map))
    args += [arr] * 2
  stats = pltpu.VMEM((N_GROUPS, SUPER_BLOCK, LANES), F32)
  outs = pl.pallas_call(
      _attention_kernel,
      out_shape=([jax.ShapeDtypeStruct((d, s // d, GROUP_WIDTH), BF16) for _, d in ATT_GROUPS]
                 + [jax.ShapeDtypeStruct((s, LANES), F32)]),
      grid=(s // SUPER_BLOCK, TILES_PER_SUPER // nt),
      in_specs=in_specs,
      out_specs=o_specs + [pl.BlockSpec((SUPER_BLOCK, LANES), lambda sb, j: (sb, 0))],
      scratch_shapes=[stats, stats, pltpu.VMEM((2, 2 * ATT_BLOCK, 2 * ATT_BLOCK), F32)],
      compiler_params=pltpu.CompilerParams(
          dimension_semantics=("arbitrary", "arbitrary"),
          vmem_limit_bytes=VMEM_LIMIT_BYTES,
      ),
      name="dilated_attention",
  )(*args)
  return outs[:N_GROUPS], outs[N_GROUPS]


def _silu(x):
  return x * jax.nn.sigmoid(x)


def _tail_kernel(alpha, x_ref, ao1_ref, ao2_ref, ao3_ref, coef_ref,
                 wga_ref, wh_ref, wb_ref, wc_ref, wgc_ref, wg1_ref, wg2_ref,
                 cw_ref, wco32_ref, wao32_ref, bg_ref, wo32_ref, lg_ref, lb_ref, o_ref, u_ref, perm_ref,
                 wco_ref, wao_ref, wo_ref):
  tm = x_ref.shape[0]
  chunk = tm // TAIL_ROW_CHUNKS
  out_refs = (ao1_ref, ao2_ref, ao3_ref)

  def dot(a, w_ref):
    return jnp.dot(a, w_ref[...], preferred_element_type=F32)

  @pl.when(pl.program_id(0) == 0)
  def _():
    u_ref[0:CONV_HALO, :] = jnp.zeros((CONV_HALO, CONV_WIDTH), F32)
    for src_ref, dst_ref in ((wco32_ref, wco_ref), (wao32_ref, wao_ref), (wo32_ref, wo_ref)):
      dst_ref[...] = src_ref[...].astype(BF16)
    row = lax.broadcasted_iota(jnp.int32, (chunk, chunk), 0)
    col = lax.broadcasted_iota(jnp.int32, (chunk, chunk), 1)
    for g, (_, d) in enumerate(ATT_GROUPS):
      src = (row % d) * (chunk // d) + row // d
      perm_ref[g] = jnp.where(col == src, 1.0, 0.0).astype(BF16)

  def attention_mix(lo):
    rows = pl.ds(lo, chunk)
    coef = coef_ref[rows, :]
    att = None
    for g, (_, d) in enumerate(ATT_GROUPS):
      out_ref = out_refs[g]
      n = chunk // d
      if d == 1:
        o_g = out_ref[0, rows, :].astype(F32)
      else:
        res_major = jnp.concatenate([out_ref[r, pl.ds(lo // d, n), :] for r in range(d)], axis=0)
        o_g = jnp.dot(perm_ref[g], res_major, preferred_element_type=F32)
      factor = jnp.concatenate(
          [jnp.broadcast_to(coef[:, g * HEADS_PER_GROUP + h:g * HEADS_PER_GROUP + h + 1],
                            (chunk, HEAD_DIM)) for h in range(HEADS_PER_GROUP)], axis=1)
      att = factor * o_g if att is None else att + factor * o_g
    return att

  def row_chunk(lo):
    rows = pl.ds(lo, chunk)
    xb = x_ref[rows, :].astype(BF16)
    u_ref[pl.ds(CONV_HALO + lo, chunk), :] = dot(xb, wc_ref) * dot(xb, wh_ref)
    yield
    p_b, p_gc = dot(xb, wb_ref), dot(xb, wgc_ref)
    conv = cw_ref[CONV_K - 1:CONV_K, :] * u_ref[pl.ds(CONV_HALO + lo, chunk), :]
    for k in range(CONV_K - 1):
      off = CONV_HALO + lo - (CONV_K - 1 - k)
      conv = conv + cw_ref[k:k + 1, :] * u_ref[pl.ds(off, chunk), :]
    a_conv = ((p_b * conv) * _silu(p_gc)).astype(BF16)
    yield
    y_conv = dot(a_conv, wco_ref)
    a_att = (attention_mix(lo) * _silu(dot(xb, wga_ref))).astype(BF16)
    yield
    y_att = dot(a_att, wao_ref)
    m_c = jax.nn.sigmoid(dot(xb, wg1_ref) + bg_ref[:, :D_MODEL]) * y_conv
    yield
    g_a = jax.nn.sigmoid(dot(xb, wg2_ref) + bg_ref[:, D_MODEL:])
    merged = (m_c + g_a * y_att).astype(BF16)
    yield
    y = alpha * x_ref[rows, :] + dot(merged, wo_ref)
    mu = jnp.mean(y, axis=-1, keepdims=True)
    yc = y - mu
    var = jnp.mean(yc * yc, axis=-1, keepdims=True)
    o_ref[rows, :] = yc * lax.rsqrt(var + LN_EPS) * lg_ref[...] + lb_ref[...]
    yield

  for first in range(0, TAIL_ROW_CHUNKS, TAIL_STREAMS):
    streams = [row_chunk(k * chunk) for k in range(first, first + TAIL_STREAMS)]
    for stage in range(TAIL_STAGES):
      for stream in streams:
        next(stream)
      if stage == 1 and first + TAIL_STREAMS == TAIL_ROW_CHUNKS:
        u_ref[0:CONV_HALO, :] = u_ref[tm:tm + CONV_HALO, :]


def _tail(x2d, att_outs, att_coefs, w_bf16, conv_w, w_conv_out, w_att_out, b_gate, w_o, ln_g, ln_b,
          alpha):
  s = x2d.shape[0]
  tm = TAIL_ROW_TILE
  chunk = tm // TAIL_ROW_CHUNKS
  assert all(chunk % (d * 16) == 0 for _, d in ATT_GROUPS)

  def whole(arr):
    return pl.BlockSpec(arr.shape, lambda i: (0,) * arr.ndim, pipeline_mode=pl.Buffered(1))

  def w_cols(off, width):
    assert off % width == 0
    return pl.BlockSpec((D_MODEL, width), lambda i: (0, off // width), pipeline_mode=pl.Buffered(1))

  w_specs = [w_cols(G_ATT_OFF, GROUP_WIDTH)] + [
      w_cols(off, CONV_WIDTH)
      for off in (H_OFF, B_OFF, C_OFF, G_CONV_OFF, GATE_OFF, GATE_OFF + D_MODEL)]
  others = (conv_w, w_conv_out, w_att_out, b_gate, w_o, ln_g, ln_b)
  return pl.pallas_call(
      functools.partial(_tail_kernel, alpha),
      out_shape=jax.ShapeDtypeStruct((s, D_MODEL), F32),
      grid=(s // tm,),
      in_specs=[pl.BlockSpec((tm, D_MODEL), lambda i: (i, 0))]
      + [pl.BlockSpec((d, tm // d, GROUP_WIDTH), lambda i: (0, i, 0)) for _, d in ATT_GROUPS]
      + [pl.BlockSpec((tm, LANES), lambda i: (i, 0))]
      + w_specs + [whole(w) for w in others],
      out_specs=pl.BlockSpec((tm, D_MODEL), lambda i: (i, 0)),
      scratch_shapes=[pltpu.VMEM((CONV_HALO + tm, CONV_WIDTH), F32),
                      pltpu.VMEM((N_GROUPS, chunk, chunk), BF16)]
      + [pltpu.VMEM(w.shape, BF16) for w in (w_conv_out, w_att_out, w_o)],
      compiler_params=pltpu.CompilerParams(
          dimension_semantics=("arbitrary",),
          vmem_limit_bytes=VMEM_LIMIT_BYTES,
      ),
      name="conv_merge_norm",
  )(x2d, *att_outs, att_coefs, *([w_bf16] * len(w_specs)), *others)


def _layer(x2d, w_in, conv_w, w_conv_out, w_att_out, b_gate, w_o, ln_g, ln_b, alpha):
  assert w_in.shape == (D_MODEL, GATE_OFF + 2 * D_MODEL)
  w_bf16 = w_in.astype(BF16)
  qkv_groups = _qkv_projection(x2d, w_bf16)
  att_outs, att_coefs = _attention(qkv_groups)
  return _tail(x2d, att_outs, att_coefs, w_bf16, conv_w, w_conv_out, w_att_out,
               b_gate.reshape(1, -1), w_o, ln_g.reshape(1, -1), ln_b.reshape(1, -1), alpha)


def kernel(x, w_in, conv_w, w_conv_out, w_att_out, b_gate, w_o, ln_g, ln_b):
  batch, seq, d_model = x.shape
  depth = w_in.shape[0]
  assert d_model == D_MODEL and seq % SUPER_BLOCK == 0
  alpha = (2.0 * depth) ** 0.25
  outs = []
  for b in range(batch):
    h = x[b]
    for layer in range(depth):
      h = _layer(h, w_in[layer], conv_w[layer], w_conv_out[layer], w_att_out[layer],
                 b_gate[layer], w_o[layer], ln_g[layer], ln_b[layer], alpha)
    outs.append(h)
  return jnp.stack(outs)
```

```python
import functools

import jax
import jax.numpy as jnp
from jax import lax
from jax.experimental import pallas as pl
from jax.experimental.pallas import tpu as pltpu

D_MODEL = 1024
HEAD_DIM = 64
HEADS_PER_GROUP = 8
ATT_GROUPS = ((128, 1), (512, 4), (2048, 16))
N_GROUPS = len(ATT_GROUPS)
GROUP_WIDTH = HEADS_PER_GROUP * HEAD_DIM
ATT_QKV_WIDTH = N_GROUPS * GROUP_WIDTH
CONV_WIDTH = D_MODEL
CONV_K = 3
LN_EPS = 1e-5
LANES = 128
BF16_SUBLANES = 16

ATT_BLOCK = 128
assert all(w // d == ATT_BLOCK for w, d in ATT_GROUPS)
SUPER_BLOCK = ATT_BLOCK * max(d for _, d in ATT_GROUPS)
TILES_PER_SUPER = SUPER_BLOCK // ATT_BLOCK
ATT_TILES_PER_STEP = 8

QKV_ROW_TILE = 1024
QKV_DOT_ROW_CHUNKS = 2
TAIL_ROW_TILE = 512
TAIL_ROW_CHUNKS = 2
TAIL_STREAMS = 2
assert TAIL_ROW_CHUNKS % TAIL_STREAMS == 0
TAIL_STAGES = 6
CONV_HALO = 8
MERGE_ROWS = 32
HEAD_PAIRS = HEADS_PER_GROUP // 2
assert 2 * HEAD_DIM == LANES
KV_WIDTH = 2 * GROUP_WIDTH
QKV_COL_OFF = (KV_WIDTH, 0, GROUP_WIDTH)
SCORE_SCALE = HEAD_DIM ** -0.5 * 1.4426950408889634

assert N_GROUPS * HEADS_PER_GROUP <= LANES
VMEM_LIMIT_BYTES = 56 * 1024 * 1024

F32 = jnp.float32
BF16 = jnp.bfloat16

Q_OFF, K_OFF, V_OFF = 0, ATT_QKV_WIDTH, 2 * ATT_QKV_WIDTH
G_ATT_OFF = 3 * ATT_QKV_WIDTH
H_OFF = G_ATT_OFF + GROUP_WIDTH
B_OFF = H_OFF + CONV_WIDTH
C_OFF = B_OFF + CONV_WIDTH
G_CONV_OFF = C_OFF + CONV_WIDTH
GATE_OFF = G_CONV_OFF + CONV_WIDTH


def _qkv_kernel(x_ref, w_ref, *refs):
  tm = x_ref.shape[0]
  out_refs = refs[:N_GROUPS]
  xp_ref = refs[N_GROUPS:2 * N_GROUPS]
  xl_ref = refs[2 * N_GROUPS:]
  dil = [d for _, d in ATT_GROUPS]
  assert dil[0] == 1
  scale = SCORE_SCALE
  lane_tiles = [pl.ds(c * LANES, LANES) for c in range(D_MODEL // LANES)]

  def load_rows():
    def piece(c, lanes):
      xp_ref[0][:, lanes] = x_ref[:, lanes].astype(BF16)
      xl_ref[0][c] = x_ref[:, lanes]
    return [functools.partial(piece, c, lanes) for c, lanes in enumerate(lane_tiles)]

  def permute_rows(g):
    q = dil[g + 1] // dil[g]
    assert dil[g + 1] == q * dil[g]
    n, n_next = tm // dil[g], tm // dil[g + 1]

    def piece(r, a):
      dst = pl.ds((a * dil[g] + r) * n_next, n_next)
      for c, lanes in enumerate(lane_tiles):
        rows = xl_ref[g][c, pl.ds(r * n + a, n_next, stride=q), :]
        xp_ref[g + 1][dst, lanes] = rows.astype(BF16)
        if g + 2 < N_GROUPS:
          xl_ref[g + 1][c, dst, :] = rows
    return [functools.partial(piece, r, a) for r in range(dil[g]) for a in range(q)]

  def project(g, pieces):
    n = tm // dil[g]
    row_chunk = tm // QKV_DOT_ROW_CHUNKS
    n_dots = 3 * QKV_DOT_ROW_CHUNKS
    k = 0
    for sec, off in enumerate((Q_OFF, K_OFF, V_OFF)):
      w_cols = pl.ds(off + g * GROUP_WIDTH, GROUP_WIDTH)
      for rc in range(QKV_DOT_ROW_CHUNKS):
        lo = rc * row_chunk
        res = jnp.dot(xp_ref[g][lo:lo + row_chunk, :], w_ref[:, w_cols],
                      preferred_element_type=F32)
        if sec == 0:
          res = res * scale
        res = res.astype(BF16)
        assert row_chunk % n == 0 or n % row_chunk == 0
        step = min(n, row_chunk)
        for s0 in range(0, row_chunk, step):
          r, l0 = divmod(lo + s0, n)
          out_refs[g][r, l0:l0 + step, pl.ds(QKV_COL_OFF[sec], GROUP_WIDTH)] = res[s0:s0 + step]
        k += 1
        for piece in pieces[(k - 1) * len(pieces) // n_dots:k * len(pieces) // n_dots]:
          piece()

  @pl.when(jnp.logical_and(pl.program_id(0) == 0, pl.program_id(1) == 0))
  def _():
    for piece in load_rows():
      piece()

  for g in range(N_GROUPS):
    @pl.when(pl.program_id(1) == g)
    def _(g=g):
      project(g, permute_rows(g) if g + 1 < N_GROUPS else load_rows())


def _qkv_projection(x2d, w_bf16):
  s = x2d.shape[0]
  tm = QKV_ROW_TILE

  assert (Q_OFF, K_OFF, V_OFF) == (0, ATT_QKV_WIDTH, 2 * ATT_QKV_WIDTH)
  w_spec = pl.BlockSpec((D_MODEL, 3 * ATT_QKV_WIDTH), lambda i, g: (0, 0),
                        pipeline_mode=pl.Buffered(1))

  def x_map(i, g):
    return jnp.minimum(i + g // (N_GROUPS - 1), s // tm - 1), 0

  return pl.pallas_call(
      _qkv_kernel,
      out_shape=[jax.ShapeDtypeStruct((d, s // d, ATT_QKV_WIDTH), BF16) for _, d in ATT_GROUPS],
      grid=(s // tm, N_GROUPS),
      in_specs=[pl.BlockSpec((tm, D_MODEL), x_map), w_spec],
      out_specs=[pl.BlockSpec((d, tm // d, ATT_QKV_WIDTH), lambda i, g: (0, i, 0))
                 for _, d in ATT_GROUPS],
      scratch_shapes=(
          [pltpu.VMEM((tm, D_MODEL), BF16)] * N_GROUPS
          + [pltpu.VMEM((D_MODEL // LANES, tm, LANES), F32)] * (N_GROUPS - 1)),
      compiler_params=pltpu.CompilerParams(
          dimension_semantics=("arbitrary", "arbitrary"),
          vmem_limit_bytes=VMEM_LIMIT_BYTES,
      ),
      name="qkv_projection",
  )(x2d, w_bf16)


def _attention_tile(cur, prev, bias, store_o):
  nq = ATT_BLOCK
  lane = lax.broadcasted_iota(jnp.int32, (nq, LANES), 1)
  low_half = lane < HEAD_DIM
  head_of_lane = lane % HEADS_PER_GROUP
  ones = jnp.ones((2 * nq, LANES), BF16)
  m_tile = l_tile = None

  for p in range(HEAD_PAIRS):
    q2 = cur(0, p)
    k2 = jnp.concatenate([prev(1, p), cur(1, p)], axis=0)
    v2 = jnp.concatenate([prev(2, p), cur(2, p)], axis=0)
    zero = jnp.zeros_like(q2)
    qs = jnp.concatenate([jnp.where(low_half, q2, zero), jnp.where(low_half, zero, q2)], axis=0)
    sc = lax.dot_general(qs, k2, (((1,), (1,)), ((), ())), preferred_element_type=F32)
    sc = sc + bias
    m = jnp.max(sc, axis=1, keepdims=True)
    e = jnp.exp2(sc - m).astype(BF16)
    ol = jnp.dot(e, jnp.concatenate([v2, ones], axis=1), preferred_element_type=F32)
    store_o(p, jnp.where(low_half, ol[:nq, :LANES], ol[nq:, :LANES]).astype(BF16))
    for k, rows in enumerate((slice(0, nq), slice(nq, 2 * nq))):
      m_h = jnp.broadcast_to(m[rows], (nq, LANES))
      l_h = ol[rows, LANES:]
      if m_tile is None:
        m_tile, l_tile = m_h, l_h
      else:
        own = head_of_lane == 2 * p + k
        m_tile, l_tile = jnp.where(own, m_h, m_tile), jnp.where(own, l_h, l_tile)
  return m_tile, l_tile


def _step_shape(d):
  nt = ATT_TILES_PER_STEP
  bpr = TILES_PER_SUPER // d
  blocks = min(nt, bpr)
  assert nt % blocks == 0 and bpr % blocks == 0
  return nt // blocks, blocks


def _step_tiles(d, sb, j):
  n_res, blocks = _step_shape(d)
  bpr = TILES_PER_SUPER // d
  spr = bpr // blocks
  b0 = (j % spr) * blocks
  res_group = j // spr
  return (res_group, sb * bpr + b0,
          (lambda t: res_group * n_res + t // blocks), (lambda t: b0 + t % blocks))


def _attention_kernel(*refs):
  in_refs = refs[:2 * N_GROUPS]
  o_refs = refs[2 * N_GROUPS:3 * N_GROUPS]
  coef_ref = refs[3 * N_GROUPS]
  max_ref, sum_ref, bias_ref = refs[3 * N_GROUPS + 1:]
  sb = pl.program_id(0)
  j = pl.program_id(1)
  nq = ATT_BLOCK
  nt = ATT_TILES_PER_STEP

  @pl.when(jnp.logical_and(sb == 0, j == 0))
  def _():
    row = lax.broadcasted_iota(jnp.int32, (2 * nq, 2 * nq), 0) % nq
    col = lax.broadcasted_iota(jnp.int32, (2 * nq, 2 * nq), 1)
    for hp in range(2):
      first_col = row if hp else jnp.maximum(row, nq)
      valid = jnp.logical_and(col >= first_col, col - nq <= row)
      bias_ref[hp] = jnp.where(valid, 0.0, -jnp.inf).astype(F32)

  def cols(sec, p):
    return pl.ds(QKV_COL_OFF[sec] + p * LANES, LANES)

  for t in range(nt):
    for g, (_, d) in enumerate(ATT_GROUPS):
      cur_ref, prev_ref = in_refs[2 * g:2 * g + 2]
      o_ref = o_refs[g]
      _, first_blk, res, loc = _step_tiles(d, sb, j)
      ri, bi = divmod(t, _step_shape(d)[1])
      tile_rows = (ri, pl.ds(bi * nq, nq))
      cur = lambda sec, p, tile_rows=tile_rows: cur_ref[tile_rows + (cols(sec, p),)]
      if bi == 0:
        prev = lambda sec, p, ri=ri: prev_ref[ri, :, cols(sec, p)]
        has_prev = (first_blk > 0).astype(jnp.int32)
      else:
        prev = lambda sec, p, ri=ri, bi=bi: cur_ref[ri, (bi - 1) * nq:bi * nq, cols(sec, p)]
        has_prev = 1

      def store_o(p, o, o_ref=o_ref, tile_rows=tile_rows):
        o_ref[tile_rows + (pl.ds(p * LANES, LANES),)] = o

      m_tile, l_tile = _attention_tile(cur, prev, bias_ref[has_prev], store_o)
      r, b = res(t), loc(t)
      if d == 1:
        rows = pl.ds(pl.multiple_of(b * nq, nq), nq)
      else:
        rows = pl.ds(b * nq * d + r, nq, stride=d)
      max_ref[g, rows, :] = m_tile
      sum_ref[g, rows, :] = l_tile

  @pl.when(j == TILES_PER_SUPER // nt - 1)
  def _():
    def body(c, carry):
      rows = pl.ds(pl.multiple_of(c * MERGE_ROWS, MERGE_ROWS), MERGE_ROWS)
      ms = [max_ref[g, rows, :] for g in range(N_GROUPS)]
      top = functools.reduce(jnp.maximum, ms)
      ws = [jnp.exp2(m - top) for m in ms]
      den = functools.reduce(jnp.add, [w * sum_ref[g, rows, :] for g, w in enumerate(ws)])
      group_of_lane = lax.broadcasted_iota(jnp.int32, ws[0].shape, 1) // HEADS_PER_GROUP
      w = functools.reduce(lambda acc, g: jnp.where(group_of_lane == g, ws[g], acc),
                           range(1, N_GROUPS), ws[0])
      coef_ref[rows, :] = w / den
      return carry
    lax.fori_loop(0, SUPER_BLOCK // MERGE_ROWS, body, 0, unroll=2)


def _attention(qkv_groups):
  s = qkv_groups[0].shape[1]
  nt = ATT_TILES_PER_STEP
  in_specs, args, o_specs = [], [], []
  for (_, d), arr in zip(ATT_GROUPS, qkv_groups):
    n_res, blocks = _step_shape(d)

    def cur_map(sb, j, d=d, blocks=blocks):
      res_group, first_blk, _, _ = _step_tiles(d, sb, j)
      return res_group, first_blk // blocks, 0

    def prev_map(sb, j, d=d):
      res_group, first_blk, _, _ = _step_tiles(d, sb, j)
      return res_group, jnp.maximum(first_blk - 1, 0), 0

    in_specs += [pl.BlockSpec((n_res, blocks * ATT_BLOCK, ATT_QKV_WIDTH), cur_map),
                 pl.BlockSpec((n_res, ATT_BLOCK, KV_WIDTH), prev_map)]
    o_specs.append(pl.BlockSpec((n_res, blocks * ATT_BLOCK, GROUP_WIDTH), cur_map))
    args += [arr] * 2
  stats = pltpu.VMEM((N_GROUPS, SUPER_BLOCK, LANES), F32)
  outs = pl.pallas_call(
      _attention_kernel,
      out_shape=([jax.ShapeDtypeStruct((d, s // d, GROUP_WIDTH), BF16) for _, d in ATT_GROUPS]
                 + [jax.ShapeDtypeStruct((s, LANES), F32)]),
      grid=(s // SUPER_BLOCK, TILES_PER_SUPER // nt),
      in_specs=in_specs,
      out_specs=o_specs + [pl.BlockSpec((SUPER_BLOCK, LANES), lambda sb, j: (sb, 0))],
      scratch_shapes=[stats, stats, pltpu.VMEM((2, 2 * ATT_BLOCK, 2 * ATT_BLOCK), F32)],
      compiler_params=pltpu.CompilerParams(
          dimension_semantics=("arbitrary", "arbitrary"),
          vmem_limit_bytes=VMEM_LIMIT_BYTES,
      ),
      name="dilated_attention",
  )(*args)
  return outs[:N_GROUPS], outs[N_GROUPS]


def _silu(x):
  return x * jax.nn.sigmoid(x)


def _tail_kernel(alpha, x_ref, ao1_ref, ao2_ref, ao3_ref, coef_ref,
                 wga_ref, wh_ref, wb_ref, wc_ref, wgc_ref, wg1_ref, wg2_ref,
                 cw_ref, wco32_ref, wao32_ref, bg_ref, wo32_ref, lg_ref, lb_ref, o_ref, u_ref, perm_ref,
                 wco_ref, wao_ref, wo_ref):
  tm = x_ref.shape[0]
  chunk = tm // TAIL_ROW_CHUNKS
  out_refs = (ao1_ref, ao2_ref, ao3_ref)

  def dot(a, w_ref):
    return jnp.dot(a, w_ref[...], preferred_element_type=F32)

  @pl.when(pl.program_id(0) == 0)
  def _():
    u_ref[0:CONV_HALO, :] = jnp.zeros((CONV_HALO, CONV_WIDTH), F32)
    for src_ref, dst_ref in ((wco32_ref, wco_ref), (wao32_ref, wao_ref), (wo32_ref, wo_ref)):
      dst_ref[...] = src_ref[...].astype(BF16)
    row = lax.broadcasted_iota(jnp.int32, (chunk, chunk), 0)
    col = lax.broadcasted_iota(jnp.int32, (chunk, chunk), 1)
    for g, (_, d) in enumerate(ATT_GROUPS):
      src = (row % d) * (chunk // d) + row // d
      perm_ref[g] = jnp.where(col == src, 1.0, 0.0).astype(BF16)

  def attention_mix(lo):
    rows = pl.ds(lo, chunk)
    coef = coef_ref[rows, :]
    att = None
    for g, (_, d) in enumerate(ATT_GROUPS):
      out_ref = out_refs[g]
      n = chunk // d
      if d == 1:
        o_g = out_ref[0, rows, :].astype(F32)
      else:
        res_major = jnp.concatenate([out_ref[r, pl.ds(lo // d, n), :] for r in range(d)], axis=0)
        o_g = jnp.dot(perm_ref[g], res_major, preferred_element_type=F32)
      factor = jnp.concatenate(
          [jnp.broadcast_to(coef[:, g * HEADS_PER_GROUP + h:g * HEADS_PER_GROUP + h + 1],
                            (chunk, HEAD_DIM)) for h in range(HEADS_PER_GROUP)], axis=1)
      att = factor * o_g if att is None else att + factor * o_g
    return att

  def row_chunk(lo):
    rows = pl.ds(lo, chunk)
    xb = x_ref[rows, :].astype(BF16)
    u_ref[pl.ds(CONV_HALO + lo, chunk), :] = dot(xb, wc_ref) * dot(xb, wh_ref)
    att = attention_mix(lo)
    yield
    p_b, p_gc = dot(xb, wb_ref), dot(xb, wgc_ref)
    conv = cw_ref[CONV_K - 1:CONV_K, :] * u_ref[pl.ds(CONV_HALO + lo, chunk), :]
    for k in range(CONV_K - 1):
      off = CONV_HALO + lo - (CONV_K - 1 - k)
      conv = conv + cw_ref[k:k + 1, :] * u_ref[pl.ds(off, chunk), :]
    a_conv = ((p_b * conv) * _silu(p_gc)).astype(BF16)
    yield
    y_conv = dot(a_conv, wco_ref)
    a_att = (att * _silu(dot(xb, wga_ref))).astype(BF16)
    yield
    y_att = dot(a_att, wao_ref)
    m_c = jax.nn.sigmoid(dot(xb, wg1_ref) + bg_ref[:, :D_MODEL]) * y_conv
    yield
    g_a = jax.nn.sigmoid(dot(xb, wg2_ref) + bg_ref[:, D_MODEL:])
    merged = (m_c + g_a * y_att).astype(BF16)
    yield
    y = alpha * x_ref[rows, :] + dot(merged, wo_ref)
    mu = jnp.mean(y, axis=-1, keepdims=True)
    yc = y - mu
    var = jnp.mean(yc * yc, axis=-1, keepdims=True)
    o_ref[rows, :] = yc * lax.rsqrt(var + LN_EPS) * lg_ref[...] + lb_ref[...]
    yield

  for first in range(0, TAIL_ROW_CHUNKS, TAIL_STREAMS):
    streams = [row_chunk(k * chunk) for k in range(first, first + TAIL_STREAMS)]
    for stage in range(TAIL_STAGES):
      for stream in streams:
        next(stream)
      if stage == 1 and first + TAIL_STREAMS == TAIL_ROW_CHUNKS:
        u_ref[0:CONV_HALO, :] = u_ref[tm:tm + CONV_HALO, :]


def _tail(x2d, att_outs, att_coefs, w_bf16, conv_w, w_conv_out, w_att_out, b_gate, w_o, ln_g, ln_b,
          alpha):
  s = x2d.shape[0]
  tm = TAIL_ROW_TILE
  chunk = tm // TAIL_ROW_CHUNKS
  assert all(chunk % (d * BF16_SUBLANES) == 0 for _, d in ATT_GROUPS)

  def whole(arr):
    return pl.BlockSpec(arr.shape, lambda i: (0,) * arr.ndim, pipeline_mode=pl.Buffered(1))

  def w_cols(off, width):
    assert off % width == 0
    return pl.BlockSpec((D_MODEL, width), lambda i: (0, off // width), pipeline_mode=pl.Buffered(1))

  w_specs = [w_cols(G_ATT_OFF, GROUP_WIDTH)] + [
      w_cols(off, CONV_WIDTH)
      for off in (H_OFF, B_OFF, C_OFF, G_CONV_OFF, GATE_OFF, GATE_OFF + D_MODEL)]
  others = (conv_w, w_conv_out, w_att_out, b_gate, w_o, ln_g, ln_b)
  return pl.pallas_call(
      functools.partial(_tail_kernel, alpha),
      out_shape=jax.ShapeDtypeStruct((s, D_MODEL), F32),
      grid=(s // tm,),
      in_specs=[pl.BlockSpec((tm, D_MODEL), lambda i: (i, 0))]
      + [pl.BlockSpec((d, tm // d, GROUP_WIDTH), lambda i: (0, i, 0)) for _, d in ATT_GROUPS]
      + [pl.BlockSpec((tm, LANES), lambda i: (i, 0))]
      + w_specs + [whole(w) for w in others],
      out_specs=pl.BlockSpec((tm, D_MODEL), lambda i: (i, 0)),
      scratch_shapes=[pltpu.VMEM((CONV_HALO + tm, CONV_WIDTH), F32),
                      pltpu.VMEM((N_GROUPS, chunk, chunk), BF16)]
      + [pltpu.VMEM(w.shape, BF16) for w in (w_conv_out, w_att_out, w_o)],
      compiler_params=pltpu.CompilerParams(
          dimension_semantics=("arbitrary",),
          vmem_limit_bytes=VMEM_LIMIT_BYTES,
      ),
      name="conv_merge_norm",
  )(x2d, *att_outs, att_coefs, *([w_bf16] * len(w_specs)), *others)


def _layer(x2d, w_in, conv_w, w_conv_out, w_att_out, b_gate, w_o, ln_g, ln_b, alpha):
  assert w_in.shape == (D_MODEL, GATE_OFF + 2 * D_MODEL)
  w_bf16 = w_in.astype(BF16)
  qkv_groups = _qkv_projection(x2d, w_bf16)
  att_outs, att_coefs = _attention(qkv_groups)
  return _tail(x2d, att_outs, att_coefs, w_bf16, conv_w, w_conv_out, w_att_out,
               b_gate.reshape(1, -1), w_o, ln_g.reshape(1, -1), ln_b.reshape(1, -1), alpha)


def kernel(x, w_in, conv_w, w_conv_out, w_att_out, b_gate, w_o, ln_g, ln_b):
  batch, seq, d_model = x.shape
  depth = w_in.shape[0]
  assert d_model == D_MODEL and seq % SUPER_BLOCK == 0
  alpha = (2.0 * depth) ** 0.25
  outs = []
  for b in range(batch):
    h = x[b]
    for layer in range(depth):
      h = _layer(h, w_in[layer], conv_w[layer], w_conv_out[layer], w_att_out[layer],
                 b_gate[layer], w_o[layer], ln_g[layer], ln_b[layer], alpha)
    outs.append(h)
  return jnp.stack(outs)
```

```python
import functools

import jax
import jax.numpy as jnp
from jax import lax
from jax.experimental import pallas as pl
from jax.experimental.pallas import tpu as pltpu

D_MODEL = 1024
HEAD_DIM = 64
HEADS_PER_GROUP = 8
ATT_GROUPS = ((128, 1), (512, 4), (2048, 16))
N_GROUPS = len(ATT_GROUPS)
GROUP_WIDTH = HEADS_PER_GROUP * HEAD_DIM
ATT_QKV_WIDTH = N_GROUPS * GROUP_WIDTH
CONV_WIDTH = D_MODEL
CONV_K = 3
LN_EPS = 1e-5
LANES = 128
BF16_SUBLANES = 16

ATT_BLOCK = 128
assert all(w // d == ATT_BLOCK for w, d in ATT_GROUPS)
SUPER_BLOCK = ATT_BLOCK * max(d for _, d in ATT_GROUPS)
TILES_PER_SUPER = SUPER_BLOCK // ATT_BLOCK
ATT_TILES_PER_STEP = 8

QKV_ROW_TILE = 1024
QKV_DOT_ROW_CHUNKS = 2
TAIL_ROW_TILE = 512
TAIL_ROW_CHUNKS = 2
TAIL_STREAMS = 2
assert TAIL_ROW_CHUNKS % TAIL_STREAMS == 0
TAIL_STAGES = 6
CONV_HALO = 8
MERGE_ROWS = 32
HEAD_PAIRS = HEADS_PER_GROUP // 2
assert 2 * HEAD_DIM == LANES
KV_WIDTH = 2 * GROUP_WIDTH
QKV_COL_OFF = (KV_WIDTH, 0, GROUP_WIDTH)
SCORE_SCALE = HEAD_DIM ** -0.5 * 1.4426950408889634

assert N_GROUPS * HEADS_PER_GROUP <= LANES
VMEM_LIMIT_BYTES = 56 * 1024 * 1024

F32 = jnp.float32
BF16 = jnp.bfloat16

Q_OFF, K_OFF, V_OFF = 0, ATT_QKV_WIDTH, 2 * ATT_QKV_WIDTH
G_ATT_OFF = 3 * ATT_QKV_WIDTH
H_OFF = G_ATT_OFF + GROUP_WIDTH
B_OFF = H_OFF + CONV_WIDTH
C_OFF = B_OFF + CONV_WIDTH
G_CONV_OFF = C_OFF + CONV_WIDTH
GATE_OFF = G_CONV_OFF + CONV_WIDTH


def _qkv_kernel(x_ref, w_ref, *refs):
  tm = x_ref.shape[0]
  out_refs = refs[:N_GROUPS]
  xp_ref = refs[N_GROUPS:2 * N_GROUPS]
  xl_ref = refs[2 * N_GROUPS:]
  dil = [d for _, d in ATT_GROUPS]
  assert dil[0] == 1
  scale = SCORE_SCALE
  lane_tiles = [pl.ds(c * LANES, LANES) for c in range(D_MODEL // LANES)]

  def load_rows():
    def piece(c, lanes, rows):
      xp_ref[0][rows, lanes] = x_ref[rows, lanes].astype(BF16)
      xl_ref[0][c, rows, :] = x_ref[rows, lanes]
    quarter = tm // 4
    return [functools.partial(piece, c, lanes, pl.ds(k * quarter, quarter))
            for c, lanes in enumerate(lane_tiles) for k in range(4)]

  def permute_rows(g):
    q = dil[g + 1] // dil[g]
    assert dil[g + 1] == q * dil[g]
    n, n_next = tm // dil[g], tm // dil[g + 1]

    def piece(r, a, c, lanes):
      dst = pl.ds((a * dil[g] + r) * n_next, n_next)
      rows = xl_ref[g][c, pl.ds(r * n + a, n_next, stride=q), :]
      xp_ref[g + 1][dst, lanes] = rows.astype(BF16)
      if g + 2 < N_GROUPS:
        xl_ref[g + 1][c, dst, :] = rows
    return [functools.partial(piece, r, a, c, lanes) for r in range(dil[g]) for a in range(q)
            for c, lanes in enumerate(lane_tiles)]

  def project(g, pieces):
    n = tm // dil[g]
    row_chunk = tm // QKV_DOT_ROW_CHUNKS
    n_dots = 3 * QKV_DOT_ROW_CHUNKS
    k = 0
    for sec, off in enumerate((Q_OFF, K_OFF, V_OFF)):
      w_cols = pl.ds(off + g * GROUP_WIDTH, GROUP_WIDTH)
      for rc in range(QKV_DOT_ROW_CHUNKS):
        lo = rc * row_chunk
        res = jnp.dot(xp_ref[g][lo:lo + row_chunk, :], w_ref[:, w_cols],
                      preferred_element_type=F32)
        if sec == 0:
          res = res * scale
        res = res.astype(BF16)
        assert row_chunk % n == 0 or n % row_chunk == 0
        step = min(n, row_chunk)
        for s0 in range(0, row_chunk, step):
          r, l0 = divmod(lo + s0, n)
          out_refs[g][r, l0:l0 + step, pl.ds(QKV_COL_OFF[sec], GROUP_WIDTH)] = res[s0:s0 + step]
        k += 1
        for piece in pieces[(k - 1) * len(pieces) // n_dots:k * len(pieces) // n_dots]:
          piece()

  @pl.when(jnp.logical_and(pl.program_id(0) == 0, pl.program_id(1) == 0))
  def _():
    for piece in load_rows():
      piece()

  for g in range(N_GROUPS):
    @pl.when(pl.program_id(1) == g)
    def _(g=g):
      project(g, permute_rows(g) if g + 1 < N_GROUPS else load_rows())


def _qkv_projection(x2d, w_bf16):
  s = x2d.shape[0]
  tm = QKV_ROW_TILE

  assert (Q_OFF, K_OFF, V_OFF) == (0, ATT_QKV_WIDTH, 2 * ATT_QKV_WIDTH)
  w_spec = pl.BlockSpec((D_MODEL, 3 * ATT_QKV_WIDTH), lambda i, g: (0, 0),
                        pipeline_mode=pl.Buffered(1))

  def x_map(i, g):
    return jnp.minimum(i + g // (N_GROUPS - 1), s // tm - 1), 0

  return pl.pallas_call(
      _qkv_kernel,
      out_shape=[jax.ShapeDtypeStruct((d, s // d, ATT_QKV_WIDTH), BF16) for _, d in ATT_GROUPS],
      grid=(s // tm, N_GROUPS),
      in_specs=[pl.BlockSpec((tm, D_MODEL), x_map), w_spec],
      out_specs=[pl.BlockSpec((d, tm // d, ATT_QKV_WIDTH), lambda i, g: (0, i, 0))
                 for _, d in ATT_GROUPS],
      scratch_shapes=(
          [pltpu.VMEM((tm, D_MODEL), BF16)] * N_GROUPS
          + [pltpu.VMEM((D_MODEL // LANES, tm, LANES), F32)] * (N_GROUPS - 1)),
      compiler_params=pltpu.CompilerParams(
          dimension_semantics=("arbitrary", "arbitrary"),
          vmem_limit_bytes=VMEM_LIMIT_BYTES,
      ),
      name="qkv_projection",
  )(x2d, w_bf16)


def _attention_tile(cur, prev, bias, store_o):
  nq = ATT_BLOCK
  lane = lax.broadcasted_iota(jnp.int32, (nq, LANES), 1)
  low_half = lane < HEAD_DIM
  head_of_lane = lane % HEADS_PER_GROUP
  ones = jnp.ones((2 * nq, LANES), BF16)
  m_tile = l_tile = None

  for p in range(HEAD_PAIRS):
    q2 = cur(0, p)
    k2 = jnp.concatenate([prev(1, p), cur(1, p)], axis=0)
    v2 = jnp.concatenate([prev(2, p), cur(2, p)], axis=0)
    zero = jnp.zeros_like(q2)
    qs = jnp.concatenate([jnp.where(low_half, q2, zero), jnp.where(low_half, zero, q2)], axis=0)
    sc = lax.dot_general(qs, k2, (((1,), (1,)), ((), ())), preferred_element_type=F32)
    sc = sc + bias
    m = jnp.max(sc, axis=1, keepdims=True)
    e = jnp.exp2(sc - m).astype(BF16)
    ol = jnp.dot(e, jnp.concatenate([v2, ones], axis=1), preferred_element_type=F32)
    store_o(p, jnp.where(low_half, ol[:nq, :LANES], ol[nq:, :LANES]).astype(BF16))
    for k, rows in enumerate((slice(0, nq), slice(nq, 2 * nq))):
      m_h = jnp.broadcast_to(m[rows], (nq, LANES))
      l_h = ol[rows, LANES:]
      if m_tile is None:
        m_tile, l_tile = m_h, l_h
      else:
        own = head_of_lane == 2 * p + k
        m_tile, l_tile = jnp.where(own, m_h, m_tile), jnp.where(own, l_h, l_tile)
  return m_tile, l_tile


def _step_shape(d):
  nt = ATT_TILES_PER_STEP
  bpr = TILES_PER_SUPER // d
  blocks = min(nt, bpr)
  assert nt % blocks == 0 and bpr % blocks == 0
  return nt // blocks, blocks


def _step_tiles(d, sb, j):
  n_res, blocks = _step_shape(d)
  bpr = TILES_PER_SUPER // d
  spr = bpr // blocks
  b0 = (j % spr) * blocks
  res_group = j // spr
  return (res_group, sb * bpr + b0,
          (lambda t: res_group * n_res + t // blocks), (lambda t: b0 + t % blocks))


def _attention_kernel(*refs):
  in_refs = refs[:2 * N_GROUPS]
  o_refs = refs[2 * N_GROUPS:3 * N_GROUPS]
  coef_ref = refs[3 * N_GROUPS]
  max_ref, sum_ref, bias_ref = refs[3 * N_GROUPS + 1:]
  sb = pl.program_id(0)
  j = pl.program_id(1)
  nq = ATT_BLOCK
  nt = ATT_TILES_PER_STEP

  @pl.when(jnp.logical_and(sb == 0, j == 0))
  def _():
    row = lax.broadcasted_iota(jnp.int32, (2 * nq, 2 * nq), 0) % nq
    col = lax.broadcasted_iota(jnp.int32, (2 * nq, 2 * nq), 1)
    for hp in range(2):
      first_col = row if hp else jnp.maximum(row, nq)
      valid = jnp.logical_and(col >= first_col, col - nq <= row)
      bias_ref[hp] = jnp.where(valid, 0.0, -jnp.inf).astype(F32)

  def cols(sec, p):
    return pl.ds(QKV_COL_OFF[sec] + p * LANES, LANES)

  for t in range(nt):
    for g, (_, d) in enumerate(ATT_GROUPS):
      cur_ref, prev_ref = in_refs[2 * g:2 * g + 2]
      o_ref = o_refs[g]
      _, first_blk, res, loc = _step_tiles(d, sb, j)
      ri, bi = divmod(t, _step_shape(d)[1])
      tile_rows = (ri, pl.ds(bi * nq, nq))
      cur = lambda sec, p, tile_rows=tile_rows: cur_ref[tile_rows + (cols(sec, p),)]
      if bi == 0:
        prev = lambda sec, p, ri=ri: prev_ref[ri, :, cols(sec, p)]
        has_prev = (first_blk > 0).astype(jnp.int32)
      else:
        prev = lambda sec, p, ri=ri, bi=bi: cur_ref[ri, (bi - 1) * nq:bi * nq, cols(sec, p)]
        has_prev = 1

      def store_o(p, o, o_ref=o_ref, tile_rows=tile_rows):
        o_ref[tile_rows + (pl.ds(p * LANES, LANES),)] = o

      m_tile, l_tile = _attention_tile(cur, prev, bias_ref[has_prev], store_o)
      r, b = res(t), loc(t)
      if d == 1:
        rows = pl.ds(pl.multiple_of(b * nq, nq), nq)
      else:
        rows = pl.ds(b * nq * d + r, nq, stride=d)
      max_ref[g, rows, :] = m_tile
      sum_ref[g, rows, :] = l_tile

  @pl.when(j == TILES_PER_SUPER // nt - 1)
  def _():
    def body(c, carry):
      rows = pl.ds(pl.multiple_of(c * MERGE_ROWS, MERGE_ROWS), MERGE_ROWS)
      ms = [max_ref[g, rows, :] for g in range(N_GROUPS)]
      top = functools.reduce(jnp.maximum, ms)
      ws = [jnp.exp2(m - top) for m in ms]
      den = functools.reduce(jnp.add, [w * sum_ref[g, rows, :] for g, w in enumerate(ws)])
      group_of_lane = lax.broadcasted_iota(jnp.int32, ws[0].shape, 1) // HEADS_PER_GROUP
      w = functools.reduce(lambda acc, g: jnp.where(group_of_lane == g, ws[g], acc),
                           range(1, N_GROUPS), ws[0])
      coef_ref[rows, :] = w / den
      return carry
    lax.fori_loop(0, SUPER_BLOCK // MERGE_ROWS, body, 0, unroll=2)


def _attention(qkv_groups):
  s = qkv_groups[0].shape[1]
  nt = ATT_TILES_PER_STEP
  in_specs, args, o_specs = [], [], []
  for (_, d), arr in zip(ATT_GROUPS, qkv_groups):
    n_res, blocks = _step_shape(d)

    def cur_map(sb, j, d=d, blocks=blocks):
      res_group, first_blk, _, _ = _step_tiles(d, sb, j)
      return res_group, first_blk // blocks, 0

    def prev_map(sb, j, d=d):
      res_group, first_blk, _, _ = _step_tiles(d, sb, j)
      return res_group, jnp.maximum(first_blk - 1, 0), 0

    in_specs += [pl.BlockSpec((n_res, blocks * ATT_BLOCK, ATT_QKV_WIDTH), cur_map),
                 pl.BlockSpec((n_res, ATT_BLOCK, KV_WIDTH), prev_map)]
    o_specs.append(pl.BlockSpec((n_res, blocks * ATT_BLOCK, GROUP_WIDTH), cur_map))
    args += [arr] * 2
  stats = pltpu.VMEM((N_GROUPS, SUPER_BLOCK, LANES), F32)
  outs = pl.pallas_call(
      _attention_kernel,
      out_shape=([jax.ShapeDtypeStruct((d, s // d, GROUP_WIDTH), BF16) for _, d in ATT_GROUPS]
                 + [jax.ShapeDtypeStruct((s, LANES), F32)]),
      grid=(s // SUPER_BLOCK, TILES_PER_SUPER // nt),
      in_specs=in_specs,
      out_specs=o_specs + [pl.BlockSpec((SUPER_BLOCK, LANES), lambda sb, j: (sb, 0))],
      scratch_shapes=[stats, stats, pltpu.VMEM((2, 2 * ATT_BLOCK, 2 * ATT_BLOCK), F32)],
      compiler_params=pltpu.CompilerParams(
          dimension_semantics=("arbitrary", "arbitrary"),
          vmem_limit_bytes=VMEM_LIMIT_BYTES,
      ),
      name="dilated_attention",
  )(*args)
  return outs[:N_GROUPS], outs[N_GROUPS]


def _silu(x):
  return x * jax.nn.sigmoid(x)


def _tail_kernel(alpha, x_ref, ao1_ref, ao2_ref, ao3_ref, coef_ref,
                 wga_ref, wh_ref, wb_ref, wc_ref, wgc_ref, wg1_ref, wg2_ref,
                 cw_ref, wco32_ref, wao32_ref, bg_ref, wo32_ref, lg_ref, lb_ref, o_ref, u_ref, perm_ref,
                 wco_ref, wao_ref, wo_ref):
  tm = x_ref.shape[0]
  chunk = tm // TAIL_ROW_CHUNKS
  out_refs = (ao1_ref, ao2_ref, ao3_ref)

  def dot(a, w_ref):
    return jnp.dot(a, w_ref[...], preferred_element_type=F32)

  @pl.when(pl.program_id(0) == 0)
  def _():
    u_ref[0:CONV_HALO, :] = jnp.zeros((CONV_HALO, CONV_WIDTH), F32)
    for src_ref, dst_ref in ((wco32_ref, wco_ref), (wao32_ref, wao_ref), (wo32_ref, wo_ref)):
      dst_ref[...] = src_ref[...].astype(BF16)
    row = lax.broadcasted_iota(jnp.int32, (chunk, chunk), 0)
    col = lax.broadcasted_iota(jnp.int32, (chunk, chunk), 1)
    for g, (_, d) in enumerate(ATT_GROUPS):
      src = (row % d) * (chunk // d) + row // d
      perm_ref[g] = jnp.where(col == src, 1.0, 0.0).astype(BF16)

  def attention_mix(lo):
    rows = pl.ds(lo, chunk)
    coef = coef_ref[rows, :]
    att = None
    for g, (_, d) in enumerate(ATT_GROUPS):
      out_ref = out_refs[g]
      n = chunk // d
      if d == 1:
        o_g = out_ref[0, rows, :].astype(F32)
      else:
        res_major = jnp.concatenate([out_ref[r, pl.ds(lo // d, n), :] for r in range(d)], axis=0)
        o_g = jnp.dot(perm_ref[g], res_major, preferred_element_type=F32)
      factor = jnp.concatenate(
          [jnp.broadcast_to(coef[:, g * HEADS_PER_GROUP + h:g * HEADS_PER_GROUP + h + 1],
                            (chunk, HEAD_DIM)) for h in range(HEADS_PER_GROUP)], axis=1)
      att = factor * o_g if att is None else att + factor * o_g
    return att

  def row_chunk(lo):
    rows = pl.ds(lo, chunk)
    xb = x_ref[rows, :].astype(BF16)
    u_ref[pl.ds(CONV_HALO + lo, chunk), :] = dot(xb, wc_ref) * dot(xb, wh_ref)
    att = attention_mix(lo)
    yield
    p_b, p_gc = dot(xb, wb_ref), dot(xb, wgc_ref)
    conv = cw_ref[CONV_K - 1:CONV_K, :] * u_ref[pl.ds(CONV_HALO + lo, chunk), :]
    for k in range(CONV_K - 1):
      off = CONV_HALO + lo - (CONV_K - 1 - k)
      conv = conv + cw_ref[k:k + 1, :] * u_ref[pl.ds(off, chunk), :]
    a_conv = ((p_b * conv) * _silu(p_gc)).astype(BF16)
    yield
    y_conv = dot(a_conv, wco_ref)
    a_att = (att * _silu(dot(xb, wga_ref))).astype(BF16)
    yield
    y_att = dot(a_att, wao_ref)
    m_c = jax.nn.sigmoid(dot(xb, wg1_ref) + bg_ref[:, :D_MODEL]) * y_conv
    yield
    g_a = jax.nn.sigmoid(dot(xb, wg2_ref) + bg_ref[:, D_MODEL:])
    merged = (m_c + g_a * y_att).astype(BF16)
    yield
    y = alpha * x_ref[rows, :] + dot(merged, wo_ref)
    mu = jnp.mean(y, axis=-1, keepdims=True)
    yc = y - mu
    var = jnp.mean(yc * yc, axis=-1, keepdims=True)
    o_ref[rows, :] = yc * lax.rsqrt(var + LN_EPS) * lg_ref[...] + lb_ref[...]
    yield

  for first in range(0, TAIL_ROW_CHUNKS, TAIL_STREAMS):
    streams = [row_chunk(k * chunk) for k in range(first, first + TAIL_STREAMS)]
    for stage in range(TAIL_STAGES):
      for stream in streams:
        next(stream)
      if stage == 1 and first + TAIL_STREAMS == TAIL_ROW_CHUNKS:
        u_ref[0:CONV_HALO, :] = u_ref[tm:tm + CONV_HALO, :]


def _tail(x2d, att_outs, att_coefs, w_bf16, conv_w, w_conv_out, w_att_out, b_gate, w_o, ln_g, ln_b,
          alpha):
  s = x2d.shape[0]
  tm = TAIL_ROW_TILE
  chunk = tm // TAIL_ROW_CHUNKS
  assert all(chunk % (d * BF16_SUBLANES) == 0 for _, d in ATT_GROUPS)

  def whole(arr):
    return pl.BlockSpec(arr.shape, lambda i: (0,) * arr.ndim, pipeline_mode=pl.Buffered(1))

  def w_cols(off, width):
    assert off % width == 0
    return pl.BlockSpec((D_MODEL, width), lambda i: (0, off // width), pipeline_mode=pl.Buffered(1))

  w_specs = [w_cols(G_ATT_OFF, GROUP_WIDTH)] + [
      w_cols(off, CONV_WIDTH)
      for off in (H_OFF, B_OFF, C_OFF, G_CONV_OFF, GATE_OFF, GATE_OFF + D_MODEL)]
  others = (conv_w, w_conv_out, w_att_out, b_gate, w_o, ln_g, ln_b)
  return pl.pallas_call(
      functools.partial(_tail_kernel, alpha),
      out_shape=jax.ShapeDtypeStruct((s, D_MODEL), F32),
      grid=(s // tm,),
      in_specs=[pl.BlockSpec((tm, D_MODEL), lambda i: (i, 0))]
      + [pl.BlockSpec((d, tm // d, GROUP_WIDTH), lambda i: (0, i, 0)) for _, d in ATT_GROUPS]
      + [pl.BlockSpec((tm, LANES), lambda i: (i, 0))]
      + w_specs + [whole(w) for w in others],
      out_specs=pl.BlockSpec((tm, D_MODEL), lambda i: (i, 0)),
      scratch_shapes=[pltpu.VMEM((CONV_HALO + tm, CONV_WIDTH), F32),
                      pltpu.VMEM((N_GROUPS, chunk, chunk), BF16)]
      + [pltpu.VMEM(w.shape, BF16) for w in (w_conv_out, w_att_out, w_o)],
      compiler_params=pltpu.CompilerParams(
          dimension_semantics=("arbitrary",),
          vmem_limit_bytes=VMEM_LIMIT_BYTES,
      ),
      name="conv_merge_norm",
  )(x2d, *att_outs, att_coefs, *([w_bf16] * len(w_specs)), *others)


def _layer(x2d, w_in, conv_w, w_conv_out, w_att_out, b_gate, w_o, ln_g, ln_b, alpha):
  assert w_in.shape == (D_MODEL, GATE_OFF + 2 * D_MODEL)
  w_bf16 = w_in.astype(BF16)
  qkv_groups = _qkv_projection(x2d, w_bf16)
  att_outs, att_coefs = _attention(qkv_groups)
  return _tail(x2d, att_outs, att_coefs, w_bf16, conv_w, w_conv_out, w_att_out,
               b_gate.reshape(1, -1), w_o, ln_g.reshape(1, -1), ln_b.reshape(1, -1), alpha)


def kernel(x, w_in, conv_w, w_conv_out, w_att_out, b_gate, w_o, ln_g, ln_b):
  batch, seq, d_model = x.shape
  depth = w_in.shape[0]
  assert d_model == D_MODEL and seq % SUPER_BLOCK == 0
  alpha = (2.0 * depth) ** 0.25
  outs = []
  for b in range(batch):
    h = x[b]
    for layer in range(depth):
      h = _layer(h, w_in[layer], conv_w[layer], w_conv_out[layer], w_att_out[layer],
                 b_gate[layer], w_o[layer], ln_g[layer], ln_b[layer], alpha)
    outs.append(h)
  return jnp.stack(outs)
```

```python
import functools

import jax
import jax.numpy as jnp
from jax import lax
from jax.experimental import pallas as pl
from jax.experimental.pallas import tpu as pltpu

D_MODEL = 1024
HEAD_DIM = 64
HEADS_PER_GROUP = 8
ATT_GROUPS = ((128, 1), (512, 4), (2048, 16))
N_GROUPS = len(ATT_GROUPS)
GROUP_WIDTH = HEADS_PER_GROUP * HEAD_DIM
ATT_QKV_WIDTH = N_GROUPS * GROUP_WIDTH
CONV_WIDTH = D_MODEL
CONV_K = 3
LN_EPS = 1e-5
LANES = 128
BF16_SUBLANES = 16

ATT_BLOCK = 128
assert all(w // d == ATT_BLOCK for w, d in ATT_GROUPS)
SUPER_BLOCK = ATT_BLOCK * max(d for _, d in ATT_GROUPS)
TILES_PER_SUPER = SUPER_BLOCK // ATT_BLOCK
ATT_TILES_PER_STEP = 8

QKV_ROW_TILE = 1024
QKV_DOT_ROW_CHUNKS = 4
TAIL_ROW_TILE = 512
TAIL_ROW_CHUNKS = 2
TAIL_STREAMS = 2
assert TAIL_ROW_CHUNKS % TAIL_STREAMS == 0
TAIL_STAGES = 6
CONV_HALO = 8
MERGE_ROWS = 32
HEAD_PAIRS = HEADS_PER_GROUP // 2
assert 2 * HEAD_DIM == LANES
KV_WIDTH = 2 * GROUP_WIDTH
QKV_COL_OFF = (KV_WIDTH, 0, GROUP_WIDTH)
SCORE_SCALE = HEAD_DIM ** -0.5 * 1.4426950408889634

assert N_GROUPS * HEADS_PER_GROUP <= LANES
VMEM_LIMIT_BYTES = 56 * 1024 * 1024

F32 = jnp.float32
BF16 = jnp.bfloat16

Q_OFF, K_OFF, V_OFF = 0, ATT_QKV_WIDTH, 2 * ATT_QKV_WIDTH
G_ATT_OFF = 3 * ATT_QKV_WIDTH
H_OFF = G_ATT_OFF + GROUP_WIDTH
B_OFF = H_OFF + CONV_WIDTH
C_OFF = B_OFF + CONV_WIDTH
G_CONV_OFF = C_OFF + CONV_WIDTH
GATE_OFF = G_CONV_OFF + CONV_WIDTH


def _qkv_kernel(x_ref, w_ref, *refs):
  tm = x_ref.shape[0]
  out_refs = refs[:N_GROUPS]
  xp_ref = refs[N_GROUPS:2 * N_GROUPS]
  xl_ref = refs[2 * N_GROUPS:]
  dil = [d for _, d in ATT_GROUPS]
  assert dil[0] == 1
  scale = SCORE_SCALE
  lane_tiles = [pl.ds(c * LANES, LANES) for c in range(D_MODEL // LANES)]

  def load_rows():
    def piece(c, lanes, rows):
      xp_ref[0][rows, lanes] = x_ref[rows, lanes].astype(BF16)
      xl_ref[0][c, rows, :] = x_ref[rows, lanes]
    quarter = tm // 4
    return [functools.partial(piece, c, lanes, pl.ds(k * quarter, quarter))
            for c, lanes in enumerate(lane_tiles) for k in range(4)]

  def permute_rows(g):
    q = dil[g + 1] // dil[g]
    assert dil[g + 1] == q * dil[g]
    n, n_next = tm // dil[g], tm // dil[g + 1]

    def piece(r, a, c, lanes):
      dst = pl.ds((a * dil[g] + r) * n_next, n_next)
      rows = xl_ref[g][c, pl.ds(r * n + a, n_next, stride=q), :]
      xp_ref[g + 1][dst, lanes] = rows.astype(BF16)
      if g + 2 < N_GROUPS:
        xl_ref[g + 1][c, dst, :] = rows
    return [functools.partial(piece, r, a, c, lanes) for r in range(dil[g]) for a in range(q)
            for c, lanes in enumerate(lane_tiles)]

  def project(g, pieces):
    n = tm // dil[g]
    row_chunk = tm // QKV_DOT_ROW_CHUNKS
    n_dots = 3 * QKV_DOT_ROW_CHUNKS
    k = 0
    for sec, off in enumerate((Q_OFF, K_OFF, V_OFF)):
      w_cols = pl.ds(off + g * GROUP_WIDTH, GROUP_WIDTH)
      for rc in range(QKV_DOT_ROW_CHUNKS):
        lo = rc * row_chunk
        res = jnp.dot(xp_ref[g][lo:lo + row_chunk, :], w_ref[:, w_cols],
                      preferred_element_type=F32)
        if sec == 0:
          res = res * scale
        res = res.astype(BF16)
        assert row_chunk % n == 0 or n % row_chunk == 0
        step = min(n, row_chunk)
        for s0 in range(0, row_chunk, step):
          r, l0 = divmod(lo + s0, n)
          out_refs[g][r, l0:l0 + step, pl.ds(QKV_COL_OFF[sec], GROUP_WIDTH)] = res[s0:s0 + step]
        k += 1
        for piece in pieces[(k - 1) * len(pieces) // n_dots:k * len(pieces) // n_dots]:
          piece()

  @pl.when(jnp.logical_and(pl.program_id(0) == 0, pl.program_id(1) == 0))
  def _():
    for piece in load_rows():
      piece()

  for g in range(N_GROUPS):
    @pl.when(pl.program_id(1) == g)
    def _(g=g):
      project(g, permute_rows(g) if g + 1 < N_GROUPS else load_rows())


def _qkv_projection(x2d, w_bf16):
  s = x2d.shape[0]
  tm = QKV_ROW_TILE

  assert (Q_OFF, K_OFF, V_OFF) == (0, ATT_QKV_WIDTH, 2 * ATT_QKV_WIDTH)
  w_spec = pl.BlockSpec((D_MODEL, 3 * ATT_QKV_WIDTH), lambda i, g: (0, 0),
                        pipeline_mode=pl.Buffered(1))

  def x_map(i, g):
    return jnp.minimum(i + g // (N_GROUPS - 1), s // tm - 1), 0

  return pl.pallas_call(
      _qkv_kernel,
      out_shape=[jax.ShapeDtypeStruct((d, s // d, ATT_QKV_WIDTH), BF16) for _, d in ATT_GROUPS],
      grid=(s // tm, N_GROUPS),
      in_specs=[pl.BlockSpec((tm, D_MODEL), x_map), w_spec],
      out_specs=[pl.BlockSpec((d, tm // d, ATT_QKV_WIDTH), lambda i, g: (0, i, 0))
                 for _, d in ATT_GROUPS],
      scratch_shapes=(
          [pltpu.VMEM((tm, D_MODEL), BF16)] * N_GROUPS
          + [pltpu.VMEM((D_MODEL // LANES, tm, LANES), F32)] * (N_GROUPS - 1)),
      compiler_params=pltpu.CompilerParams(
          dimension_semantics=("arbitrary", "arbitrary"),
          vmem_limit_bytes=VMEM_LIMIT_BYTES,
      ),
      name="qkv_projection",
  )(x2d, w_bf16)


def _attention_tile(cur, prev, bias, store_o):
  nq = ATT_BLOCK
  lane = lax.broadcasted_iota(jnp.int32, (nq, LANES), 1)
  low_half = lane < HEAD_DIM
  head_of_lane = lane % HEADS_PER_GROUP
  ones = jnp.ones((2 * nq, LANES), BF16)
  m_tile = l_tile = None

  for p in range(HEAD_PAIRS):
    q2 = cur(0, p)
    k2 = jnp.concatenate([prev(1, p), cur(1, p)], axis=0)
    v2 = jnp.concatenate([prev(2, p), cur(2, p)], axis=0)
    zero = jnp.zeros_like(q2)
    qs = jnp.concatenate([jnp.where(low_half, q2, zero), jnp.where(low_half, zero, q2)], axis=0)
    sc = lax.dot_general(qs, k2, (((1,), (1,)), ((), ())), preferred_element_type=F32)
    sc = sc + bias
    m = jnp.max(sc, axis=1, keepdims=True)
    e = jnp.exp2(sc - m).astype(BF16)
    ol = jnp.dot(e, jnp.concatenate([v2, ones], axis=1), preferred_element_type=F32)
    store_o(p, jnp.where(low_half, ol[:nq, :LANES], ol[nq:, :LANES]).astype(BF16))
    for k, rows in enumerate((slice(0, nq), slice(nq, 2 * nq))):
      m_h = jnp.broadcast_to(m[rows], (nq, LANES))
      l_h = ol[rows, LANES:]
      if m_tile is None:
        m_tile, l_tile = m_h, l_h
      else:
        own = head_of_lane == 2 * p + k
        m_tile, l_tile = jnp.where(own, m_h, m_tile), jnp.where(own, l_h, l_tile)
  return m_tile, l_tile


def _step_shape(d):
  nt = ATT_TILES_PER_STEP
  bpr = TILES_PER_SUPER // d
  blocks = min(nt, bpr)
  assert nt % blocks == 0 and bpr % blocks == 0
  return nt // blocks, blocks


def _step_tiles(d, sb, j):
  n_res, blocks = _step_shape(d)
  bpr = TILES_PER_SUPER // d
  spr = bpr // blocks
  b0 = (j % spr) * blocks
  res_group = j // spr
  return (res_group, sb * bpr + b0,
          (lambda t: res_group * n_res + t // blocks), (lambda t: b0 + t % blocks))


def _attention_kernel(*refs):
  in_refs = refs[:2 * N_GROUPS]
  o_refs = refs[2 * N_GROUPS:3 * N_GROUPS]
  coef_ref = refs[3 * N_GROUPS]
  max_ref, sum_ref, bias_ref = refs[3 * N_GROUPS + 1:]
  sb = pl.program_id(0)
  j = pl.program_id(1)
  nq = ATT_BLOCK
  nt = ATT_TILES_PER_STEP

  @pl.when(jnp.logical_and(sb == 0, j == 0))
  def _():
    row = lax.broadcasted_iota(jnp.int32, (2 * nq, 2 * nq), 0) % nq
    col = lax.broadcasted_iota(jnp.int32, (2 * nq, 2 * nq), 1)
    for hp in range(2):
      first_col = row if hp else jnp.maximum(row, nq)
      valid = jnp.logical_and(col >= first_col, col - nq <= row)
      bias_ref[hp] = jnp.where(valid, 0.0, -jnp.inf).astype(F32)

  def cols(sec, p):
    return pl.ds(QKV_COL_OFF[sec] + p * LANES, LANES)

  for t in range(nt):
    for g, (_, d) in enumerate(ATT_GROUPS):
      cur_ref, prev_ref = in_refs[2 * g:2 * g + 2]
      o_ref = o_refs[g]
      _, first_blk, res, loc = _step_tiles(d, sb, j)
      ri, bi = divmod(t, _step_shape(d)[1])
      tile_rows = (ri, pl.ds(bi * nq, nq))
      cur = lambda sec, p, tile_rows=tile_rows: cur_ref[tile_rows + (cols(sec, p),)]
      if bi == 0:
        prev = lambda sec, p, ri=ri: prev_ref[ri, :, cols(sec, p)]
        has_prev = (first_blk > 0).astype(jnp.int32)
      else:
        prev = lambda sec, p, ri=ri, bi=bi: cur_ref[ri, (bi - 1) * nq:bi * nq, cols(sec, p)]
        has_prev = 1

      def store_o(p, o, o_ref=o_ref, tile_rows=tile_rows):
        o_ref[tile_rows + (pl.ds(p * LANES, LANES),)] = o

      m_tile, l_tile = _attention_tile(cur, prev, bias_ref[has_prev], store_o)
      r, b = res(t), loc(t)
      if d == 1:
        rows = pl.ds(pl.multiple_of(b * nq, nq), nq)
      else:
        rows = pl.ds(b * nq * d + r, nq, stride=d)
      max_ref[g, rows, :] = m_tile
      sum_ref[g, rows, :] = l_tile

  @pl.when(j == TILES_PER_SUPER // nt - 1)
  def _():
    def body(c, carry):
      rows = pl.ds(pl.multiple_of(c * MERGE_ROWS, MERGE_ROWS), MERGE_ROWS)
      ms = [max_ref[g, rows, :] for g in range(N_GROUPS)]
      top = functools.reduce(jnp.maximum, ms)
      ws = [jnp.exp2(m - top) for m in ms]
      den = functools.reduce(jnp.add, [w * sum_ref[g, rows, :] for g, w in enumerate(ws)])
      group_of_lane = lax.broadcasted_iota(jnp.int32, ws[0].shape, 1) // HEADS_PER_GROUP
      w = functools.reduce(lambda acc, g: jnp.where(group_of_lane == g, ws[g], acc),
                           range(1, N_GROUPS), ws[0])
      coef_ref[rows, :] = w / den
      return carry
    lax.fori_loop(0, SUPER_BLOCK // MERGE_ROWS, body, 0, unroll=2)


def _attention(qkv_groups):
  s = qkv_groups[0].shape[1]
  nt = ATT_TILES_PER_STEP
  in_specs, args, o_specs = [], [], []
  for (_, d), arr in zip(ATT_GROUPS, qkv_groups):
    n_res, blocks = _step_shape(d)

    def cur_map(sb, j, d=d, blocks=blocks):
      res_group, first_blk, _, _ = _step_tiles(d, sb, j)
      return res_group, first_blk // blocks, 0

    def prev_map(sb, j, d=d):
      res_group, first_blk, _, _ = _step_tiles(d, sb, j)
      return res_group, jnp.maximum(first_blk - 1, 0), 0

    in_specs += [pl.BlockSpec((n_res, blocks * ATT_BLOCK, ATT_QKV_WIDTH), cur_map),
                 pl.BlockSpec((n_res, ATT_BLOCK, KV_WIDTH), prev_map)]
    o_specs.append(pl.BlockSpec((n_res, blocks * ATT_BLOCK, GROUP_WIDTH), cur_map))
    args += [arr] * 2
  stats = pltpu.VMEM((N_GROUPS, SUPER_BLOCK, LANES), F32)
  outs = pl.pallas_call(
      _attention_kernel,
      out_shape=([jax.ShapeDtypeStruct((d, s // d, GROUP_WIDTH), BF16) for _, d in ATT_GROUPS]
                 + [jax.ShapeDtypeStruct((s, LANES), F32)]),
      grid=(s // SUPER_BLOCK, TILES_PER_SUPER // nt),
      in_specs=in_specs,
      out_specs=o_specs + [pl.BlockSpec((SUPER_BLOCK, LANES), lambda sb, j: (sb, 0))],
      scratch_shapes=[stats, stats, pltpu.VMEM((2, 2 * ATT_BLOCK, 2 * ATT_BLOCK), F32)],
      compiler_params=pltpu.CompilerParams(
          dimension_semantics=("arbitrary", "arbitrary"),
          vmem_limit_bytes=VMEM_LIMIT_BYTES,
      ),
      name="dilated_attention",
  )(*args)
  return outs[:N_GROUPS], outs[N_GROUPS]


def _silu(x):
  return x * jax.nn.sigmoid(x)


def _tail_kernel(alpha, x_ref, ao1_ref, ao2_ref, ao3_ref, coef_ref,
                 wga_ref, wh_ref, wb_ref, wc_ref, wgc_ref, wg1_ref, wg2_ref,
                 cw_ref, wco32_ref, wao32_ref, bg_ref, wo32_ref, lg_ref, lb_ref, o_ref, u_ref, perm_ref,
                 wco_ref, wao_ref, wo_ref):
  tm = x_ref.shape[0]
  chunk = tm // TAIL_ROW_CHUNKS
  out_refs = (ao1_ref, ao2_ref, ao3_ref)

  def dot(a, w_ref):
    return jnp.dot(a, w_ref[...], preferred_element_type=F32)

  @pl.when(pl.program_id(0) == 0)
  def _():
    u_ref[0:CONV_HALO, :] = jnp.zeros((CONV_HALO, CONV_WIDTH), F32)
    for src_ref, dst_ref in ((wco32_ref, wco_ref), (wao32_ref, wao_ref), (wo32_ref, wo_ref)):
      dst_ref[...] = src_ref[...].astype(BF16)
    row = lax.broadcasted_iota(jnp.int32, (chunk, chunk), 0)
    col = lax.broadcasted_iota(jnp.int32, (chunk, chunk), 1)
    for g, (_, d) in enumerate(ATT_GROUPS):
      src = (row % d) * (chunk // d) + row // d
      perm_ref[g] = jnp.where(col == src, 1.0, 0.0).astype(BF16)

  def attention_mix(lo):
    rows = pl.ds(lo, chunk)
    coef = coef_ref[rows, :]
    att = None
    for g, (_, d) in enumerate(ATT_GROUPS):
      out_ref = out_refs[g]
      n = chunk // d
      if d == 1:
        o_g = out_ref[0, rows, :].astype(F32)
      else:
        res_major = jnp.concatenate([out_ref[r, pl.ds(lo // d, n), :] for r in range(d)], axis=0)
        o_g = jnp.dot(perm_ref[g], res_major, preferred_element_type=F32)
      factor = jnp.concatenate(
          [jnp.broadcast_to(coef[:, g * HEADS_PER_GROUP + h:g * HEADS_PER_GROUP + h + 1],
                            (chunk, HEAD_DIM)) for h in range(HEADS_PER_GROUP)], axis=1)
      att = factor * o_g if att is None else att + factor * o_g
    return att

  def row_chunk(lo):
    rows = pl.ds(lo, chunk)
    xb = x_ref[rows, :].astype(BF16)
    u_ref[pl.ds(CONV_HALO + lo, chunk), :] = dot(xb, wc_ref) * dot(xb, wh_ref)
    att = attention_mix(lo)
    yield
    p_b, p_gc = dot(xb, wb_ref), dot(xb, wgc_ref)
    conv = cw_ref[CONV_K - 1:CONV_K, :] * u_ref[pl.ds(CONV_HALO + lo, chunk), :]
    for k in range(CONV_K - 1):
      off = CONV_HALO + lo - (CONV_K - 1 - k)
      conv = conv + cw_ref[k:k + 1, :] * u_ref[pl.ds(off, chunk), :]
    a_conv = ((p_b * conv) * _silu(p_gc)).astype(BF16)
    yield
    y_conv = dot(a_conv, wco_ref)
    a_att = (att * _silu(dot(xb, wga_ref))).astype(BF16)
    yield
    y_att = dot(a_att, wao_ref)
    m_c = jax.nn.sigmoid(dot(xb, wg1_ref) + bg_ref[:, :D_MODEL]) * y_conv
    yield
    g_a = jax.nn.sigmoid(dot(xb, wg2_ref) + bg_ref[:, D_MODEL:])
    merged = (m_c + g_a * y_att).astype(BF16)
    yield
    y = alpha * x_ref[rows, :] + dot(merged, wo_ref)
    mu = jnp.mean(y, axis=-1, keepdims=True)
    yc = y - mu
    var = jnp.mean(yc * yc, axis=-1, keepdims=True)
    o_ref[rows, :] = yc * lax.rsqrt(var + LN_EPS) * lg_ref[...] + lb_ref[...]
    yield

  for first in range(0, TAIL_ROW_CHUNKS, TAIL_STREAMS):
    streams = [row_chunk(k * chunk) for k in range(first, first + TAIL_STREAMS)]
    for stage in range(TAIL_STAGES):
      for stream in streams:
        next(stream)
      if stage == 1 and first + TAIL_STREAMS == TAIL_ROW_CHUNKS:
        u_ref[0:CONV_HALO, :] = u_ref[tm:tm + CONV_HALO, :]


def _tail(x2d, att_outs, att_coefs, w_bf16, conv_w, w_conv_out, w_att_out, b_gate, w_o, ln_g, ln_b,
          alpha):
  s = x2d.shape[0]
  tm = TAIL_ROW_TILE
  chunk = tm // TAIL_ROW_CHUNKS
  assert all(chunk % (d * BF16_SUBLANES) == 0 for _, d in ATT_GROUPS)

  def whole(arr):
    return pl.BlockSpec(arr.shape, lambda i: (0,) * arr.ndim, pipeline_mode=pl.Buffered(1))

  def w_cols(off, width):
    assert off % width == 0
    return pl.BlockSpec((D_MODEL, width), lambda i: (0, off // width), pipeline_mode=pl.Buffered(1))

  w_specs = [w_cols(G_ATT_OFF, GROUP_WIDTH)] + [
      w_cols(off, CONV_WIDTH)
      for off in (H_OFF, B_OFF, C_OFF, G_CONV_OFF, GATE_OFF, GATE_OFF + D_MODEL)]
  others = (conv_w, w_conv_out, w_att_out, b_gate, w_o, ln_g, ln_b)
  return pl.pallas_call(
      functools.partial(_tail_kernel, alpha),
      out_shape=jax.ShapeDtypeStruct((s, D_MODEL), F32),
      grid=(s // tm,),
      in_specs=[pl.BlockSpec((tm, D_MODEL), lambda i: (i, 0))]
      + [pl.BlockSpec((d, tm // d, GROUP_WIDTH), lambda i: (0, i, 0)) for _, d in ATT_GROUPS]
      + [pl.BlockSpec((tm, LANES), lambda i: (i, 0))]
      + w_specs + [whole(w) for w in others],
      out_specs=pl.BlockSpec((tm, D_MODEL), lambda i: (i, 0)),
      scratch_shapes=[pltpu.VMEM((CONV_HALO + tm, CONV_WIDTH), F32),
                      pltpu.VMEM((N_GROUPS, chunk, chunk), BF16)]
      + [pltpu.VMEM(w.shape, BF16) for w in (w_conv_out, w_att_out, w_o)],
      compiler_params=pltpu.CompilerParams(
          dimension_semantics=("arbitrary",),
          vmem_limit_bytes=VMEM_LIMIT_BYTES,
      ),
      name="conv_merge_norm",
  )(x2d, *att_outs, att_coefs, *([w_bf16] * len(w_specs)), *others)


def _layer(x2d, w_in, conv_w, w_conv_out, w_att_out, b_gate, w_o, ln_g, ln_b, alpha):
  assert w_in.shape == (D_MODEL, GATE_OFF + 2 * D_MODEL)
  w_bf16 = w_in.astype(BF16)
  qkv_groups = _qkv_projection(x2d, w_bf16)
  att_outs, att_coefs = _attention(qkv_groups)
  return _tail(x2d, att_outs, att_coefs, w_bf16, conv_w, w_conv_out, w_att_out,
               b_gate.reshape(1, -1), w_o, ln_g.reshape(1, -1), ln_b.reshape(1, -1), alpha)


def kernel(x, w_in, conv_w, w_conv_out, w_att_out, b_gate, w_o, ln_g, ln_b):
  batch, seq, d_model = x.shape
  depth = w_in.shape[0]
  assert d_model == D_MODEL and seq % SUPER_BLOCK == 0
  alpha = (2.0 * depth) ** 0.25
  outs = []
  for b in range(batch):
    h = x[b]
    for layer in range(depth):
      h = _layer(h, w_in[layer], conv_w[layer], w_conv_out[layer], w_att_out[layer],
                 b_gate[layer], w_o[layer], ln_g[layer], ln_b[layer], alpha)
    outs.append(h)
  return jnp.stack(outs)
```

```python
import functools

import jax
import jax.numpy as jnp
from jax import lax
from jax.experimental import pallas as pl
from jax.experimental.pallas import tpu as pltpu

D_MODEL = 1024
HEAD_DIM = 64
HEADS_PER_GROUP = 8
ATT_GROUPS = ((128, 1), (512, 4), (2048, 16))
N_GROUPS = len(ATT_GROUPS)
GROUP_WIDTH = HEADS_PER_GROUP * HEAD_DIM
ATT_QKV_WIDTH = N_GROUPS * GROUP_WIDTH
CONV_WIDTH = D_MODEL
CONV_K = 3
LN_EPS = 1e-5
LANES = 128
BF16_SUBLANES = 16

ATT_BLOCK = 128
assert all(w // d == ATT_BLOCK for w, d in ATT_GROUPS)
SUPER_BLOCK = ATT_BLOCK * max(d for _, d in ATT_GROUPS)
TILES_PER_SUPER = SUPER_BLOCK // ATT_BLOCK
ATT_TILES_PER_STEP = 8

QKV_ROW_TILE = 1024
QKV_DOT_ROW_CHUNKS = 2
TAIL_ROW_TILE = 512
TAIL_ROW_CHUNKS = 2
TAIL_STREAMS = 2
assert TAIL_ROW_CHUNKS % TAIL_STREAMS == 0
TAIL_STAGES = 6
CONV_HALO = 8
MERGE_ROWS = 32
HEAD_PAIRS = HEADS_PER_GROUP // 2
assert 2 * HEAD_DIM == LANES
KV_WIDTH = 2 * GROUP_WIDTH
QKV_COL_OFF = (KV_WIDTH, 0, GROUP_WIDTH)
SCORE_SCALE = HEAD_DIM ** -0.5 * 1.4426950408889634

assert N_GROUPS * HEADS_PER_GROUP <= LANES
VMEM_LIMIT_BYTES = 56 * 1024 * 1024

F32 = jnp.float32
BF16 = jnp.bfloat16

Q_OFF, K_OFF, V_OFF = 0, ATT_QKV_WIDTH, 2 * ATT_QKV_WIDTH
G_ATT_OFF = 3 * ATT_QKV_WIDTH
H_OFF = G_ATT_OFF + GROUP_WIDTH
B_OFF = H_OFF + CONV_WIDTH
C_OFF = B_OFF + CONV_WIDTH
G_CONV_OFF = C_OFF + CONV_WIDTH
GATE_OFF = G_CONV_OFF + CONV_WIDTH


def _qkv_kernel(x_ref, w_ref, *refs):
  n_lane_tiles = D_MODEL // LANES
  x_refs = (x_ref, w_ref) + refs[:n_lane_tiles - 2]
  w_ref = refs[n_lane_tiles - 2]
  refs = refs[n_lane_tiles - 1:]
  tm = x_ref.shape[0]
  out_refs = refs[:N_GROUPS]
  xp_ref = refs[N_GROUPS:2 * N_GROUPS]
  xl_ref = (x_refs,) + tuple(refs[2 * N_GROUPS:])
  dil = [d for _, d in ATT_GROUPS]
  assert dil[0] == 1
  scale = SCORE_SCALE
  lane_tiles = [pl.ds(c * LANES, LANES) for c in range(n_lane_tiles)]

  def load_rows():
    def piece(c, lanes, rows):
      xp_ref[0][rows, lanes] = x_refs[c][rows, :].astype(BF16)
    quarter = tm // 4
    return [functools.partial(piece, c, lanes, pl.ds(k * quarter, quarter))
            for c, lanes in enumerate(lane_tiles) for k in range(4)]

  def permute_rows(g):
    q = dil[g + 1] // dil[g]
    assert dil[g + 1] == q * dil[g]
    n, n_next = tm // dil[g], tm // dil[g + 1]

    def piece(r, a, c, lanes):
      dst = pl.ds((a * dil[g] + r) * n_next, n_next)
      if g == 0:
        rows = x_refs[c][pl.ds(r * n + a, n_next, stride=q), :]
      else:
        rows = xl_ref[g][c, pl.ds(r * n + a, n_next, stride=q), :]
      xp_ref[g + 1][dst, lanes] = rows.astype(BF16)
      if g + 2 < N_GROUPS:
        xl_ref[g + 1][c, dst, :] = rows
    return [functools.partial(piece, r, a, c, lanes) for r in range(dil[g]) for a in range(q)
            for c, lanes in enumerate(lane_tiles)]

  def project(g, pieces):
    n = tm // dil[g]
    row_chunk = tm // QKV_DOT_ROW_CHUNKS
    n_dots = 3 * QKV_DOT_ROW_CHUNKS
    k = 0
    for sec, off in enumerate((Q_OFF, K_OFF, V_OFF)):
      w_cols = pl.ds(off + g * GROUP_WIDTH, GROUP_WIDTH)
      for rc in range(QKV_DOT_ROW_CHUNKS):
        lo = rc * row_chunk
        res = jnp.dot(xp_ref[g][lo:lo + row_chunk, :], w_ref[:, w_cols],
                      preferred_element_type=F32)
        if sec == 0:
          res = res * scale
        res = res.astype(BF16)
        assert row_chunk % n == 0 or n % row_chunk == 0
        step = min(n, row_chunk)
        for s0 in range(0, row_chunk, step):
          r, l0 = divmod(lo + s0, n)
          out_refs[g][r, l0:l0 + step, pl.ds(QKV_COL_OFF[sec], GROUP_WIDTH)] = res[s0:s0 + step]
        k += 1
        for piece in pieces[(k - 1) * len(pieces) // n_dots:k * len(pieces) // n_dots]:
          piece()

  @pl.when(jnp.logical_and(pl.program_id(0) == 0, pl.program_id(1) == 0))
  def _():
    for piece in load_rows():
      piece()

  for g in range(N_GROUPS):
    @pl.when(pl.program_id(1) == g)
    def _(g=g):
      project(g, permute_rows(g) if g + 1 < N_GROUPS else load_rows())


def _qkv_projection(x2d, w_bf16):
  s = x2d.shape[0]
  tm = QKV_ROW_TILE

  assert (Q_OFF, K_OFF, V_OFF) == (0, ATT_QKV_WIDTH, 2 * ATT_QKV_WIDTH)
  w_spec = pl.BlockSpec((D_MODEL, 3 * ATT_QKV_WIDTH), lambda i, g: (0, 0),
                        pipeline_mode=pl.Buffered(1))

  def x_map(i, g):
    return jnp.minimum(i + g // (N_GROUPS - 1), s // tm - 1), 0

  return pl.pallas_call(
      _qkv_kernel,
      out_shape=[jax.ShapeDtypeStruct((d, s // d, ATT_QKV_WIDTH), BF16) for _, d in ATT_GROUPS],
      grid=(s // tm, N_GROUPS),
      in_specs=[pl.BlockSpec((tm, LANES), lambda i, g, c=c: (x_map(i, g)[0], c))
                for c in range(D_MODEL // LANES)] + [w_spec],
      out_specs=[pl.BlockSpec((d, tm // d, ATT_QKV_WIDTH), lambda i, g: (0, i, 0))
                 for _, d in ATT_GROUPS],
      scratch_shapes=(
          [pltpu.VMEM((tm, D_MODEL), BF16)] * N_GROUPS
          + [pltpu.VMEM((D_MODEL // LANES, tm, LANES), F32)] * (N_GROUPS - 2)),
      compiler_params=pltpu.CompilerParams(
          dimension_semantics=("arbitrary", "arbitrary"),
          vmem_limit_bytes=VMEM_LIMIT_BYTES,
      ),
      name="qkv_projection",
  )(*([x2d] * (D_MODEL // LANES)), w_bf16)


def _attention_tile(cur, prev, bias, store_o):
  nq = ATT_BLOCK
  lane = lax.broadcasted_iota(jnp.int32, (nq, LANES), 1)
  low_half = lane < HEAD_DIM
  head_of_lane = lane % HEADS_PER_GROUP
  ones = jnp.ones((2 * nq, LANES), BF16)
  m_tile = l_tile = None

  for p in range(HEAD_PAIRS):
    q2 = cur(0, p)
    k2 = jnp.concatenate([prev(1, p), cur(1, p)], axis=0)
    v2 = jnp.concatenate([prev(2, p), cur(2, p)], axis=0)
    zero = jnp.zeros_like(q2)
    qs = jnp.concatenate([jnp.where(low_half, q2, zero), jnp.where(low_half, zero, q2)], axis=0)
    sc = lax.dot_general(qs, k2, (((1,), (1,)), ((), ())), preferred_element_type=F32)
    sc = sc + bias
    m = jnp.max(sc, axis=1, keepdims=True)
    e = jnp.exp2(sc - m).astype(BF16)
    ol = jnp.dot(e, jnp.concatenate([v2, ones], axis=1), preferred_element_type=F32)
    store_o(p, jnp.where(low_half, ol[:nq, :LANES], ol[nq:, :LANES]).astype(BF16))
    for k, rows in enumerate((slice(0, nq), slice(nq, 2 * nq))):
      m_h = jnp.broadcast_to(m[rows], (nq, LANES))
      l_h = ol[rows, LANES:]
      if m_tile is None:
        m_tile, l_tile = m_h, l_h
      else:
        own = head_of_lane == 2 * p + k
        m_tile, l_tile = jnp.where(own, m_h, m_tile), jnp.where(own, l_h, l_tile)
  return m_tile, l_tile


def _step_shape(d):
  nt = ATT_TILES_PER_STEP
  bpr = TILES_PER_SUPER // d
  blocks = min(nt, bpr)
  assert nt % blocks == 0 and bpr % blocks == 0
  return nt // blocks, blocks


def _step_tiles(d, sb, j):
  n_res, blocks = _step_shape(d)
  bpr = TILES_PER_SUPER // d
  spr = bpr // blocks
  b0 = (j % spr) * blocks
  res_group = j // spr
  return (res_group, sb * bpr + b0,
          (lambda t: res_group * n_res + t // blocks), (lambda t: b0 + t % blocks))


def _attention_kernel(*refs):
  in_refs = refs[:2 * N_GROUPS]
  o_refs = refs[2 * N_GROUPS:3 * N_GROUPS]
  coef_ref = refs[3 * N_GROUPS]
  max_ref, sum_ref, bias_ref = refs[3 * N_GROUPS + 1:]
  sb = pl.program_id(0)
  j = pl.program_id(1)
  nq = ATT_BLOCK
  nt = ATT_TILES_PER_STEP

  @pl.when(jnp.logical_and(sb == 0, j == 0))
  def _():
    row = lax.broadcasted_iota(jnp.int32, (2 * nq, 2 * nq), 0) % nq
    col = lax.broadcasted_iota(jnp.int32, (2 * nq, 2 * nq), 1)
    for hp in range(2):
      first_col = row if hp else jnp.maximum(row, nq)
      valid = jnp.logical_and(col >= first_col, col - nq <= row)
      bias_ref[hp] = jnp.where(valid, 0.0, -jnp.inf).astype(F32)

  def cols(sec, p):
    return pl.ds(QKV_COL_OFF[sec] + p * LANES, LANES)

  for t in range(nt):
    for g, (_, d) in enumerate(ATT_GROUPS):
      cur_ref, prev_ref = in_refs[2 * g:2 * g + 2]
      o_ref = o_refs[g]
      _, first_blk, res, loc = _step_tiles(d, sb, j)
      ri, bi = divmod(t, _step_shape(d)[1])
      tile_rows = (ri, pl.ds(bi * nq, nq))
      cur = lambda sec, p, tile_rows=tile_rows: cur_ref[tile_rows + (cols(sec, p),)]
      if bi == 0:
        prev = lambda sec, p, ri=ri: prev_ref[ri, :, cols(sec, p)]
        has_prev = (first_blk > 0).astype(jnp.int32)
      else:
        prev = lambda sec, p, ri=ri, bi=bi: cur_ref[ri, (bi - 1) * nq:bi * nq, cols(sec, p)]
        has_prev = 1

      def store_o(p, o, o_ref=o_ref, tile_rows=tile_rows):
        o_ref[tile_rows + (pl.ds(p * LANES, LANES),)] = o

      m_tile, l_tile = _attention_tile(cur, prev, bias_ref[has_prev], store_o)
      r, b = res(t), loc(t)
      if d == 1:
        rows = pl.ds(pl.multiple_of(b * nq, nq), nq)
      else:
        rows = pl.ds(b * nq * d + r, nq, stride=d)
      max_ref[g, rows, :] = m_tile
      sum_ref[g, rows, :] = l_tile

  @pl.when(j == TILES_PER_SUPER // nt - 1)
  def _():
    def body(c, carry):
      rows = pl.ds(pl.multiple_of(c * MERGE_ROWS, MERGE_ROWS), MERGE_ROWS)
      ms = [max_ref[g, rows, :] for g in range(N_GROUPS)]
      top = functools.reduce(jnp.maximum, ms)
      ws = [jnp.exp2(m - top) for m in ms]
      den = functools.reduce(jnp.add, [w * sum_ref[g, rows, :] for g, w in enumerate(ws)])
      group_of_lane = lax.broadcasted_iota(jnp.int32, ws[0].shape, 1) // HEADS_PER_GROUP
      w = functools.reduce(lambda acc, g: jnp.where(group_of_lane == g, ws[g], acc),
                           range(1, N_GROUPS), ws[0])
      coef_ref[rows, :] = w / den
      return carry
    lax.fori_loop(0, SUPER_BLOCK // MERGE_ROWS, body, 0, unroll=2)


def _attention(qkv_groups):
  s = qkv_groups[0].shape[1]
  nt = ATT_TILES_PER_STEP
  in_specs, args, o_specs = [], [], []
  for (_, d), arr in zip(ATT_GROUPS, qkv_groups):
    n_res, blocks = _step_shape(d)

    def cur_map(sb, j, d=d, blocks=blocks):
      res_group, first_blk, _, _ = _step_tiles(d, sb, j)
      return res_group, first_blk // blocks, 0

    def prev_map(sb, j, d=d):
      res_group, first_blk, _, _ = _step_tiles(d, sb, j)
      return res_group, jnp.maximum(first_blk - 1, 0), 0

    in_specs += [pl.BlockSpec((n_res, blocks * ATT_BLOCK, ATT_QKV_WIDTH), cur_map),
                 pl.BlockSpec((n_res, ATT_BLOCK, KV_WIDTH), prev_map)]
    o_specs.append(pl.BlockSpec((n_res, blocks * ATT_BLOCK, GROUP_WIDTH), cur_map))
    args += [arr] * 2
  stats = pltpu.VMEM((N_GROUPS, SUPER_BLOCK, LANES), F32)
  outs = pl.pallas_call(
      _attention_kernel,
      out_shape=([jax.ShapeDtypeStruct((d, s // d, GROUP_WIDTH), BF16) for _, d in ATT_GROUPS]
                 + [jax.ShapeDtypeStruct((s, LANES), F32)]),
      grid=(s // SUPER_BLOCK, TILES_PER_SUPER // nt),
      in_specs=in_specs,
      out_specs=o_specs + [pl.BlockSpec((SUPER_BLOCK, LANES), lambda sb, j: (sb, 0))],
      scratch_shapes=[stats, stats, pltpu.VMEM((2, 2 * ATT_BLOCK, 2 * ATT_BLOCK), F32)],
      compiler_params=pltpu.CompilerParams(
          dimension_semantics=("arbitrary", "arbitrary"),
          vmem_limit_bytes=VMEM_LIMIT_BYTES,
      ),
      name="dilated_attention",
  )(*args)
  return outs[:N_GROUPS], outs[N_GROUPS]


def _silu(x):
  return x * jax.nn.sigmoid(x)


def _tail_kernel(alpha, x_ref, ao1_ref, ao2_ref, ao3_ref, coef_ref,
                 wga_ref, wh_ref, wb_ref, wc_ref, wgc_ref, wg1_ref, wg2_ref,
                 cw_ref, wco32_ref, wao32_ref, bg_ref, wo32_ref, lg_ref, lb_ref, o_ref, u_ref, perm_ref,
                 wco_ref, wao_ref, wo_ref):
  tm = x_ref.shape[0]
  chunk = tm // TAIL_ROW_CHUNKS
  out_refs = (ao1_ref, ao2_ref, ao3_ref)

  def dot(a, w_ref):
    return jnp.dot(a, w_ref[...], preferred_element_type=F32)

  @pl.when(pl.program_id(0) == 0)
  def _():
    u_ref[0:CONV_HALO, :] = jnp.zeros((CONV_HALO, CONV_WIDTH), F32)
    for src_ref, dst_ref in ((wco32_ref, wco_ref), (wao32_ref, wao_ref), (wo32_ref, wo_ref)):
      dst_ref[...] = src_ref[...].astype(BF16)
    row = lax.broadcasted_iota(jnp.int32, (chunk, chunk), 0)
    col = lax.broadcasted_iota(jnp.int32, (chunk, chunk), 1)
    for g, (_, d) in enumerate(ATT_GROUPS):
      src = (row % d) * (chunk // d) + row // d
      perm_ref[g] = jnp.where(col == src, 1.0, 0.0).astype(BF16)

  def attention_mix(lo):
    rows = pl.ds(lo, chunk)
    coef = coef_ref[rows, :]
    att = None
    for g, (_, d) in enumerate(ATT_GROUPS):
      out_ref = out_refs[g]
      n = chunk // d
      if d == 1:
        o_g = out_ref[0, rows, :].astype(F32)
      else:
        res_major = jnp.concatenate([out_ref[r, pl.ds(lo // d, n), :] for r in range(d)], axis=0)
        o_g = jnp.dot(perm_ref[g], res_major, preferred_element_type=F32)
      factor = jnp.concatenate(
          [jnp.broadcast_to(coef[:, g * HEADS_PER_GROUP + h:g * HEADS_PER_GROUP + h + 1],
                            (chunk, HEAD_DIM)) for h in range(HEADS_PER_GROUP)], axis=1)
      att = factor * o_g if att is None else att + factor * o_g
    return att

  def row_chunk(lo):
    rows = pl.ds(lo, chunk)
    xb = x_ref[rows, :].astype(BF16)
    u_ref[pl.ds(CONV_HALO + lo, chunk), :] = dot(xb, wc_ref) * dot(xb, wh_ref)
    att = attention_mix(lo)
    yield
    p_b, p_gc = dot(xb, wb_ref), dot(xb, wgc_ref)
    conv = cw_ref[CONV_K - 1:CONV_K, :] * u_ref[pl.ds(CONV_HALO + lo, chunk), :]
    for k in range(CONV_K - 1):
      off = CONV_HALO + lo - (CONV_K - 1 - k)
      conv = conv + cw_ref[k:k + 1, :] * u_ref[pl.ds(off, chunk), :]
    a_conv = ((p_b * conv) * _silu(p_gc)).astype(BF16)
    yield
    y_conv = dot(a_conv, wco_ref)
    a_att = (att * _silu(dot(xb, wga_ref))).astype(BF16)
    yield
    y_att = dot(a_att, wao_ref)
    m_c = jax.nn.sigmoid(dot(xb, wg1_ref) + bg_ref[:, :D_MODEL]) * y_conv
    yield
    g_a = jax.nn.sigmoid(dot(xb, wg2_ref) + bg_ref[:, D_MODEL:])
    merged = (m_c + g_a * y_att).astype(BF16)
    yield
    y = alpha * x_ref[rows, :] + dot(merged, wo_ref)
    mu = jnp.mean(y, axis=-1, keepdims=True)
    yc = y - mu
    var = jnp.mean(yc * yc, axis=-1, keepdims=True)
    o_ref[rows, :] = yc * lax.rsqrt(var + LN_EPS) * lg_ref[...] + lb_ref[...]
    yield

  for first in range(0, TAIL_ROW_CHUNKS, TAIL_STREAMS):
    streams = [row_chunk(k * chunk) for k in range(first, first + TAIL_STREAMS)]
    for stage in range(TAIL_STAGES):
      for stream in streams:
        next(stream)
      if stage == 1 and first + TAIL_STREAMS == TAIL_ROW_CHUNKS:
        u_ref[0:CONV_HALO, :] = u_ref[tm:tm + CONV_HALO, :]


def _tail(x2d, att_outs, att_coefs, w_bf16, conv_w, w_conv_out, w_att_out, b_gate, w_o, ln_g, ln_b,
          alpha):
  s = x2d.shape[0]
  tm = TAIL_ROW_TILE
  chunk = tm // TAIL_ROW_CHUNKS
  assert all(chunk % (d * BF16_SUBLANES) == 0 for _, d in ATT_GROUPS)

  def whole(arr):
    return pl.BlockSpec(arr.shape, lambda i: (0,) * arr.ndim, pipeline_mode=pl.Buffered(1))

  def w_cols(off, width):
    assert off % width == 0
    return pl.BlockSpec((D_MODEL, width), lambda i: (0, off // width), pipeline_mode=pl.Buffered(1))

  w_specs = [w_cols(G_ATT_OFF, GROUP_WIDTH)] + [
      w_cols(off, CONV_WIDTH)
      for off in (H_OFF, B_OFF, C_OFF, G_CONV_OFF, GATE_OFF, GATE_OFF + D_MODEL)]
  others = (conv_w, w_conv_out, w_att_out, b_gate, w_o, ln_g, ln_b)
  return pl.pallas_call(
      functools.partial(_tail_kernel, alpha),
      out_shape=jax.ShapeDtypeStruct((s, D_MODEL), F32),
      grid=(s // tm,),
      in_specs=[pl.BlockSpec((tm, D_MODEL), lambda i: (i, 0))]
      + [pl.BlockSpec((d, tm // d, GROUP_WIDTH), lambda i: (0, i, 0)) for _, d in ATT_GROUPS]
      + [pl.BlockSpec((tm, LANES), lambda i: (i, 0))]
      + w_specs + [whole(w) for w in others],
      out_specs=pl.BlockSpec((tm, D_MODEL), lambda i: (i, 0)),
      scratch_shapes=[pltpu.VMEM((CONV_HALO + tm, CONV_WIDTH), F32),
                      pltpu.VMEM((N_GROUPS, chunk, chunk), BF16)]
      + [pltpu.VMEM(w.shape, BF16) for w in (w_conv_out, w_att_out, w_o)],
      compiler_params=pltpu.CompilerParams(
          dimension_semantics=("arbitrary",),
          vmem_limit_bytes=VMEM_LIMIT_BYTES,
      ),
      name="conv_merge_norm",
  )(x2d, *att_outs, att_coefs, *([w_bf16] * len(w_specs)), *others)


def _layer(x2d, w_in, conv_w, w_conv_out, w_att_out, b_gate, w_o, ln_g, ln_b, alpha):
  assert w_in.shape == (D_MODEL, GATE_OFF + 2 * D_MODEL)
  w_bf16 = w_in.astype(BF16)
  qkv_groups = _qkv_projection(x2d, w_bf16)
  att_outs, att_coefs = _attention(qkv_groups)
  return _tail(x2d, att_outs, att_coefs, w_bf16, conv_w, w_conv_out, w_att_out,
               b_gate.reshape(1, -1), w_o, ln_g.reshape(1, -1), ln_b.reshape(1, -1), alpha)


def kernel(x, w_in, conv_w, w_conv_out, w_att_out, b_gate, w_o, ln_g, ln_b):
  batch, seq, d_model = x.shape
  depth = w_in.shape[0]
  assert d_model == D_MODEL and seq % SUPER_BLOCK == 0
  alpha = (2.0 * depth) ** 0.25
  outs = []
  for b in range(batch):
    h = x[b]
    for layer in range(depth):
      h = _layer(h, w_in[layer], conv_w[layer], w_conv_out[layer], w_att_out[layer],
                 b_gate[layer], w_o[layer], ln_g[layer], ln_b[layer], alpha)
    outs.append(h)
  return jnp.stack(outs)
```

```python
import functools

import jax
import jax.numpy as jnp
from jax import lax
from jax.experimental import pallas as pl
from jax.experimental.pallas import tpu as pltpu

D_MODEL = 1024
HEAD_DIM = 64
HEADS_PER_GROUP = 8
ATT_GROUPS = ((128, 1), (512, 4), (2048, 16))
N_GROUPS = len(ATT_GROUPS)
GROUP_WIDTH = HEADS_PER_GROUP * HEAD_DIM
ATT_QKV_WIDTH = N_GROUPS * GROUP_WIDTH
CONV_WIDTH = D_MODEL
CONV_K = 3
LN_EPS = 1e-5
LANES = 128
BF16_SUBLANES = 16

ATT_BLOCK = 128
assert all(w // d == ATT_BLOCK for w, d in ATT_GROUPS)
SUPER_BLOCK = ATT_BLOCK * max(d for _, d in ATT_GROUPS)
TILES_PER_SUPER = SUPER_BLOCK // ATT_BLOCK
ATT_TILES_PER_STEP = 8

QKV_ROW_TILE = 1024
QKV_DOT_ROW_CHUNKS = 2
TAIL_ROW_TILE = 512
TAIL_ROW_CHUNKS = 2
TAIL_STREAMS = 2
assert TAIL_ROW_CHUNKS % TAIL_STREAMS == 0
TAIL_STAGES = 6
CONV_HALO = 8
MERGE_ROWS = 32
HEAD_PAIRS = HEADS_PER_GROUP // 2
assert 2 * HEAD_DIM == LANES
KV_WIDTH = 2 * GROUP_WIDTH
QKV_COL_OFF = (KV_WIDTH, 0, GROUP_WIDTH)
SCORE_SCALE = HEAD_DIM ** -0.5 * 1.4426950408889634

assert N_GROUPS * HEADS_PER_GROUP <= LANES
VMEM_LIMIT_BYTES = 56 * 1024 * 1024

F32 = jnp.float32
BF16 = jnp.bfloat16

Q_OFF, K_OFF, V_OFF = 0, ATT_QKV_WIDTH, 2 * ATT_QKV_WIDTH
G_ATT_OFF = 3 * ATT_QKV_WIDTH
H_OFF = G_ATT_OFF + GROUP_WIDTH
B_OFF = H_OFF + CONV_WIDTH
C_OFF = B_OFF + CONV_WIDTH
G_CONV_OFF = C_OFF + CONV_WIDTH
GATE_OFF = G_CONV_OFF + CONV_WIDTH


def _qkv_kernel(x_ref, w_ref, *refs):
  tm = x_ref.shape[0]
  out_refs = refs[:N_GROUPS]
  xp_ref = refs[N_GROUPS:2 * N_GROUPS]
  xl_ref = refs[2 * N_GROUPS:]
  dil = [d for _, d in ATT_GROUPS]
  assert dil[0] == 1
  scale = SCORE_SCALE
  lane_tiles = [pl.ds(c * LANES, LANES) for c in range(D_MODEL // LANES)]

  def load_rows():
    def piece(c, lanes, rows):
      xp_ref[0][rows, lanes] = x_ref[rows, lanes].astype(BF16)
      xl_ref[0][c, rows, :] = x_ref[rows, lanes]
    quarter = tm // 4
    return [functools.partial(piece, c, lanes, pl.ds(k * quarter, quarter))
            for c, lanes in enumerate(lane_tiles) for k in range(4)]

  def permute_rows(g):
    q = dil[g + 1] // dil[g]
    assert dil[g + 1] == q * dil[g]
    n, n_next = tm // dil[g], tm // dil[g + 1]

    def piece(r, a, c, lanes):
      dst = pl.ds((a * dil[g] + r) * n_next, n_next)
      rows = xl_ref[g][c, pl.ds(r * n + a, n_next, stride=q), :]
      xp_ref[g + 1][dst, lanes] = rows.astype(BF16)
      if g + 2 < N_GROUPS:
        xl_ref[g + 1][c, dst, :] = rows
    return [functools.partial(piece, r, a, c, lanes) for r in range(dil[g]) for a in range(q)
            for c, lanes in enumerate(lane_tiles)]

  def project(g, pieces):
    n = tm // dil[g]
    row_chunk = tm // QKV_DOT_ROW_CHUNKS
    n_dots = 3 * QKV_DOT_ROW_CHUNKS
    k = 0
    for sec, off in enumerate((Q_OFF, K_OFF, V_OFF)):
      w_cols = pl.ds(off + g * GROUP_WIDTH, GROUP_WIDTH)
      for rc in range(QKV_DOT_ROW_CHUNKS):
        lo = rc * row_chunk
        res = jnp.dot(xp_ref[g][lo:lo + row_chunk, :], w_ref[:, w_cols],
                      preferred_element_type=F32)
        if sec == 0:
          res = res * scale
        res = res.astype(BF16)
        assert row_chunk % n == 0 or n % row_chunk == 0
        step = min(n, row_chunk)
        for s0 in range(0, row_chunk, step):
          r, l0 = divmod(lo + s0, n)
          out_refs[g][r, l0:l0 + step, pl.ds(QKV_COL_OFF[sec], GROUP_WIDTH)] = res[s0:s0 + step]
        k += 1
        for piece in pieces[(k - 1) * len(pieces) // n_dots:k * len(pieces) // n_dots]:
          piece()

  @pl.when(jnp.logical_and(pl.program_id(0) == 0, pl.program_id(1) == 0))
  def _():
    for piece in load_rows():
      piece()

  for g in range(N_GROUPS):
    @pl.when(pl.program_id(1) == g)
    def _(g=g):
      project(g, permute_rows(g) if g + 1 < N_GROUPS else load_rows())


def _qkv_projection(x2d, w_bf16):
  s = x2d.shape[0]
  tm = QKV_ROW_TILE

  assert (Q_OFF, K_OFF, V_OFF) == (0, ATT_QKV_WIDTH, 2 * ATT_QKV_WIDTH)
  w_spec = pl.BlockSpec((D_MODEL, 3 * ATT_QKV_WIDTH), lambda i, g: (0, 0),
                        pipeline_mode=pl.Buffered(1))

  def x_map(i, g):
    return jnp.minimum(i + g // (N_GROUPS - 1), s // tm - 1), 0

  return pl.pallas_call(
      _qkv_kernel,
      out_shape=[jax.ShapeDtypeStruct((d, s // d, ATT_QKV_WIDTH), BF16) for _, d in ATT_GROUPS],
      grid=(s // tm, N_GROUPS),
      in_specs=[pl.BlockSpec((tm, D_MODEL), x_map), w_spec],
      out_specs=[pl.BlockSpec((d, tm // d, ATT_QKV_WIDTH), lambda i, g: (0, i, 0))
                 for _, d in ATT_GROUPS],
      scratch_shapes=(
          [pltpu.VMEM((tm, D_MODEL), BF16)] * N_GROUPS
          + [pltpu.VMEM((D_MODEL // LANES, tm, LANES), F32)] * (N_GROUPS - 1)),
      compiler_params=pltpu.CompilerParams(
          dimension_semantics=("arbitrary", "arbitrary"),
          vmem_limit_bytes=VMEM_LIMIT_BYTES,
      ),
      name="qkv_projection",
  )(x2d, w_bf16)


def _attention_tile(cur, prev, bias, store_o):
  nq = ATT_BLOCK
  lane = lax.broadcasted_iota(jnp.int32, (nq, LANES), 1)
  low_half = lane < HEAD_DIM
  head_of_lane = lane % HEADS_PER_GROUP
  ones = jnp.ones((2 * nq, LANES), BF16)
  m_tile = l_tile = None

  for p in range(HEAD_PAIRS):
    q2 = cur(0, p)
    k2 = jnp.concatenate([prev(1, p), cur(1, p)], axis=0)
    v2 = jnp.concatenate([prev(2, p), cur(2, p)], axis=0)
    zero = jnp.zeros_like(q2)
    qs = jnp.concatenate([jnp.where(low_half, q2, zero), jnp.where(low_half, zero, q2)], axis=0)
    sc = lax.dot_general(qs, k2, (((1,), (1,)), ((), ())), preferred_element_type=F32)
    sc = sc + bias
    m = jnp.max(sc, axis=1, keepdims=True)
    e = jnp.exp2(sc - m).astype(BF16)
    ol = jnp.dot(e, jnp.concatenate([v2, ones], axis=1), preferred_element_type=F32)
    store_o(p, jnp.where(low_half, ol[:nq, :LANES], ol[nq:, :LANES]).astype(BF16))
    for k, rows in enumerate((slice(0, nq), slice(nq, 2 * nq))):
      m_h = jnp.broadcast_to(m[rows], (nq, LANES))
      l_h = ol[rows, LANES:]
      if m_tile is None:
        m_tile, l_tile = m_h, l_h
      else:
        own = head_of_lane == 2 * p + k
        m_tile, l_tile = jnp.where(own, m_h, m_tile), jnp.where(own, l_h, l_tile)
  return m_tile, l_tile


def _step_shape(d):
  nt = ATT_TILES_PER_STEP
  bpr = TILES_PER_SUPER // d
  blocks = min(nt, bpr)
  assert nt % blocks == 0 and bpr % blocks == 0
  return nt // blocks, blocks


def _step_tiles(d, sb, j):
  n_res, blocks = _step_shape(d)
  bpr = TILES_PER_SUPER // d
  spr = bpr // blocks
  b0 = (j % spr) * blocks
  res_group = j // spr
  return (res_group, sb * bpr + b0,
          (lambda t: res_group * n_res + t // blocks), (lambda t: b0 + t % blocks))


def _attention_kernel(*refs):
  in_refs = refs[:2 * N_GROUPS]
  o_refs = refs[2 * N_GROUPS:3 * N_GROUPS]
  coef_ref = refs[3 * N_GROUPS]
  max_ref, sum_ref, bias_ref = refs[3 * N_GROUPS + 1:]
  sb = pl.program_id(0)
  j = pl.program_id(1)
  nq = ATT_BLOCK
  nt = ATT_TILES_PER_STEP

  @pl.when(jnp.logical_and(sb == 0, j == 0))
  def _():
    row = lax.broadcasted_iota(jnp.int32, (2 * nq, 2 * nq), 0) % nq
    col = lax.broadcasted_iota(jnp.int32, (2 * nq, 2 * nq), 1)
    for hp in range(2):
      first_col = row if hp else jnp.maximum(row, nq)
      valid = jnp.logical_and(col >= first_col, col - nq <= row)
      bias_ref[hp] = jnp.where(valid, 0.0, -jnp.inf).astype(F32)

  def cols(sec, p):
    return pl.ds(QKV_COL_OFF[sec] + p * LANES, LANES)

  for t in range(nt):
    for g, (_, d) in enumerate(ATT_GROUPS):
      cur_ref, prev_ref = in_refs[2 * g:2 * g + 2]
      o_ref = o_refs[g]
      _, first_blk, res, loc = _step_tiles(d, sb, j)
      ri, bi = divmod(t, _step_shape(d)[1])
      tile_rows = (ri, pl.ds(bi * nq, nq))
      cur = lambda sec, p, tile_rows=tile_rows: cur_ref[tile_rows + (cols(sec, p),)]
      if bi == 0:
        prev = lambda sec, p, ri=ri: prev_ref[ri, :, cols(sec, p)]
        has_prev = (first_blk > 0).astype(jnp.int32)
      else:
        prev = lambda sec, p, ri=ri, bi=bi: cur_ref[ri, (bi - 1) * nq:bi * nq, cols(sec, p)]
        has_prev = 1

      def store_o(p, o, o_ref=o_ref, tile_rows=tile_rows):
        o_ref[tile_rows + (pl.ds(p * LANES, LANES),)] = o

      m_tile, l_tile = _attention_tile(cur, prev, bias_ref[has_prev], store_o)
      r, b = res(t), loc(t)
      if d == 1:
        rows = pl.ds(pl.multiple_of(b * nq, nq), nq)
      else:
        rows = pl.ds(b * nq * d + r, nq, stride=d)
      max_ref[g, rows, :] = m_tile
      sum_ref[g, rows, :] = l_tile

  @pl.when(j == TILES_PER_SUPER // nt - 1)
  def _():
    def body(c, carry):
      rows = pl.ds(pl.multiple_of(c * MERGE_ROWS, MERGE_ROWS), MERGE_ROWS)
      ms = [max_ref[g, rows, :] for g in range(N_GROUPS)]
      top = functools.reduce(jnp.maximum, ms)
      ws = [jnp.exp2(m - top) for m in ms]
      den = functools.reduce(jnp.add, [w * sum_ref[g, rows, :] for g, w in enumerate(ws)])
      group_of_lane = lax.broadcasted_iota(jnp.int32, ws[0].shape, 1) // HEADS_PER_GROUP
      w = functools.reduce(lambda acc, g: jnp.where(group_of_lane == g, ws[g], acc),
                           range(1, N_GROUPS), ws[0])
      coef_ref[rows, :] = w / den
      return carry
    lax.fori_loop(0, SUPER_BLOCK // MERGE_ROWS, body, 0, unroll=2)


def _attention(qkv_groups):
  s = qkv_groups[0].shape[1]
  nt = ATT_TILES_PER_STEP
  in_specs, args, o_specs = [], [], []
  for (_, d), arr in zip(ATT_GROUPS, qkv_groups):
    n_res, blocks = _step_shape(d)

    def cur_map(sb, j, d=d, blocks=blocks):
      res_group, first_blk, _, _ = _step_tiles(d, sb, j)
      return res_group, first_blk // blocks, 0

    def prev_map(sb, j, d=d):
      res_group, first_blk, _, _ = _step_tiles(d, sb, j)
      return res_group, jnp.maximum(first_blk - 1, 0), 0

    in_specs += [pl.BlockSpec((n_res, blocks * ATT_BLOCK, ATT_QKV_WIDTH), cur_map),
                 pl.BlockSpec((n_res, ATT_BLOCK, KV_WIDTH), prev_map)]
    o_specs.append(pl.BlockSpec((n_res, blocks * ATT_BLOCK, GROUP_WIDTH), cur_map))
    args += [arr] * 2
  stats = pltpu.VMEM((N_GROUPS, SUPER_BLOCK, LANES), F32)
  outs = pl.pallas_call(
      _attention_kernel,
      out_shape=([jax.ShapeDtypeStruct((d, s // d, GROUP_WIDTH), BF16) for _, d in ATT_GROUPS]
                 + [jax.ShapeDtypeStruct((s, LANES), F32)]),
      grid=(s // SUPER_BLOCK, TILES_PER_SUPER // nt),
      in_specs=in_specs,
      out_specs=o_specs + [pl.BlockSpec((SUPER_BLOCK, LANES), lambda sb, j: (sb, 0))],
      scratch_shapes=[stats, stats, pltpu.VMEM((2, 2 * ATT_BLOCK, 2 * ATT_BLOCK), F32)],
      compiler_params=pltpu.CompilerParams(
          dimension_semantics=("arbitrary", "arbitrary"),
          vmem_limit_bytes=VMEM_LIMIT_BYTES,
      ),
      name="dilated_attention",
  )(*args)
  return outs[:N_GROUPS], outs[N_GROUPS]


def _silu(x):
  return x * jax.nn.sigmoid(x)


def _tail_kernel(alpha, x_ref, ao1_ref, ao2_ref, ao3_ref, coef_ref,
                 wga_ref, wh_ref, wb_ref, wc_ref, wgc_ref, wg1_ref, wg2_ref,
                 cw_ref, wco32_ref, wao32_ref, bg_ref, wo32_ref, lg_ref, lb_ref, o_ref, u_ref, perm_ref,
                 wco_ref, wao_ref, wo_ref):
  tm = x_ref.shape[0]
  chunk = tm // TAIL_ROW_CHUNKS
  out_refs = (ao1_ref, ao2_ref, ao3_ref)

  def dot(a, w_ref):
    return jnp.dot(a, w_ref[...], preferred_element_type=F32)

  @pl.when(pl.program_id(0) == 0)
  def _():
    u_ref[0:CONV_HALO, :] = jnp.zeros((CONV_HALO, CONV_WIDTH), F32)
    for src_ref, dst_ref in ((wco32_ref, wco_ref), (wao32_ref, wao_ref), (wo32_ref, wo_ref)):
      dst_ref[...] = src_ref[...].astype(BF16)
    row = lax.broadcasted_iota(jnp.int32, (chunk, chunk), 0)
    col = lax.broadcasted_iota(jnp.int32, (chunk, chunk), 1)
    for g, (_, d) in enumerate(ATT_GROUPS):
      src = (row % d) * (chunk // d) + row // d
      perm_ref[g] = jnp.where(col == src, 1.0, 0.0).astype(BF16)

  def attention_mix(lo):
    rows = pl.ds(lo, chunk)
    coef = coef_ref[rows, :]
    att = None
    for g, (_, d) in enumerate(ATT_GROUPS):
      out_ref = out_refs[g]
      n = chunk // d
      if d == 1:
        o_g = out_ref[0, rows, :].astype(F32)
      else:
        res_major = jnp.concatenate([out_ref[r, pl.ds(lo // d, n), :] for r in range(d)], axis=0)
        o_g = jnp.dot(perm_ref[g], res_major, preferred_element_type=F32)
      factor = jnp.concatenate(
          [jnp.broadcast_to(coef[:, g * HEADS_PER_GROUP + h:g * HEADS_PER_GROUP + h + 1],
                            (chunk, HEAD_DIM)) for h in range(HEADS_PER_GROUP)], axis=1)
      att = factor * o_g if att is None else att + factor * o_g
    return att

  def row_chunk(lo):
    rows = pl.ds(lo, chunk)
    xb = x_ref[rows, :].astype(BF16)
    att = attention_mix(lo)
    u_ref[pl.ds(CONV_HALO + lo, chunk), :] = dot(xb, wc_ref) * dot(xb, wh_ref)
    yield
    p_b, p_gc = dot(xb, wb_ref), dot(xb, wgc_ref)
    conv = cw_ref[CONV_K - 1:CONV_K, :] * u_ref[pl.ds(CONV_HALO + lo, chunk), :]
    for k in range(CONV_K - 1):
      off = CONV_HALO + lo - (CONV_K - 1 - k)
      conv = conv + cw_ref[k:k + 1, :] * u_ref[pl.ds(off, chunk), :]
    a_conv = ((p_b * conv) * _silu(p_gc)).astype(BF16)
    yield
    y_conv = dot(a_conv, wco_ref)
    a_att = (att * _silu(dot(xb, wga_ref))).astype(BF16)
    yield
    y_att = dot(a_att, wao_ref)
    m_c = jax.nn.sigmoid(dot(xb, wg1_ref) + bg_ref[:, :D_MODEL]) * y_conv
    yield
    g_a = jax.nn.sigmoid(dot(xb, wg2_ref) + bg_ref[:, D_MODEL:])
    merged = (m_c + g_a * y_att).astype(BF16)
    yield
    y = alpha * x_ref[rows, :] + dot(merged, wo_ref)
    mu = jnp.mean(y, axis=-1, keepdims=True)
    yc = y - mu
    var = jnp.mean(yc * yc, axis=-1, keepdims=True)
    o_ref[rows, :] = yc * lax.rsqrt(var + LN_EPS) * lg_ref[...] + lb_ref[...]
    yield

  for first in range(0, TAIL_ROW_CHUNKS, TAIL_STREAMS):
    streams = [row_chunk(k * chunk) for k in range(first, first + TAIL_STREAMS)]
    for stage in range(TAIL_STAGES):
      for stream in streams:
        next(stream)
      if stage == 1 and first + TAIL_STREAMS == TAIL_ROW_CHUNKS:
        u_ref[0:CONV_HALO, :] = u_ref[tm:tm + CONV_HALO, :]


def _tail(x2d, att_outs, att_coefs, w_bf16, conv_w, w_conv_out, w_att_out, b_gate, w_o, ln_g, ln_b,
          alpha):
  s = x2d.shape[0]
  tm = TAIL_ROW_TILE
  chunk = tm // TAIL_ROW_CHUNKS
  assert all(chunk % (d * BF16_SUBLANES) == 0 for _, d in ATT_GROUPS)

  def whole(arr):
    return pl.BlockSpec(arr.shape, lambda i: (0,) * arr.ndim, pipeline_mode=pl.Buffered(1))

  def w_cols(off, width):
    assert off % width == 0
    return pl.BlockSpec((D_MODEL, width), lambda i: (0, off // width), pipeline_mode=pl.Buffered(1))

  w_specs = [w_cols(G_ATT_OFF, GROUP_WIDTH)] + [
      w_cols(off, CONV_WIDTH)
      for off in (H_OFF, B_OFF, C_OFF, G_CONV_OFF, GATE_OFF, GATE_OFF + D_MODEL)]
  others = (conv_w, w_conv_out, w_att_out, b_gate, w_o, ln_g, ln_b)
  return pl.pallas_call(
      functools.partial(_tail_kernel, alpha),
      out_shape=jax.ShapeDtypeStruct((s, D_MODEL), F32),
      grid=(s // tm,),
      in_specs=[pl.BlockSpec((tm, D_MODEL), lambda i: (i, 0))]
      + [pl.BlockSpec((d, tm // d, GROUP_WIDTH), lambda i: (0, i, 0)) for _, d in ATT_GROUPS]
      + [pl.BlockSpec((tm, LANES), lambda i: (i, 0))]
      + w_specs + [whole(w) for w in others],
      out_specs=pl.BlockSpec((tm, D_MODEL), lambda i: (i, 0)),
      scratch_shapes=[pltpu.VMEM((CONV_HALO + tm, CONV_WIDTH), F32),
                      pltpu.VMEM((N_GROUPS, chunk, chunk), BF16)]
      + [pltpu.VMEM(w.shape, BF16) for w in (w_conv_out, w_att_out, w_o)],
      compiler_params=pltpu.CompilerParams(
          dimension_semantics=("arbitrary",),
          vmem_limit_bytes=VMEM_LIMIT_BYTES,
      ),
      name="conv_merge_norm",
  )(x2d, *att_outs, att_coefs, *([w_bf16] * len(w_specs)), *others)


def _layer(x2d, w_in, conv_w, w_conv_out, w_att_out, b_gate, w_o, ln_g, ln_b, alpha):
  assert w_in.shape == (D_MODEL, GATE_OFF + 2 * D_MODEL)
  w_bf16 = w_in.astype(BF16)
  qkv_groups = _qkv_projection(x2d, w_bf16)
  att_outs, att_coefs = _attention(qkv_groups)
  return _tail(x2d, att_outs, att_coefs, w_bf16, conv_w, w_conv_out, w_att_out,
               b_gate.reshape(1, -1), w_o, ln_g.reshape(1, -1), ln_b.reshape(1, -1), alpha)


def kernel(x, w_in, conv_w, w_conv_out, w_att_out, b_gate, w_o, ln_g, ln_b):
  batch, seq, d_model = x.shape
  depth = w_in.shape[0]
  assert d_model == D_MODEL and seq % SUPER_BLOCK == 0
  alpha = (2.0 * depth) ** 0.25
  outs = []
  for b in range(batch):
    h = x[b]
    for layer in range(depth):
      h = _layer(h, w_in[layer], conv_w[layer], w_conv_out[layer], w_att_out[layer],
                 b_gate[layer], w_o[layer], ln_g[layer], ln_b[layer], alpha)
    outs.append(h)
  return jnp.stack(outs)
```
